```python
import jax, jax.numpy as jnp
from jax import lax
import numpy as np

D_MODEL = 1024
BATCH = 8
SEQ = 4096
DEPTH = 1

HG_HEADS = 8
HG_DK = 128
HG_DV = 128
HG_QK = HG_HEADS * HG_DK
HG_V = HG_HEADS * HG_DV
HG_CHUNK = 32
SB_HEADS = 8
SB_DH = 128
SB_W = SB_HEADS * SB_DH
SB_BLOCK = 128
IN_SPLITS = (HG_QK, HG_QK, HG_V, HG_V, SB_W, SB_W, SB_W, 2 * D_MODEL)
N_EXPERTS = 256
TOP_K = 8
N_GROUPS = 8
TOPK_GROUPS = 4
D_EXPERT = 256
D_SHARED = 256
ROUTED_SCALE = 2.5
MOE_BLOCK = 128
LN_EPS = 1e-5
RMS_EPS = 1e-6
DEEPNORM_ALPHA = (2.0 * DEPTH) ** 0.25
DEEPNORM_BETA = (8.0 * DEPTH) ** -0.25

kernel_name = 'hybrid_hgrn2_stickbreaking_moe_deepnorm'


def layer_norm(x, g, b):
    x32 = x.astype(jnp.float32)
    mu = jnp.mean(x32, axis=-1, keepdims=True)
    var = jnp.mean(jnp.square(x32 - mu), axis=-1, keepdims=True)
    y = (x32 - mu) * lax.rsqrt(var + LN_EPS) * g.astype(jnp.float32) + b.astype(jnp.float32)
    return y.astype(x.dtype)


def hgrn2_chunked(q, log_f, k, v):
    B, S, H, DK = q.shape
    DV = v.shape[-1]
    C = HG_CHUNK
    N = S // C

    def to_chunks(t):
        return t.reshape(B, N, C, H, t.shape[-1]).transpose(1, 0, 3, 2, 4)

    q, log_f, k, v = to_chunks(q), to_chunks(log_f), to_chunks(k), to_chunks(v)
    b = jnp.cumsum(log_f, axis=-2)
    b_end = b[..., -1:, :]
    q_start = q * jnp.exp(b)
    q_end = q * jnp.exp(b - b_end)
    k_end = k * jnp.exp(b_end - b)
    causal = jnp.tril(jnp.ones((C, C), dtype=bool))
    scores = jnp.where(causal, jnp.einsum('nbhtd,nbhsd->nbhts', q_end, k_end), 0.0)
    o_intra = jnp.einsum('nbhts,nbhsv->nbhtv', scores, v)

    def step(state, xs):
        q_s, k_e, v_c, dec = xs
        o = jnp.einsum('bhtd,bhdv->bhtv', q_s, state)
        state = jnp.exp(dec)[..., 0, :, None] * state + jnp.einsum('bhsd,bhsv->bhdv', k_e, v_c)
        return state, o

    state0 = jnp.zeros((B, H, DK, DV), jnp.float32)
    _, o_inter = lax.scan(step, state0, (q_start, k_end, v, b_end))
    o = o_intra + o_inter
    return o.transpose(1, 0, 3, 2, 4).reshape(B, S, H, DV)


def stick_breaking_attention(q, k, v):
    B, H, S, dh = q.shape
    nb = S // SB_BLOCK
    scale = dh ** -0.5
    key_pos = jnp.arange(S)
    q_blocks = q.reshape(B, H, nb, SB_BLOCK, dh).transpose(2, 0, 1, 3, 4)

    def block(args):
        qb, i = args
        z = jnp.einsum('bhtd,bhsd->bhts', qb, k) * scale
        qpos = i * SB_BLOCK + jnp.arange(SB_BLOCK)
        before = key_pos[None, :] < qpos[:, None]
        log_keep = jnp.where(before, jax.nn.log_sigmoid(-z), 0.0)
        stick = lax.cumsum(log_keep, axis=3, reverse=True) - log_keep
        w = jnp.where(before, jnp.exp(jax.nn.log_sigmoid(z) + stick), 0.0)
        return jnp.einsum('bhts,bhsd->bhtd', w, v)

    o = lax.map(block, (q_blocks, jnp.arange(nb)))
    return o.transpose(1, 2, 0, 3, 4).reshape(B, H, S, dh)


def hybrid_mixer(x, w_in, lb, hg_norm_g, w_branch_hg, w_branch_sb, w_out):
    B, S, D = x.shape
    proj = x @ w_in
    offsets = np.cumsum(IN_SPLITS)[:-1].tolist()
    hq, hf, hi, hg, sq, sk, sv, gates = jnp.split(proj, offsets, axis=-1)

    hf32 = hf.astype(jnp.float32)
    log_f = jnp.log(lb + (1.0 - lb) * jax.nn.sigmoid(hf32))
    k_in = (1.0 - lb) * jax.nn.sigmoid(-hf32)
    shp = (B, S, HG_HEADS, HG_DK)
    o_hg = hgrn2_chunked(hq.astype(jnp.float32).reshape(shp), log_f.reshape(shp),
                         k_in.reshape(shp), hi.astype(jnp.float32).reshape(B, S, HG_HEADS, HG_DV))
    o_hg = o_hg * lax.rsqrt(jnp.mean(jnp.square(o_hg), axis=-1, keepdims=True) + RMS_EPS)
    o_hg = o_hg * hg_norm_g.astype(jnp.float32).reshape(HG_HEADS, HG_DV)
    o_hg = (o_hg.reshape(B, S, HG_V) * jax.nn.silu(hg.astype(jnp.float32))).astype(x.dtype)
    y_hg = o_hg @ w_branch_hg

    def heads(t):
        return t.astype(jnp.float32).reshape(B, S, SB_HEADS, SB_DH).transpose(0, 2, 1, 3)
    o_sb = stick_breaking_attention(heads(sq), heads(sk), heads(sv))
    o_sb = o_sb.transpose(0, 2, 1, 3).reshape(B, S, SB_W).astype(x.dtype)
    y_sb = o_sb @ w_branch_sb

    g_hg, g_sb = jnp.split(gates, 2, axis=-1)
    merged = jax.nn.sigmoid(g_hg) * y_hg + jax.nn.sigmoid(g_sb) * y_sb
    return merged @ w_out


def route(xf, w_router, router_bias):
    T = xf.shape[0]
    scores = jax.nn.sigmoid(xf.astype(jnp.float32) @ w_router.astype(jnp.float32))
    biased = scores + router_bias.astype(jnp.float32)
    grp = biased.reshape(T, N_GROUPS, N_EXPERTS // N_GROUPS)
    grp_score = lax.top_k(grp, 2)[0].sum(-1)
    _, top_grp = lax.top_k(grp_score, TOPK_GROUPS)
    grp_mask = jax.nn.one_hot(top_grp, N_GROUPS, dtype=jnp.float32).sum(1) > 0
    expert_mask = jnp.repeat(grp_mask, N_EXPERTS // N_GROUPS, axis=1)
    masked = jnp.where(expert_mask, biased, jnp.finfo(jnp.float32).min)
    _, idx = lax.top_k(masked, TOP_K)
    w = jnp.take_along_axis(scores, idx, axis=1)
    w = w / jnp.sum(w, axis=-1, keepdims=True) * ROUTED_SCALE
    return idx, w


def routed_experts(xf, idx, w, w_gate, w_up, w_down):
    T, D = xf.shape
    A = T * TOP_K
    e_flat = idx.reshape(A).astype(jnp.int32)
    tok_flat = (jnp.arange(A, dtype=jnp.int32) // TOP_K)
    w_flat = w.reshape(A)
    order = jnp.argsort(e_flat)
    e_s, tok_s, w_s = e_flat[order], tok_flat[order], w_flat[order]
    counts = jax.ops.segment_sum(jnp.ones((A,), jnp.int32), e_flat, num_segments=N_EXPERTS)
    start = jnp.cumsum(counts) - counts
    padded = (counts + MOE_BLOCK - 1) // MOE_BLOCK * MOE_BLOCK
    pad_end = jnp.cumsum(padded)
    pad_start = pad_end - padded
    dest = pad_start[e_s] + (jnp.arange(A, dtype=jnp.int32) - start[e_s])
    P = A + N_EXPERTS * MOE_BLOCK
    n_blocks = P // MOE_BLOCK
    row_tok = jnp.full((P,), T, jnp.int32).at[dest].set(tok_s)
    row_w = jnp.zeros((P,), jnp.float32).at[dest].set(w_s)
    block_expert = jnp.minimum(
        jnp.searchsorted(pad_end, jnp.arange(n_blocks, dtype=jnp.int32) * MOE_BLOCK, side='right'),
        N_EXPERTS - 1)

    def step(out, xs):
        toks, wts, e = xs
        xb = jnp.take(xf, toks, axis=0, mode='fill', fill_value=0)
        h = jax.nn.silu(xb @ w_gate[e]) * (xb @ w_up[e])
        yb = ((h @ w_down[e]) * wts[:, None]).astype(out.dtype)
        return out.at[toks].add(yb, mode='drop'), None

    out0 = jnp.zeros((T, D), xf.dtype)
    out, _ = lax.scan(step, out0, (row_tok.reshape(n_blocks, MOE_BLOCK),
                                   row_w.reshape(n_blocks, MOE_BLOCK), block_expert))
    return out


def moe(x, w_router, router_bias, w_exp_gate, w_exp_up, w_exp_down, w_sh_gate, w_sh_up, w_sh_down):
    B, S, D = x.shape
    xf = x.reshape(B * S, D)
    idx, w = route(xf, w_router, router_bias)
    routed = routed_experts(xf, idx, w, w_exp_gate, w_exp_up, w_exp_down)
    shared = (jax.nn.silu(xf @ w_sh_gate) * (xf @ w_sh_up)) @ w_sh_down
    return (routed + shared).reshape(B, S, D)


def setup_inputs(seed: int = 0) -> dict:
    key = jax.random.key(seed)
    ks = jax.random.split(key, 24)
    L, D = DEPTH, D_MODEL

    def nrm(k, shape, scale):
        return jax.random.normal(k, shape, jnp.float32) * scale

    s_in = D ** -0.5
    w_in = jnp.concatenate([
        nrm(ks[1], (L, D, 2 * HG_QK), s_in),
        nrm(ks[2], (L, D, HG_V), s_in * DEEPNORM_BETA),
        nrm(ks[3], (L, D, HG_V), s_in),
        nrm(ks[4], (L, D, 2 * SB_W), s_in),
        nrm(ks[5], (L, D, SB_W), s_in * DEEPNORM_BETA),
        nrm(ks[6], (L, D, 2 * D), s_in),
    ], axis=-1)
    return {
        'x': nrm(ks[0], (BATCH, SEQ, D), 1.0),
        'w_in': w_in,
        'lower_bounds': nrm(ks[7], (L + 1, HG_QK), 0.1),
        'hg_norm_g': 1.0 + nrm(ks[8], (L, HG_V), 0.02),
        'w_branch_hg': nrm(ks[9], (L, HG_V, D), HG_V ** -0.5 * DEEPNORM_BETA),
        'w_branch_sb': nrm(ks[10], (L, SB_W, D), SB_W ** -0.5 * DEEPNORM_BETA),
        'w_out': nrm(ks[11], (L, D, D), s_in * DEEPNORM_BETA),
        'ln1_g': 1.0 + nrm(ks[12], (L, D), 0.02),
        'ln1_b': nrm(ks[13], (L, D), 0.02),
        'w_router': nrm(ks[14], (L, D, N_EXPERTS), s_in),
        'router_bias': nrm(ks[15], (L, N_EXPERTS), 0.01),
        'w_exp_gate': nrm(ks[16], (L, N_EXPERTS, D, D_EXPERT), s_in),
        'w_exp_up': nrm(ks[17], (L, N_EXPERTS, D, D_EXPERT), s_in),
        'w_exp_down': nrm(ks[18], (L, N_EXPERTS, D_EXPERT, D), D_EXPERT ** -0.5 * DEEPNORM_BETA),
        'w_sh_gate': nrm(ks[19], (L, D, D_SHARED), s_in),
        'w_sh_up': nrm(ks[20], (L, D, D_SHARED), s_in),
        'w_sh_down': nrm(ks[21], (L, D_SHARED, D), D_SHARED ** -0.5 * DEEPNORM_BETA),
        'ln2_g': 1.0 + nrm(ks[22], (L, D), 0.02),
        'ln2_b': nrm(ks[23], (L, D), 0.02),
    }


def reference(x, w_in, lower_bounds, hg_norm_g, w_branch_hg, w_branch_sb, w_out, ln1_g, ln1_b,
              w_router, router_bias, w_exp_gate, w_exp_up, w_exp_down,
              w_sh_gate, w_sh_up, w_sh_down, ln2_g, ln2_b):
    lb_all = jnp.cumsum(jax.nn.softmax(lower_bounds.astype(jnp.float32), axis=0), axis=0)
    for l in range(DEPTH):
        h = hybrid_mixer(x, w_in[l], lb_all[l], hg_norm_g[l], w_branch_hg[l], w_branch_sb[l], w_out[l])
        x = layer_norm(DEEPNORM_ALPHA * x + h, ln1_g[l], ln1_b[l])
        m = moe(x, w_router[l], router_bias[l], w_exp_gate[l], w_exp_up[l], w_exp_down[l],
                w_sh_gate[l], w_sh_up[l], w_sh_down[l])
        x = layer_norm(DEEPNORM_ALPHA * x + m, ln2_g[l], ln2_b[l])
    return x
```

```python
import functools

import jax
import jax.numpy as jnp
from jax import lax
from jax.experimental import pallas as pl
from jax.experimental.pallas import tpu as pltpu

F32 = jnp.float32
BF16 = jnp.bfloat16

LANES = 128
SUBLANES = 8
VMEM_LIMIT = 48 * 1024 * 1024

HG_HEADS = 8
HG_CHUNK = 32
SB_HEADS = 8
N_GROUPS = 8
TOPK_GROUPS = 4
TOP_K = 8
ROUTED_SCALE = 2.5
MOE_BLOCK = 128
LN_EPS = 1e-5
RMS_EPS = 1e-6

NT_DIMS = (((1,), (1,)), ((), ()))
TN_DIMS = (((0,), (0,)), ((), ()))


def _dot(a, b):
    return jnp.dot(a, b, preferred_element_type=F32)


def _sigmoid(x):
    return 1.0 / (1.0 + jnp.exp(-x))


def _split3(x):
    hi = x.astype(BF16)
    r1 = x - hi.astype(F32)
    mid = r1.astype(BF16)
    lo = (r1 - mid.astype(F32)).astype(BF16)
    return hi, mid, lo


def _layer_norm(r, g, b):
    mu = jnp.mean(r, axis=-1, keepdims=True)
    d = r - mu
    var = jnp.mean(d * d, axis=-1, keepdims=True)
    return d * lax.rsqrt(var + LN_EPS) * g + b


def _mm_kernel(x_ref, w_ref, o_ref):
    o_ref[...] = _dot(x_ref[...].astype(BF16), w_ref[...]).astype(o_ref.dtype)


def _matmul(x, w, out_dtype, tm, tn):
    m, k = x.shape
    n = w.shape[1]
    return pl.pallas_call(
        _mm_kernel,
        grid=(m // tm, n // tn),
        in_specs=[pl.BlockSpec((tm, k), lambda i, j: (i, 0)),
                  pl.BlockSpec((k, tn), lambda i, j: (0, j))],
        out_specs=pl.BlockSpec((tm, tn), lambda i, j: (i, j)),
        out_shape=jax.ShapeDtypeStruct((m, n), out_dtype),
        compiler_params=pltpu.CompilerParams(
            dimension_semantics=("parallel", "arbitrary"), vmem_limit_bytes=VMEM_LIMIT),
        name="in_proj",
    )(x, w)


def _hgrn_kernel(lb_ref, g_ref, q_ref, f_ref, v_ref, og_ref, o_ref, st_ref, *, ts, chunk):
    @pl.when(pl.program_id(2) == 0)
    def _():
        st_ref[...] = jnp.zeros_like(st_ref)

    lb = lb_ref[0]
    hf = f_ref[...]
    log_f = jnp.log(lb + (1.0 - lb) * _sigmoid(hf))
    k_in = (1.0 - lb) * _sigmoid(-hf)

    shift = chunk.bit_length() - 1
    row = lax.broadcasted_iota(jnp.int32, (ts, ts), 0)
    col = lax.broadcasted_iota(jnp.int32, (ts, ts), 1)
    same = (row >> shift) == (col >> shift)
    causal = jnp.logical_and(same, col <= row)
    tri = jnp.where(causal, 1.0, 0.0).astype(BF16)
    ones = jnp.where(same, 1.0, 0.0).astype(BF16)

    parts = jnp.concatenate(_split3(log_f), axis=1)
    cs = _dot(tri, parts)
    tot = _dot(ones, parts)
    d = LANES
    b = cs[:, :d] + cs[:, d:2 * d] + cs[:, 2 * d:]
    b_end = tot[:, :d] + tot[:, d:2 * d] + tot[:, 2 * d:]

    q = q_ref[...]
    q_start = (q * jnp.exp(b)).astype(BF16)
    q_end = (q * jnp.exp(b - b_end)).astype(BF16)
    k_end = (k_in * jnp.exp(b_end - b)).astype(BF16)
    v = v_ref[...]

    scores = lax.dot_general(q_end, k_end, NT_DIMS, preferred_element_type=F32)
    scores = jnp.where(causal, scores, 0.0)
    o_intra = _dot(scores.astype(BF16), v)

    dec = jnp.exp(b_end)
    st = st_ref[...]
    outs = []
    for c in range(ts // chunk):
        lo = c * chunk
        outs.append(lax.dot_general(q_start[lo:lo + chunk], st.astype(BF16), NT_DIMS,
                                    preferred_element_type=F32))
        kv = lax.dot_general(v[lo:lo + chunk], k_end[lo:lo + chunk], TN_DIMS,
                             preferred_element_type=F32)
        st = dec[lo:lo + 1, :] * st + kv
    st_ref[...] = st
    o = o_intra + jnp.concatenate(outs, axis=0)

    o = o * lax.rsqrt(jnp.mean(o * o, axis=-1, keepdims=True) + RMS_EPS)
    o = o * g_ref[0]
    hg = og_ref[...]
    o_ref[...] = (o * (hg * _sigmoid(hg))).astype(o_ref.dtype)


def _hgrn(p32, pbf, lb, g, batch, seq, ts):
    t = batch * seq
    h = HG_HEADS
    ns = seq // ts
    row = lambda off: (lambda b, hh, s: (b * ns + s, off + hh))
    par = lambda b, hh, s: (hh, 0, 0)
    return pl.pallas_call(
        functools.partial(_hgrn_kernel, ts=ts, chunk=HG_CHUNK),
        grid=(batch, h, ns),
        in_specs=[pl.BlockSpec((1, 1, LANES), par),
                  pl.BlockSpec((1, 1, LANES), par),
                  pl.BlockSpec((ts, LANES), row(0)),
                  pl.BlockSpec((ts, LANES), row(h)),
                  pl.BlockSpec((ts, LANES), row(0)),
                  pl.BlockSpec((ts, LANES), row(2 * h))],
        out_specs=pl.BlockSpec((ts, LANES), row(0)),
        out_shape=jax.ShapeDtypeStruct((t, h * LANES), BF16),
        scratch_shapes=[pltpu.VMEM((LANES, LANES), F32)],
        compiler_params=pltpu.CompilerParams(
            dimension_semantics=("parallel", "parallel", "arbitrary"), vmem_limit_bytes=VMEM_LIMIT),
        name="hgrn2",
    )(lb.reshape(h, 1, LANES), g.reshape(h, 1, LANES), p32, p32, pbf, p32)


def _sb_kernel(q_ref, k_ref, v_ref, u_ref, o_ref, *, tq, scale):
    i = pl.program_id(2)
    q = q_ref[...]
    u = u_ref[...]
    rep = tq // LANES

    def block(j, carry, acc, masked):
        start = pl.multiple_of(j * tq, tq)
        kj = k_ref[pl.ds(start, tq), :]
        vj = v_ref[pl.ds(start, tq), :]
        z = lax.dot_general(q, kj, NT_DIMS, preferred_element_type=F32) * scale
        ls_neg = -(jnp.maximum(z, 0.0) + jnp.log1p(jnp.exp(-jnp.abs(z))))
        if masked:
            row = lax.broadcasted_iota(jnp.int32, (tq, tq), 0)
            col = lax.broadcasted_iota(jnp.int32, (tq, tq), 1)
            before = col < row
            log_keep = jnp.where(before, ls_neg, 0.0)
        else:
            log_keep = ls_neg
        hi = log_keep.astype(BF16)
        lo = (log_keep - hi.astype(F32)).astype(BF16)
        r = _dot(hi, u) + _dot(lo, u)
        stick = r[:, :tq] + jnp.concatenate([carry] * rep, axis=1)
        w = jnp.exp(ls_neg + z + stick)
        if masked:
            w = jnp.where(before, w, 0.0)
        acc = acc + _dot(w.astype(BF16), vj)
        carry = carry + r[:, tq:tq + LANES]
        return carry, acc

    zero = jnp.zeros((tq, LANES), F32)
    carry, acc = block(i, zero, zero, True)

    def body(jj, state):
        return block(i - 1 - jj, state[0], state[1], False)

    carry, acc = lax.fori_loop(0, i, body, (carry, acc))
    o_ref[...] = acc.astype(o_ref.dtype)


def _sb_attention(pbf, batch, seq, tq):
    t = batch * seq
    h = SB_HEADS
    nq = seq // tq
    r = jnp.arange(tq)
    later = (r[:, None] > r[None, :]).astype(BF16)
    u = jnp.concatenate([later, jnp.ones((tq, tq), BF16)], axis=1)
    return pl.pallas_call(
        functools.partial(_sb_kernel, tq=tq, scale=float(LANES) ** -0.5),
        grid=(batch, h, nq),
        in_specs=[pl.BlockSpec((tq, LANES), lambda b, hh, i: (b * nq + i, h + hh)),
                  pl.BlockSpec((seq, LANES), lambda b, hh, i: (b, 2 * h + hh)),
                  pl.BlockSpec((seq, LANES), lambda b, hh, i: (b, 3 * h + hh)),
                  pl.BlockSpec((tq, 2 * tq), lambda b, hh, i: (0, 0))],
        out_specs=pl.BlockSpec((tq, LANES), lambda b, hh, i: (b * nq + i, hh)),
        out_shape=jax.ShapeDtypeStruct((t, h * LANES), BF16),
        compiler_params=pltpu.CompilerParams(
            dimension_semantics=("parallel", "parallel", "arbitrary"), vmem_limit_bytes=VMEM_LIMIT),
        name="stick_breaking",
    )(pbf, pbf, pbf, u)


def _merge_kernel(ohg_ref, osb_ref, ghg_ref, gsb_ref, x_ref, wbh_ref, wbs_ref, wo_ref,
                  lg_ref, lb_ref, wrh_ref, wrl_ref, x1_ref, sc_ref, *, alpha):
    y_hg = _dot(ohg_ref[...], wbh_ref[...])
    y_sb = _dot(osb_ref[...], wbs_ref[...])
    merged = _sigmoid(ghg_ref[...]) * y_hg + _sigmoid(gsb_ref[...]) * y_sb
    hmix = _dot(merged.astype(BF16), wo_ref[...])
    x1 = _layer_norm(alpha * x_ref[...] + hmix, lg_ref[...], lb_ref[...])
    x1_ref[...] = x1
    xh = x1.astype(BF16)
    xl = (x1 - xh.astype(F32)).astype(BF16)
    wrh = wrh_ref[...]
    logits = _dot(xh, wrh) + _dot(xl, wrh) + _dot(xh, wrl_ref[...])
    sc_ref[...] = _sigmoid(logits)


def _merge(o_hg, o_sb, p32, x, wbh, wbs, wo, lg, lb, wrh, wrl, alpha, tm):
    t, d = x.shape
    e = wrh.shape[1]
    row = lambda off: (lambda i: (i, off))
    full = lambda i: (0, 0)
    return pl.pallas_call(
        functools.partial(_merge_kernel, alpha=alpha),
        grid=(t // tm,),
        in_specs=[pl.BlockSpec((tm, d), row(0)), pl.BlockSpec((tm, d), row(0)),
                  pl.BlockSpec((tm, d), row(3)), pl.BlockSpec((tm, d), row(4)),
                  pl.BlockSpec((tm, d), row(0)),
                  pl.BlockSpec((d, d), full), pl.BlockSpec((d, d), full), pl.BlockSpec((d, d), full),
                  pl.BlockSpec((1, d), full), pl.BlockSpec((1, d), full),
                  pl.BlockSpec((d, e), full), pl.BlockSpec((d, e), full)],
        out_specs=[pl.BlockSpec((tm, d), row(0)), pl.BlockSpec((tm, e), row(0))],
        out_shape=[jax.ShapeDtypeStruct((t, d), F32), jax.ShapeDtypeStruct((t, e), F32)],
        compiler_params=pltpu.CompilerParams(
            dimension_semantics=("parallel",), vmem_limit_bytes=VMEM_LIMIT),
        name="merge_ln1_router",
    )(o_hg, o_sb, p32, p32, x, wbh, wbs, wo, lg, lb, wrh, wrl)


def _moe_kernel(bexp_ref, nused_ref, tok_hbm, dst_hbm, x_hbm, wrep_ref, wg_ref, wu_ref, wd_ref,
                y_hbm, xbuf, ybuf, tok_s, dst_s, wg_b, wu_b, wd_b, gsem, ssem, isem,
                *, mb, nchunk, trash_row0):
    i = pl.program_id(0)
    nused = nused_ref[0]
    rows = mb * nchunk

    def idx_copies(blk):
        slot = blk % 3
        return (pltpu.make_async_copy(tok_hbm.at[blk], tok_s.at[slot], isem.at[slot]),
                pltpu.make_async_copy(dst_hbm.at[blk], dst_s.at[slot], isem.at[slot]))

    def start_idx(blk):
        for cp in idx_copies(blk):
            cp.start()

    def wait_idx(blk):
        for cp in idx_copies(blk):
            cp.wait()

    def start_gather(blk):
        islot = blk % 3
        slot = blk % 2

        def body(r, _):
            src = pl.multiple_of(tok_s[islot, r], nchunk)
            pltpu.make_async_copy(x_hbm.at[pl.ds(src, nchunk), :],
                                  xbuf.at[slot, pl.ds(r * nchunk, nchunk), :], gsem.at[slot]).start()
            return 0

        lax.fori_loop(0, mb, body, 0, unroll=8)

    def wait_gather(blk):
        slot = blk % 2
        pltpu.make_async_copy(xbuf.at[slot], xbuf.at[slot], gsem.at[slot]).wait()

    def start_scatter(blk):
        islot = blk % 3
        slot = blk % 2

        def body(r, _):
            dst = pl.multiple_of(dst_s[islot, r], nchunk)
            pltpu.make_async_copy(ybuf.at[slot, pl.ds(r * nchunk, nchunk), :],
                                  y_hbm.at[pl.ds(dst, nchunk), :], ssem.at[slot]).start()
            return 0

        lax.fori_loop(0, mb, body, 0, unroll=8)

    def wait_scatter(blk):
        slot = blk % 2
        pltpu.make_async_copy(ybuf.at[slot], ybuf.at[slot], ssem.at[slot]).wait()

    @pl.when(i == 0)
    def _():
        start_idx(0)
        ybuf[0] = jnp.zeros((rows, LANES), F32)
        for half in range(2):
            cp = pltpu.make_async_copy(ybuf.at[0], y_hbm.at[pl.ds(trash_row0 + half * rows, rows), :],
                                       ssem.at[0])
            cp.start()
            cp.wait()
        wait_idx(0)
        start_gather(0)

        @pl.when(nused > 1)
        def _():
            start_idx(1)

    @pl.when(i < nused)
    def _():
        @pl.when(i + 1 < nused)
        def _():
            wait_idx(i + 1)
            start_gather(i + 1)

        @pl.when(i + 2 < nused)
        def _():
            start_idx(i + 2)

        wait_gather(i)

        new_expert = jnp.logical_or(i == 0, bexp_ref[i] != bexp_ref[jnp.maximum(i - 1, 0)])

        @pl.when(new_expert)
        def _():
            wg_b[...] = wg_ref[0].astype(BF16)
            wu_b[...] = wu_ref[0].astype(BF16)
            wd_b[...] = wd_ref[0].astype(BF16)

        slot = i % 2
        xs = [xbuf[slot, pl.ds(c, mb, stride=nchunk), :] for c in range(nchunk)]
        xb = jnp.concatenate(xs, axis=1).astype(BF16)
        gate = _dot(xb, wg_b[...])
        up = _dot(xb, wu_b[...])
        hid = (gate * _sigmoid(gate)) * up
        y = _dot(hid.astype(BF16), wd_b[...])
        wrep = wrep_ref[...]

        @pl.when(i >= 2)
        def _():
            wait_scatter(i - 2)

        for c in range(nchunk):
            ybuf[slot, pl.ds(c, mb, stride=nchunk), :] = y[:, c * LANES:(c + 1) * LANES] * wrep
        start_scatter(i)

        @pl.when(i == nused - 1)
        def _():
            @pl.when(i >= 1)
            def _():
                wait_scatter(i - 1)

            wait_scatter(i)


def _moe(x2, tok_blocks, dst_blocks, wrep, block_expert, nused, w_gate, w_up, w_down, n_slots):
    n_blocks, mb = tok_blocks.shape
    e, d, de = w_gate.shape
    nchunk = d // LANES
    rows = mb * nchunk
    grid_spec = pltpu.PrefetchScalarGridSpec(
        num_scalar_prefetch=2,
        grid=(n_blocks,),
        in_specs=[pl.BlockSpec(memory_space=pl.ANY),
                  pl.BlockSpec(memory_space=pl.ANY),
                  pl.BlockSpec(memory_space=pl.ANY),
                  pl.BlockSpec((mb, LANES), lambda i, be, nu: (i, 0)),
                  pl.BlockSpec((1, d, de), lambda i, be, nu: (be[i], 0, 0)),
                  pl.BlockSpec((1, d, de), lambda i, be, nu: (be[i], 0, 0)),
                  pl.BlockSpec((1, de, d), lambda i, be, nu: (be[i], 0, 0))],
        out_specs=pl.BlockSpec(memory_space=pl.ANY),
        scratch_shapes=[pltpu.VMEM((2, rows, LANES), F32),
                        pltpu.VMEM((2, rows, LANES), F32),
                        pltpu.SMEM((3, mb), jnp.int32),
                        pltpu.SMEM((3, mb), jnp.int32),
                        pltpu.VMEM((d, de), BF16),
                        pltpu.VMEM((d, de), BF16),
                        pltpu.VMEM((de, d), BF16),
                        pltpu.SemaphoreType.DMA((2,)),
                        pltpu.SemaphoreType.DMA((2,)),
                        pltpu.SemaphoreType.DMA((3,))],
    )
    return pl.pallas_call(
        functools.partial(_moe_kernel, mb=mb, nchunk=nchunk,
                          trash_row0=(n_slots - 2 * mb) * nchunk),
        grid_spec=grid_spec,
        out_shape=jax.ShapeDtypeStruct((n_slots * nchunk, LANES), F32),
        compiler_params=pltpu.CompilerParams(
            dimension_semantics=("arbitrary",), vmem_limit_bytes=VMEM_LIMIT),
        name="routed_experts",
    )(block_expert, nused, tok_blocks, dst_blocks, x2, wrep, w_gate, w_up, w_down)


def _final_kernel(y_ref, x1_ref, wsg_ref, wsu_ref, wsd_ref, lg_ref, lb_ref, o_ref, r_s,
                  *, tm, topk, nchunk, alpha):
    y = y_ref[...].reshape(tm, topk, nchunk, LANES)
    r_s[...] = jnp.sum(y, axis=1).reshape(tm * nchunk, LANES)
    routed = jnp.concatenate([r_s[pl.ds(c, tm, stride=nchunk), :] for c in range(nchunk)], axis=1)
    x1 = x1_ref[...]
    xb = x1.astype(BF16)
    gate = _dot(xb, wsg_ref[...])
    up = _dot(xb, wsu_ref[...])
    shared = _dot(((gate * _sigmoid(gate)) * up).astype(BF16), wsd_ref[...])
    o_ref[...] = _layer_norm(alpha * x1 + (routed + shared), lg_ref[...], lb_ref[...])


def _final(y3, x1, wsg, wsu, wsd, lg, lb, alpha, tm):
    t, d = x1.shape
    nchunk = d // LANES
    ds = wsg.shape[1]
    full = lambda i: (0, 0)
    return pl.pallas_call(
        functools.partial(_final_kernel, tm=tm, topk=TOP_K, nchunk=nchunk, alpha=alpha),
        grid=(t // tm,),
        in_specs=[pl.BlockSpec((tm * TOP_K * nchunk, LANES), lambda i: (i, 0)),
                  pl.BlockSpec((tm, d), lambda i: (i, 0)),
                  pl.BlockSpec((d, ds), full), pl.BlockSpec((d, ds), full), pl.BlockSpec((ds, d), full),
                  pl.BlockSpec((1, d), full), pl.BlockSpec((1, d), full)],
        out_specs=pl.BlockSpec((tm, d), lambda i: (i, 0)),
        out_shape=jax.ShapeDtypeStruct((t, d), F32),
        scratch_shapes=[pltpu.VMEM((tm * nchunk, LANES), F32)],
        compiler_params=pltpu.CompilerParams(
            dimension_semantics=("parallel",), vmem_limit_bytes=VMEM_LIMIT),
        name="combine_shared_ln2",
    )(y3, x1, wsg, wsu, wsd, lg, lb)


def _route(scores, router_bias):
    t, e = scores.shape
    biased = scores + router_bias.astype(F32)
    grp = biased.reshape(t, N_GROUPS, e // N_GROUPS)
    grp_score = lax.top_k(grp, 2)[0].sum(-1)
    _, top_grp = lax.top_k(grp_score, TOPK_GROUPS)
    grp_mask = jax.nn.one_hot(top_grp, N_GROUPS, dtype=F32).sum(1) > 0
    expert_mask = jnp.repeat(grp_mask, e // N_GROUPS, axis=1)
    masked = jnp.where(expert_mask, biased, jnp.finfo(F32).min)
    _, idx = lax.top_k(masked, TOP_K)
    w = jnp.take_along_axis(scores, idx, axis=1)
    w = w / jnp.sum(w, axis=-1, keepdims=True) * ROUTED_SCALE
    return idx, w


def _dispatch_tables(idx, w, n_experts, nchunk):
    t = idx.shape[0]
    a = t * TOP_K
    mb = MOE_BLOCK
    e_flat = idx.reshape(a).astype(jnp.int32)
    w_flat = w.reshape(a)
    order = jnp.argsort(e_flat).astype(jnp.int32)
    e_s = e_flat[order]
    counts = jax.ops.segment_sum(jnp.ones((a,), jnp.int32), e_flat, num_segments=n_experts)
    start = jnp.cumsum(counts) - counts
    padded = (counts + mb - 1) // mb * mb
    pad_end = jnp.cumsum(padded)
    pad_start = pad_end - padded
    dest = pad_start[e_s] + (jnp.arange(a, dtype=jnp.int32) - start[e_s])
    p = a + n_experts * mb
    n_blocks = p // mb
    pos = jnp.arange(p, dtype=jnp.int32)
    trash = a + ((pos // mb) % 2) * mb + pos % mb
    row_tok = jnp.zeros((p,), jnp.int32).at[dest].set(order // TOP_K) * nchunk
    row_dst = trash.at[dest].set(order) * nchunk
    row_w = jnp.zeros((p,), F32).at[dest].set(w_flat[order])
    block_expert = jnp.minimum(
        jnp.searchsorted(pad_end, jnp.arange(n_blocks, dtype=jnp.int32) * mb, side='right'),
        n_experts - 1).astype(jnp.int32)
    nused = (pad_end[-1:] // mb).astype(jnp.int32)
    wrep = jnp.broadcast_to(row_w[:, None], (p, LANES))
    return (row_tok.reshape(n_blocks, mb), row_dst.reshape(n_blocks, mb), wrep, block_expert, nused,
            a + 2 * mb)


def _tile(n, pref):
    while n % pref:
        pref //= 2
    return pref


def kernel(x, w_in, lower_bounds, hg_norm_g, w_branch_hg, w_branch_sb, w_out, ln1_g, ln1_b,
           w_router, router_bias, w_exp_gate, w_exp_up, w_exp_down,
           w_sh_gate, w_sh_up, w_sh_down, ln2_g, ln2_b):
    depth = w_in.shape[0]
    assert depth == 1, "single-layer block only"
    batch, seq, d = x.shape
    t = batch * seq
    n_experts = w_router.shape[-1]
    alpha = (2.0 * depth) ** 0.25
    nchunk = d // LANES

    lb = jnp.cumsum(jax.nn.softmax(lower_bounds.astype(F32), axis=0), axis=0)[0]
    xf = x.reshape(t, d)

    w = w_in[0]
    c = d
    w32 = jnp.concatenate([w[:, 0:2 * c], w[:, 3 * c:4 * c], w[:, 7 * c:9 * c]], axis=1).astype(BF16)
    wbf = jnp.concatenate([w[:, 2 * c:3 * c], w[:, 4 * c:7 * c]], axis=1).astype(BF16)
    tm = _tile(t, 1024)
    p32 = _matmul(xf, w32, F32, tm, 512)
    pbf = _matmul(xf, wbf, BF16, tm, 512)

    o_hg = _hgrn(p32, pbf, lb, hg_norm_g[0].astype(F32), batch, seq, _tile(seq, 256))
    o_sb = _sb_attention(pbf, batch, seq, _tile(seq, 256))

    wr = w_router[0].astype(F32)
    wrh = wr.astype(BF16)
    wrl = (wr - wrh.astype(F32)).astype(BF16)
    x1, scores = _merge(o_hg, o_sb, p32, xf,
                        w_branch_hg[0].astype(BF16), w_branch_sb[0].astype(BF16), w_out[0].astype(BF16),
                        ln1_g[0].reshape(1, d).astype(F32), ln1_b[0].reshape(1, d).astype(F32),
                        wrh, wrl, alpha, _tile(t, 512))

    idx, wts = _route(scores, router_bias[0])
    tok_b, dst_b, wrep, block_expert, nused, n_slots = _dispatch_tables(idx, wts, n_experts, nchunk)
    y3 = _moe(x1.reshape(t * nchunk, LANES), tok_b, dst_b, wrep, block_expert, nused,
              w_exp_gate[0], w_exp_up[0], w_exp_down[0], n_slots)

    out = _final(y3, x1, w_sh_gate[0].astype(BF16), w_sh_up[0].astype(BF16), w_sh_down[0].astype(BF16),
                 ln2_g[0].reshape(1, d).astype(F32), ln2_b[0].reshape(1, d).astype(F32),
                 alpha, _tile(t, 256))
    return out.reshape(batch, seq, d)
```

```python
import functools

import jax
import jax.numpy as jnp
from jax import lax
from jax.experimental import pallas as pl
from jax.experimental.pallas import tpu as pltpu

F32 = jnp.float32
BF16 = jnp.bfloat16

LANES = 128
SUBLANES = 8
VMEM_LIMIT = 48 * 1024 * 1024

HG_HEADS = 8
HG_CHUNK = 32
SB_HEADS = 8
N_GROUPS = 8
TOPK_GROUPS = 4
TOP_K = 8
ROUTED_SCALE = 2.5
MOE_BLOCK = 128
LN_EPS = 1e-5
RMS_EPS = 1e-6

NT_DIMS = (((1,), (1,)), ((), ()))
TN_DIMS = (((0,), (0,)), ((), ()))


def _dot(a, b):
    return jnp.dot(a, b, preferred_element_type=F32)


def _sigmoid(x):
    return 1.0 / (1.0 + jnp.exp(-x))


def _split3(x):
    hi = x.astype(BF16)
    r1 = x - hi.astype(F32)
    mid = r1.astype(BF16)
    lo = (r1 - mid.astype(F32)).astype(BF16)
    return hi, mid, lo


def _layer_norm(r, g, b):
    mu = jnp.mean(r, axis=-1, keepdims=True)
    d = r - mu
    var = jnp.mean(d * d, axis=-1, keepdims=True)
    return d * lax.rsqrt(var + LN_EPS) * g + b


def _mm_kernel(x_ref, w_ref, o_ref):
    o_ref[...] = _dot(x_ref[...].astype(BF16), w_ref[...]).astype(o_ref.dtype)


def _matmul(x, w, out_dtype, tm, tn):
    m, k = x.shape
    n = w.shape[1]
    return pl.pallas_call(
        _mm_kernel,
        grid=(m // tm, n // tn),
        in_specs=[pl.BlockSpec((tm, k), lambda i, j: (i, 0)),
                  pl.BlockSpec((k, tn), lambda i, j: (0, j))],
        out_specs=pl.BlockSpec((tm, tn), lambda i, j: (i, j)),
        out_shape=jax.ShapeDtypeStruct((m, n), out_dtype),
        compiler_params=pltpu.CompilerParams(
            dimension_semantics=("parallel", "arbitrary"), vmem_limit_bytes=VMEM_LIMIT),
        name="in_proj",
    )(x, w)


def _hgrn_kernel(lb_ref, g_ref, q_ref, f_ref, v_ref, og_ref, o_ref, st_ref, *, ts, chunk):
    @pl.when(pl.program_id(2) == 0)
    def _():
        st_ref[...] = jnp.zeros_like(st_ref)

    lb = lb_ref[0]
    hf = f_ref[...]
    log_f = jnp.log(lb + (1.0 - lb) * _sigmoid(hf))
    k_in = (1.0 - lb) * _sigmoid(-hf)

    shift = chunk.bit_length() - 1
    row = lax.broadcasted_iota(jnp.int32, (ts, ts), 0)
    col = lax.broadcasted_iota(jnp.int32, (ts, ts), 1)
    same = (row >> shift) == (col >> shift)
    causal = jnp.logical_and(same, col <= row)
    tri = jnp.where(causal, 1.0, 0.0).astype(BF16)
    ones = jnp.where(same, 1.0, 0.0).astype(BF16)

    parts = jnp.concatenate(_split3(log_f), axis=1)
    cs = _dot(tri, parts)
    tot = _dot(ones, parts)
    d = LANES
    b = cs[:, :d] + cs[:, d:2 * d] + cs[:, 2 * d:]
    b_end = tot[:, :d] + tot[:, d:2 * d] + tot[:, 2 * d:]

    q = q_ref[...]
    q_start = (q * jnp.exp(b)).astype(BF16)
    q_end = (q * jnp.exp(b - b_end)).astype(BF16)
    k_end = (k_in * jnp.exp(b_end - b)).astype(BF16)
    v = v_ref[...]

    scores = lax.dot_general(q_end, k_end, NT_DIMS, preferred_element_type=F32)
    scores = jnp.where(causal, scores, 0.0)
    o_intra = _dot(scores.astype(BF16), v)

    dec = jnp.exp(b_end)
    st = st_ref[...]
    outs = []
    for c in range(ts // chunk):
        lo = c * chunk
        outs.append(lax.dot_general(q_start[lo:lo + chunk], st.astype(BF16), NT_DIMS,
                                    preferred_element_type=F32))
        kv = lax.dot_general(v[lo:lo + chunk], k_end[lo:lo + chunk], TN_DIMS,
                             preferred_element_type=F32)
        st = dec[lo:lo + 1, :] * st + kv
    st_ref[...] = st
    o = o_intra + jnp.concatenate(outs, axis=0)

    o = o * lax.rsqrt(jnp.mean(o * o, axis=-1, keepdims=True) + RMS_EPS)
    o = o * g_ref[0]
    hg = og_ref[...]
    o_ref[...] = (o * (hg * _sigmoid(hg))).astype(o_ref.dtype)


def _hgrn(p32, pbf, lb, g, batch, seq, ts):
    t = batch * seq
    h = HG_HEADS
    ns = seq // ts
    row = lambda off: (lambda b, hh, s: (b * ns + s, off + hh))
    par = lambda b, hh, s: (hh, 0, 0)
    return pl.pallas_call(
        functools.partial(_hgrn_kernel, ts=ts, chunk=HG_CHUNK),
        grid=(batch, h, ns),
        in_specs=[pl.BlockSpec((1, 1, LANES), par),
                  pl.BlockSpec((1, 1, LANES), par),
                  pl.BlockSpec((ts, LANES), row(0)),
                  pl.BlockSpec((ts, LANES), row(h)),
                  pl.BlockSpec((ts, LANES), row(0)),
                  pl.BlockSpec((ts, LANES), row(2 * h))],
        out_specs=pl.BlockSpec((ts, LANES), row(0)),
        out_shape=jax.ShapeDtypeStruct((t, h * LANES), BF16),
        scratch_shapes=[pltpu.VMEM((LANES, LANES), F32)],
        compiler_params=pltpu.CompilerParams(
            dimension_semantics=("parallel", "parallel", "arbitrary"), vmem_limit_bytes=VMEM_LIMIT),
        name="hgrn2",
    )(lb.reshape(h, 1, LANES), g.reshape(h, 1, LANES), p32, p32, pbf, p32)


def _sb_kernel(q_ref, k_ref, v_ref, u_ref, o_ref, *, tq, scale):
    i = pl.program_id(2)
    q = q_ref[...]
    u = u_ref[...]
    rep = tq // LANES

    def block(j, carry, acc, masked):
        start = pl.multiple_of(j * tq, tq)
        kj = k_ref[pl.ds(start, tq), :]
        vj = v_ref[pl.ds(start, tq), :]
        z = lax.dot_general(q, kj, NT_DIMS, preferred_element_type=F32) * scale
        ls_neg = -(jnp.maximum(z, 0.0) + jnp.log1p(jnp.exp(-jnp.abs(z))))
        if masked:
            row = lax.broadcasted_iota(jnp.int32, (tq, tq), 0)
            col = lax.broadcasted_iota(jnp.int32, (tq, tq), 1)
            before = col < row
            log_keep = jnp.where(before, ls_neg, 0.0)
        else:
            log_keep = ls_neg
        hi = log_keep.astype(BF16)
        lo = (log_keep - hi.astype(F32)).astype(BF16)
        r = _dot(hi, u) + _dot(lo, u)
        stick = r[:, :tq] + jnp.concatenate([carry] * rep, axis=1)
        w = jnp.exp(ls_neg + z + stick)
        if masked:
            w = jnp.where(before, w, 0.0)
        acc = acc + _dot(w.astype(BF16), vj)
        carry = carry + r[:, tq:tq + LANES]
        return carry, acc

    zero = jnp.zeros((tq, LANES), F32)
    carry, acc = block(i, zero, zero, True)

    def body(jj, state):
        return block(i - 1 - jj, state[0], state[1], False)

    carry, acc = lax.fori_loop(0, i, body, (carry, acc))
    o_ref[...] = acc.astype(o_ref.dtype)


def _sb_attention(pbf, batch, seq, tq):
    t = batch * seq
    h = SB_HEADS
    nq = seq // tq
    r = jnp.arange(tq)
    later = (r[:, None] > r[None, :]).astype(BF16)
    u = jnp.concatenate([later, jnp.ones((tq, tq), BF16)], axis=1)
    return pl.pallas_call(
        functools.partial(_sb_kernel, tq=tq, scale=float(LANES) ** -0.5),
        grid=(batch, h, nq),
        in_specs=[pl.BlockSpec((tq, LANES), lambda b, hh, i: (b * nq + i, h + hh)),
                  pl.BlockSpec((seq, LANES), lambda b, hh, i: (b, 2 * h + hh)),
                  pl.BlockSpec((seq, LANES), lambda b, hh, i: (b, 3 * h + hh)),
                  pl.BlockSpec((tq, 2 * tq), lambda b, hh, i: (0, 0))],
        out_specs=pl.BlockSpec((tq, LANES), lambda b, hh, i: (b * nq + i, hh)),
        out_shape=jax.ShapeDtypeStruct((t, h * LANES), BF16),
        compiler_params=pltpu.CompilerParams(
            dimension_semantics=("parallel", "parallel", "arbitrary"), vmem_limit_bytes=VMEM_LIMIT),
        name="stick_breaking",
    )(pbf, pbf, pbf, u)


def _merge_kernel(ohg_ref, osb_ref, ghg_ref, gsb_ref, x_ref, wbh_ref, wbs_ref, wo_ref,
                  lg_ref, lb_ref, x1_ref, *, alpha):
    y_hg = _dot(ohg_ref[...], wbh_ref[...])
    y_sb = _dot(osb_ref[...], wbs_ref[...])
    merged = _sigmoid(ghg_ref[...]) * y_hg + _sigmoid(gsb_ref[...]) * y_sb
    hmix = _dot(merged.astype(BF16), wo_ref[...])
    x1_ref[...] = _layer_norm(alpha * x_ref[...] + hmix, lg_ref[...], lb_ref[...])


def _merge(o_hg, o_sb, p32, x, wbh, wbs, wo, lg, lb, alpha, tm):
    t, d = x.shape
    row = lambda off: (lambda i: (i, off))
    full = lambda i: (0, 0)
    return pl.pallas_call(
        functools.partial(_merge_kernel, alpha=alpha),
        grid=(t // tm,),
        in_specs=[pl.BlockSpec((tm, d), row(0)), pl.BlockSpec((tm, d), row(0)),
                  pl.BlockSpec((tm, d), row(3)), pl.BlockSpec((tm, d), row(4)),
                  pl.BlockSpec((tm, d), row(0)),
                  pl.BlockSpec((d, d), full), pl.BlockSpec((d, d), full), pl.BlockSpec((d, d), full),
                  pl.BlockSpec((1, d), full), pl.BlockSpec((1, d), full)],
        out_specs=pl.BlockSpec((tm, d), row(0)),
        out_shape=jax.ShapeDtypeStruct((t, d), F32),
        compiler_params=pltpu.CompilerParams(
            dimension_semantics=("parallel",), vmem_limit_bytes=VMEM_LIMIT),
        name="merge_ln1",
    )(o_hg, o_sb, p32, p32, x, wbh, wbs, wo, lg, lb)


def _route_kernel(x_ref, wh_ref, wl_ref, bias_ref, u_ref, ones_ref,
                  idx_ref, rank_ref, w_ref, cnt_ref, carry, *, tile, n_exp):
    @pl.when(pl.program_id(0) == 0)
    def _():
        carry[...] = jnp.zeros_like(carry)

    x = x_ref[...]
    xh = x.astype(BF16)
    xl = (x - xh.astype(F32)).astype(BF16)
    wh = wh_ref[...]
    nt = functools.partial(lax.dot_general, dimension_numbers=NT_DIMS, preferred_element_type=F32)
    logits = nt(wh, xh) + nt(wh, xl) + nt(wl_ref[...], xh)
    scores = _sigmoid(logits)
    biased = scores + bias_ref[...]

    neg_inf = -jnp.inf
    gsz = n_exp // N_GROUPS
    groups = [biased[g * gsz:(g + 1) * gsz] for g in range(N_GROUPS)]
    gscore = []
    for xg in groups:
        m1 = jnp.max(xg, axis=0, keepdims=True)
        n1 = jnp.sum(jnp.where(xg == m1, 1.0, 0.0), axis=0, keepdims=True)
        m2 = jnp.max(jnp.where(xg < m1, xg, neg_inf), axis=0, keepdims=True)
        gscore.append(m1 + jnp.where(n1 >= 2.0, m1, m2))
    masked = []
    for g in range(N_GROUPS):
        beaten_by = jnp.zeros_like(gscore[g])
        for h in range(N_GROUPS):
            if h < g:
                beaten_by = beaten_by + jnp.where(gscore[h] >= gscore[g], 1.0, 0.0)
            elif h > g:
                beaten_by = beaten_by + jnp.where(gscore[h] > gscore[g], 1.0, 0.0)
        keep = jnp.broadcast_to(beaten_by, (gsz, tile)) < float(TOPK_GROUPS)
        masked.append(jnp.where(keep, groups[g], jnp.finfo(F32).min))
    cur = jnp.concatenate(masked, axis=0)

    eid = lax.broadcasted_iota(jnp.int32, (n_exp, tile), 0).astype(F32)
    chosen = jnp.zeros((n_exp, tile), F32)
    idxs, ws = [], []
    for _ in range(TOP_K):
        m = jnp.max(cur, axis=0, keepdims=True)
        ik = jnp.min(jnp.where(cur == m, eid, float(n_exp)), axis=0, keepdims=True)
        sel = eid == ik
        idxs.append(ik)
        ws.append(jnp.sum(jnp.where(sel, scores, 0.0), axis=0, keepdims=True))
        chosen = chosen + jnp.where(sel, 1.0, 0.0)
        cur = jnp.where(sel, neg_inf, cur)

    chosen_b = chosen.astype(BF16)
    run = carry[...]
    rank = _dot(chosen_b, u_ref[...]) + jnp.concatenate([run] * (tile // LANES), axis=1)
    ranks = [jnp.sum(jnp.where(eid == ik, rank, 0.0), axis=0, keepdims=True) for ik in idxs]
    run = run + _dot(chosen_b, ones_ref[...])
    carry[...] = run
    cnt_ref[...] = run

    wsum = ws[0]
    for wk in ws[1:]:
        wsum = wsum + wk
    idx_ref[...] = jnp.concatenate(idxs, axis=0).astype(jnp.int32)
    rank_ref[...] = jnp.concatenate(ranks, axis=0).astype(jnp.int32)
    w_ref[...] = jnp.concatenate([wk / wsum * ROUTED_SCALE for wk in ws], axis=0)


def _route(x1, w_router, router_bias, tile):
    t, d = x1.shape
    n_exp = w_router.shape[1]
    wt = w_router.astype(F32).T
    wh = wt.astype(BF16)
    wl = (wt - wh.astype(F32)).astype(BF16)
    r = jnp.arange(tile)
    u = (r[:, None] < r[None, :]).astype(BF16)
    ones = jnp.ones((tile, LANES), BF16)
    full = lambda i: (0, 0)
    tok = lambda i: (0, i)
    return pl.pallas_call(
        functools.partial(_route_kernel, tile=tile, n_exp=n_exp),
        grid=(t // tile,),
        in_specs=[pl.BlockSpec((tile, d), lambda i: (i, 0)),
                  pl.BlockSpec((n_exp, d), full), pl.BlockSpec((n_exp, d), full),
                  pl.BlockSpec((n_exp, 1), full),
                  pl.BlockSpec((tile, tile), full), pl.BlockSpec((tile, LANES), full)],
        out_specs=[pl.BlockSpec((TOP_K, tile), tok), pl.BlockSpec((TOP_K, tile), tok),
                   pl.BlockSpec((TOP_K, tile), tok), pl.BlockSpec((n_exp, LANES), full)],
        out_shape=[jax.ShapeDtypeStruct((TOP_K, t), jnp.int32), jax.ShapeDtypeStruct((TOP_K, t), jnp.int32),
                   jax.ShapeDtypeStruct((TOP_K, t), F32), jax.ShapeDtypeStruct((n_exp, LANES), F32)],
        scratch_shapes=[pltpu.VMEM((n_exp, LANES), F32)],
        compiler_params=pltpu.CompilerParams(
            dimension_semantics=("arbitrary",), vmem_limit_bytes=VMEM_LIMIT),
        name="router_topk",
    )(x1, wh, wl, router_bias.astype(F32).reshape(n_exp, 1), u, ones)


def _dispatch_kernel(pstart_ref, pend_ref, idx_hbm, rank_hbm, x_hbm, xs_hbm,
                     idx_s, rank_s, zbuf, isem, dsem, zsem, *, td, nchunk, mb, n_exp, nsteps, n_blocks):
    i = pl.program_id(0)
    rows = mb * nchunk

    def idx_copies(step):
        slot = step % 2
        cols = pl.ds(step * td, td)
        return (pltpu.make_async_copy(idx_hbm.at[:, cols], idx_s.at[slot], isem.at[slot]),
                pltpu.make_async_copy(rank_hbm.at[:, cols], rank_s.at[slot], isem.at[slot]))

    def wait_rows():
        n = td * TOP_K * nchunk
        pltpu.make_async_copy(xs_hbm.at[pl.ds(0, n), :], xs_hbm.at[pl.ds(0, n), :], dsem).wait()

    @pl.when(i == 0)
    def _():
        for cp in idx_copies(0):
            cp.start()
        zbuf[...] = jnp.zeros_like(zbuf)

        def zero_copy(e):
            start = pl.multiple_of((pend_ref[e] - mb) * nchunk, nchunk)
            return pltpu.make_async_copy(zbuf, xs_hbm.at[pl.ds(start, rows), :], zsem)

        def zstart(e, _):
            @pl.when(pend_ref[e] > pstart_ref[e])
            def _():
                zero_copy(e).start()
            return 0

        def zwait(e, _):
            @pl.when(pend_ref[e] > pstart_ref[e])
            def _():
                zero_copy(e).wait()
            return 0

        lax.fori_loop(0, n_exp, zstart, 0)
        lax.fori_loop(0, n_exp, zwait, 0)

        def tail_copy(b):
            start = pl.multiple_of(b * rows, rows)
            return pltpu.make_async_copy(zbuf, xs_hbm.at[pl.ds(start, rows), :], zsem)

        def tstart(b, _):
            tail_copy(b).start()
            return 0

        def twait(b, _):
            tail_copy(b).wait()
            return 0

        first_unused = pend_ref[n_exp - 1] // mb
        lax.fori_loop(first_unused, n_blocks, tstart, 0)
        lax.fori_loop(first_unused, n_blocks, twait, 0)

    for cp in idx_copies(i):
        cp.wait()

    @pl.when(i + 1 < nsteps)
    def _():
        for cp in idx_copies(i + 1):
            cp.start()

    slot = i % 2

    def body(r, _):
        src = pl.multiple_of((i * td + r) * nchunk, nchunk)
        for k in range(TOP_K):
            e = idx_s[slot, k, r]
            dst = pl.multiple_of((pstart_ref[e] + rank_s[slot, k, r]) * nchunk, nchunk)
            pltpu.make_async_copy(x_hbm.at[pl.ds(src, nchunk), :], xs_hbm.at[pl.ds(dst, nchunk), :],
                                  dsem).start()
        return 0

    lax.fori_loop(0, td, body, 0, unroll=2)

    @pl.when(i > 0)
    def _():
        wait_rows()

    @pl.when(i == nsteps - 1)
    def _():
        wait_rows()


def _dispatch(x2, idx_t, rank_t, pstart, pend, p_rows, td):
    n_exp = pstart.shape[0]
    t = idx_t.shape[1]
    nchunk = x2.shape[0] // t
    nsteps = t // td
    grid_spec = pltpu.PrefetchScalarGridSpec(
        num_scalar_prefetch=2,
        grid=(nsteps,),
        in_specs=[pl.BlockSpec(memory_space=pl.ANY)] * 3,
        out_specs=pl.BlockSpec(memory_space=pl.ANY),
        scratch_shapes=[pltpu.SMEM((2, TOP_K, td), jnp.int32),
                        pltpu.SMEM((2, TOP_K, td), jnp.int32),
                        pltpu.VMEM((MOE_BLOCK * nchunk, LANES), F32),
                        pltpu.SemaphoreType.DMA((2,)),
                        pltpu.SemaphoreType.DMA,
                        pltpu.SemaphoreType.DMA],
    )
    return pl.pallas_call(
        functools.partial(_dispatch_kernel, td=td, nchunk=nchunk, mb=MOE_BLOCK, n_exp=n_exp, nsteps=nsteps,
                          n_blocks=p_rows // MOE_BLOCK),
        grid_spec=grid_spec,
        out_shape=jax.ShapeDtypeStruct((p_rows * nchunk, LANES), F32),
        compiler_params=pltpu.CompilerParams(
            dimension_semantics=("arbitrary",), vmem_limit_bytes=VMEM_LIMIT),
        name="moe_dispatch",
    )(pstart, pend, idx_t, rank_t, x2)


def _experts_kernel(bexp_ref, nused_ref, xs_ref, wg_ref, wu_ref, wd_ref, ys_ref, wg_b, wu_b, wd_b,
                    *, mb, nchunk):
    i = pl.program_id(0)

    @pl.when(i < nused_ref[0])
    def _():
        new_expert = jnp.logical_or(i == 0, bexp_ref[i] != bexp_ref[jnp.maximum(i - 1, 0)])

        @pl.when(new_expert)
        def _():
            wg_b[...] = wg_ref[0].astype(BF16)
            wu_b[...] = wu_ref[0].astype(BF16)
            wd_b[...] = wd_ref[0].astype(BF16)

        xs = [xs_ref[pl.ds(c, mb, stride=nchunk), :] for c in range(nchunk)]
        xb = jnp.concatenate(xs, axis=1).astype(BF16)
        gate = _dot(xb, wg_b[...])
        up = _dot(xb, wu_b[...])
        hid = (gate * _sigmoid(gate)) * up
        y = _dot(hid.astype(BF16), wd_b[...])
        for c in range(nchunk):
            ys_ref[pl.ds(c, mb, stride=nchunk), :] = y[:, c * LANES:(c + 1) * LANES]

    @pl.when(i >= nused_ref[0])
    def _():
        ys_ref[...] = jnp.zeros_like(ys_ref)


def _experts(xs, block_expert, nused, w_gate, w_up, w_down):
    e, d, de = w_gate.shape
    nchunk = d // LANES
    mb = MOE_BLOCK
    rows = mb * nchunk
    n_blocks = xs.shape[0] // rows
    blk = lambda i, be, nu: (jnp.minimum(i, nu[0] - 1), 0)
    wsel = lambda i, be, nu: (be[i], 0, 0)
    grid_spec = pltpu.PrefetchScalarGridSpec(
        num_scalar_prefetch=2,
        grid=(n_blocks,),
        in_specs=[pl.BlockSpec((rows, LANES), blk),
                  pl.BlockSpec((1, d, de), wsel), pl.BlockSpec((1, d, de), wsel),
                  pl.BlockSpec((1, de, d), wsel)],
        out_specs=pl.BlockSpec((rows, LANES), lambda i, be, nu: (i, 0)),
        scratch_shapes=[pltpu.VMEM((d, de), BF16), pltpu.VMEM((d, de), BF16), pltpu.VMEM((de, d), BF16)],
    )
    return pl.pallas_call(
        functools.partial(_experts_kernel, mb=mb, nchunk=nchunk),
        grid_spec=grid_spec,
        out_shape=jax.ShapeDtypeStruct(xs.shape, F32),
        compiler_params=pltpu.CompilerParams(
            dimension_semantics=("arbitrary",), vmem_limit_bytes=VMEM_LIMIT),
        name="routed_experts",
    )(block_expert, nused, xs, w_gate, w_up, w_down)


def _final_kernel(pstart_ref, idx_hbm, rank_hbm, w_hbm, ys_hbm, x1_ref, wsg_ref, wsu_ref, wsd_ref,
                  lg_ref, lb_ref, o_ref, ybuf, rbuf, idx_s, rank_s, w_s, isem, gsem,
                  *, tm, nchunk, alpha, nsteps):
    i = pl.program_id(0)

    def idx_copies(step):
        slot = step % 3
        cols = pl.ds(step * tm, tm)
        return (pltpu.make_async_copy(idx_hbm.at[:, cols], idx_s.at[slot], isem.at[slot]),
                pltpu.make_async_copy(rank_hbm.at[:, cols], rank_s.at[slot], isem.at[slot]),
                pltpu.make_async_copy(w_hbm.at[:, cols], w_s.at[slot], isem.at[slot]))

    def start_gather(step):
        islot = step % 3
        slot = step % 2

        def body(r, _):
            for k in range(TOP_K):
                e = idx_s[islot, k, r]
                src = pl.multiple_of((pstart_ref[e] + rank_s[islot, k, r]) * nchunk, nchunk)
                pltpu.make_async_copy(ys_hbm.at[pl.ds(src, nchunk), :],
                                      ybuf.at[slot, pl.ds((r * TOP_K + k) * nchunk, nchunk), :],
                                      gsem.at[slot]).start()
            return 0

        lax.fori_loop(0, tm, body, 0, unroll=2)

    @pl.when(i == 0)
    def _():
        for cp in idx_copies(0):
            cp.start()
        for cp in idx_copies(0):
            cp.wait()
        start_gather(0)

        @pl.when(nsteps > 1)
        def _():
            for cp in idx_copies(1):
                cp.start()

    @pl.when(i + 1 < nsteps)
    def _():
        for cp in idx_copies(i + 1):
            cp.wait()
        start_gather(i + 1)

    @pl.when(i + 2 < nsteps)
    def _():
        for cp in idx_copies(i + 2):
            cp.start()

    slot = i % 2
    islot = i % 3
    pltpu.make_async_copy(ybuf.at[slot], ybuf.at[slot], gsem.at[slot]).wait()

    def combine(r, _):
        acc = jnp.zeros((nchunk, LANES), F32)
        for k in range(TOP_K):
            acc = acc + ybuf[slot, pl.ds((r * TOP_K + k) * nchunk, nchunk), :] * w_s[islot, k, r]
        rbuf[pl.ds(r * nchunk, nchunk), :] = acc
        return 0

    lax.fori_loop(0, tm, combine, 0, unroll=2)
    routed = jnp.concatenate([rbuf[pl.ds(c, tm, stride=nchunk), :] for c in range(nchunk)], axis=1)

    x1 = x1_ref[...]
    xb = x1.astype(BF16)
    gate = _dot(xb, wsg_ref[...])
    up = _dot(xb, wsu_ref[...])
    shared = _dot(((gate * _sigmoid(gate)) * up).astype(BF16), wsd_ref[...])
    o_ref[...] = _layer_norm(alpha * x1 + (routed + shared), lg_ref[...], lb_ref[...])


def _final(ys, idx_t, rank_t, w_t, pstart, x1, wsg, wsu, wsd, lg, lb, alpha, tm):
    t, d = x1.shape
    nchunk = d // LANES
    ds = wsg.shape[1]
    nsteps = t // tm
    full = lambda i, ps: (0, 0)
    grid_spec = pltpu.PrefetchScalarGridSpec(
        num_scalar_prefetch=1,
        grid=(nsteps,),
        in_specs=[pl.BlockSpec(memory_space=pl.ANY)] * 4 + [
            pl.BlockSpec((tm, d), lambda i, ps: (i, 0)),
            pl.BlockSpec((d, ds), full), pl.BlockSpec((d, ds), full), pl.BlockSpec((ds, d), full),
            pl.BlockSpec((1, d), full), pl.BlockSpec((1, d), full)],
        out_specs=pl.BlockSpec((tm, d), lambda i, ps: (i, 0)),
        scratch_shapes=[pltpu.VMEM((2, tm * TOP_K * nchunk, LANES), F32),
                        pltpu.VMEM((tm * nchunk, LANES), F32),
                        pltpu.SMEM((3, TOP_K, tm), jnp.int32),
                        pltpu.SMEM((3, TOP_K, tm), jnp.int32),
                        pltpu.SMEM((3, TOP_K, tm), F32),
                        pltpu.SemaphoreType.DMA((3,)),
                        pltpu.SemaphoreType.DMA((2,))],
    )
    return pl.pallas_call(
        functools.partial(_final_kernel, tm=tm, nchunk=nchunk, alpha=alpha, nsteps=nsteps),
        grid_spec=grid_spec,
        out_shape=jax.ShapeDtypeStruct((t, d), F32),
        compiler_params=pltpu.CompilerParams(
            dimension_semantics=("arbitrary",), vmem_limit_bytes=VMEM_LIMIT),
        name="combine_shared_ln2",
    )(pstart, idx_t, rank_t, w_t, ys, x1, wsg, wsu, wsd, lg, lb)


def _block_tables(counts, n_rows):
    mb = MOE_BLOCK
    n_exp = counts.shape[0]
    padded = (counts + mb - 1) // mb * mb
    pad_end = jnp.cumsum(padded)
    pad_start = pad_end - padded
    n_blocks = n_rows // mb
    block_expert = jnp.minimum(
        jnp.searchsorted(pad_end, jnp.arange(n_blocks, dtype=jnp.int32) * mb, side='right'),
        n_exp - 1).astype(jnp.int32)
    nused = (pad_end[-1:] // mb).astype(jnp.int32)
    return pad_start.astype(jnp.int32), pad_end.astype(jnp.int32), block_expert, nused


def _tile(n, pref):
    while n % pref:
        pref //= 2
    return pref


def kernel(x, w_in, lower_bounds, hg_norm_g, w_branch_hg, w_branch_sb, w_out, ln1_g, ln1_b,
           w_router, router_bias, w_exp_gate, w_exp_up, w_exp_down,
           w_sh_gate, w_sh_up, w_sh_down, ln2_g, ln2_b):
    depth = w_in.shape[0]
    assert depth == 1, "single-layer block only"
    batch, seq, d = x.shape
    t = batch * seq
    n_exp = w_router.shape[-1]
    alpha = (2.0 * depth) ** 0.25
    nchunk = d // LANES

    lb = jnp.cumsum(jax.nn.softmax(lower_bounds.astype(F32), axis=0), axis=0)[0]
    xf = x.reshape(t, d)

    w = w_in[0]
    c = d
    w32 = jnp.concatenate([w[:, 0:2 * c], w[:, 3 * c:4 * c], w[:, 7 * c:9 * c]], axis=1).astype(BF16)
    wbf = jnp.concatenate([w[:, 2 * c:3 * c], w[:, 4 * c:7 * c]], axis=1).astype(BF16)
    tm = _tile(t, 1024)
    p32 = _matmul(xf, w32, F32, tm, 512)
    pbf = _matmul(xf, wbf, BF16, tm, 512)

    o_hg = _hgrn(p32, pbf, lb, hg_norm_g[0].astype(F32), batch, seq, _tile(seq, 256))
    o_sb = _sb_attention(pbf, batch, seq, _tile(seq, 256))

    x1 = _merge(o_hg, o_sb, p32, xf,
                w_branch_hg[0].astype(BF16), w_branch_sb[0].astype(BF16), w_out[0].astype(BF16),
                ln1_g[0].reshape(1, d).astype(F32), ln1_b[0].reshape(1, d).astype(F32),
                alpha, _tile(t, 512))

    idx_t, rank_t, w_t, cnt = _route(x1, w_router[0], router_bias[0], _tile(t, 256))
    p_rows = t * TOP_K + n_exp * MOE_BLOCK
    pstart, pend, block_expert, nused = _block_tables(cnt[:, 0].astype(jnp.int32), p_rows)
    xs = _dispatch(x1.reshape(t * nchunk, LANES), idx_t, rank_t, pstart, pend, p_rows, _tile(t, 128))
    ys = _experts(xs, block_expert, nused, w_exp_gate[0], w_exp_up[0], w_exp_down[0])
    out = _final(ys, idx_t, rank_t, w_t, pstart, x1,
                 w_sh_gate[0].astype(BF16), w_sh_up[0].astype(BF16), w_sh_down[0].astype(BF16),
                 ln2_g[0].reshape(1, d).astype(F32), ln2_b[0].reshape(1, d).astype(F32),
                 alpha, _tile(t, 128))
    return out.reshape(batch, seq, d)
```

```python
import functools

import jax
import jax.numpy as jnp
from jax import lax
from jax.experimental import pallas as pl
from jax.experimental.pallas import tpu as pltpu

F32 = jnp.float32
BF16 = jnp.bfloat16

LANES = 128
SUBLANES = 8
VMEM_LIMIT = 48 * 1024 * 1024

HG_HEADS = 8
HG_CHUNK = 32
SB_HEADS = 8
N_GROUPS = 8
TOPK_GROUPS = 4
TOP_K = 8
ROUTED_SCALE = 2.5
MOE_BLOCK = 128
LN_EPS = 1e-5
RMS_EPS = 1e-6

NT_DIMS = (((1,), (1,)), ((), ()))
TN_DIMS = (((0,), (0,)), ((), ()))


def _dot(a, b):
    return jnp.dot(a, b, preferred_element_type=F32)


def _sigmoid(x):
    return 1.0 / (1.0 + jnp.exp(-x))


def _split3(x):
    hi = x.astype(BF16)
    r1 = x - hi.astype(F32)
    mid = r1.astype(BF16)
    lo = (r1 - mid.astype(F32)).astype(BF16)
    return hi, mid, lo


def _layer_norm(r, g, b):
    mu = jnp.mean(r, axis=-1, keepdims=True)
    d = r - mu
    var = jnp.mean(d * d, axis=-1, keepdims=True)
    return d * lax.rsqrt(var + LN_EPS) * g + b


def _mm_kernel(x_ref, w_ref, o_ref):
    o_ref[...] = _dot(x_ref[...].astype(BF16), w_ref[...]).astype(o_ref.dtype)


def _matmul(x, w, out_dtype, tm, tn):
    m, k = x.shape
    n = w.shape[1]
    return pl.pallas_call(
        _mm_kernel,
        grid=(m // tm, n // tn),
        in_specs=[pl.BlockSpec((tm, k), lambda i, j: (i, 0)),
                  pl.BlockSpec((k, tn), lambda i, j: (0, j))],
        out_specs=pl.BlockSpec((tm, tn), lambda i, j: (i, j)),
        out_shape=jax.ShapeDtypeStruct((m, n), out_dtype),
        compiler_params=pltpu.CompilerParams(
            dimension_semantics=("parallel", "arbitrary"), vmem_limit_bytes=VMEM_LIMIT),
        name="in_proj",
    )(x, w)


def _hgrn_kernel(lb_ref, g_ref, q_ref, f_ref, v_ref, og_ref, o_ref, st_ref, *, ts, chunk):
    @pl.when(pl.program_id(2) == 0)
    def _():
        st_ref[...] = jnp.zeros_like(st_ref)

    lb = lb_ref[0]
    hf = f_ref[...]
    log_f = jnp.log(lb + (1.0 - lb) * _sigmoid(hf))
    k_in = (1.0 - lb) * _sigmoid(-hf)

    shift = chunk.bit_length() - 1
    row = lax.broadcasted_iota(jnp.int32, (ts, ts), 0)
    col = lax.broadcasted_iota(jnp.int32, (ts, ts), 1)
    same = (row >> shift) == (col >> shift)
    causal = jnp.logical_and(same, col <= row)
    tri = jnp.where(causal, 1.0, 0.0).astype(BF16)
    ones = jnp.where(same, 1.0, 0.0).astype(BF16)

    parts = jnp.concatenate(_split3(log_f), axis=1)
    cs = _dot(tri, parts)
    tot = _dot(ones, parts)
    d = LANES
    b = cs[:, :d] + cs[:, d:2 * d] + cs[:, 2 * d:]
    b_end = tot[:, :d] + tot[:, d:2 * d] + tot[:, 2 * d:]

    q = q_ref[...]
    q_start = (q * jnp.exp(b)).astype(BF16)
    q_end = (q * jnp.exp(b - b_end)).astype(BF16)
    k_end = (k_in * jnp.exp(b_end - b)).astype(BF16)
    v = v_ref[...]

    scores = lax.dot_general(q_end, k_end, NT_DIMS, preferred_element_type=F32)
    scores = jnp.where(causal, scores, 0.0)
    o_intra = _dot(scores.astype(BF16), v)

    dec = jnp.exp(b_end)
    st = st_ref[...]
    outs = []
    for c in range(ts // chunk):
        lo = c * chunk
        outs.append(lax.dot_general(q_start[lo:lo + chunk], st.astype(BF16), NT_DIMS,
                                    preferred_element_type=F32))
        kv = lax.dot_general(v[lo:lo + chunk], k_end[lo:lo + chunk], TN_DIMS,
                             preferred_element_type=F32)
        st = dec[lo:lo + 1, :] * st + kv
    st_ref[...] = st
    o = o_intra + jnp.concatenate(outs, axis=0)

    o = o * lax.rsqrt(jnp.mean(o * o, axis=-1, keepdims=True) + RMS_EPS)
    o = o * g_ref[0]
    hg = og_ref[...]
    o_ref[...] = (o * (hg * _sigmoid(hg))).astype(o_ref.dtype)


def _hgrn(p32, pbf, lb, g, batch, seq, ts):
    t = batch * seq
    h = HG_HEADS
    ns = seq // ts
    row = lambda off: (lambda b, hh, s: (b * ns + s, off + hh))
    par = lambda b, hh, s: (hh, 0, 0)
    return pl.pallas_call(
        functools.partial(_hgrn_kernel, ts=ts, chunk=HG_CHUNK),
        grid=(batch, h, ns),
        in_specs=[pl.BlockSpec((1, 1, LANES), par),
                  pl.BlockSpec((1, 1, LANES), par),
                  pl.BlockSpec((ts, LANES), row(0)),
                  pl.BlockSpec((ts, LANES), row(h)),
                  pl.BlockSpec((ts, LANES), row(0)),
                  pl.BlockSpec((ts, LANES), row(2 * h))],
        out_specs=pl.BlockSpec((ts, LANES), row(0)),
        out_shape=jax.ShapeDtypeStruct((t, h * LANES), BF16),
        scratch_shapes=[pltpu.VMEM((LANES, LANES), F32)],
        compiler_params=pltpu.CompilerParams(
            dimension_semantics=("parallel", "parallel", "arbitrary"), vmem_limit_bytes=VMEM_LIMIT),
        name="hgrn2",
    )(lb.reshape(h, 1, LANES), g.reshape(h, 1, LANES), p32, p32, pbf, p32)


def _sb_kernel(q_ref, k_ref, v_ref, u_ref, o_ref, *, tq, scale):
    i = pl.program_id(2)
    q = q_ref[...]
    u = u_ref[...]
    rep = tq // LANES

    def block(j, carry, acc, masked):
        start = pl.multiple_of(j * tq, tq)
        kj = k_ref[pl.ds(start, tq), :]
        vj = v_ref[pl.ds(start, tq), :]
        z = lax.dot_general(q, kj, NT_DIMS, preferred_element_type=F32) * scale
        ls_neg = -(jnp.maximum(z, 0.0) + jnp.log1p(jnp.exp(-jnp.abs(z))))
        if masked:
            row = lax.broadcasted_iota(jnp.int32, (tq, tq), 0)
            col = lax.broadcasted_iota(jnp.int32, (tq, tq), 1)
            before = col < row
            log_keep = jnp.where(before, ls_neg, 0.0)
        else:
            log_keep = ls_neg
        hi = log_keep.astype(BF16)
        lo = (log_keep - hi.astype(F32)).astype(BF16)
        r = _dot(hi, u) + _dot(lo, u)
        stick = r[:, :tq] + jnp.concatenate([carry] * rep, axis=1)
        w = jnp.exp(ls_neg + z + stick)
        if masked:
            w = jnp.where(before, w, 0.0)
        acc = acc + _dot(w.astype(BF16), vj)
        carry = carry + r[:, tq:tq + LANES]
        return carry, acc

    zero = jnp.zeros((tq, LANES), F32)
    carry, acc = block(i, zero, zero, True)

    def body(jj, state):
        return block(i - 1 - jj, state[0], state[1], False)

    carry, acc = lax.fori_loop(0, i, body, (carry, acc))
    o_ref[...] = acc.astype(o_ref.dtype)


def _sb_attention(pbf, batch, seq, tq):
    t = batch * seq
    h = SB_HEADS
    nq = seq // tq
    r = jnp.arange(tq)
    later = (r[:, None] > r[None, :]).astype(BF16)
    u = jnp.concatenate([later, jnp.ones((tq, tq), BF16)], axis=1)
    return pl.pallas_call(
        functools.partial(_sb_kernel, tq=tq, scale=float(LANES) ** -0.5),
        grid=(batch, h, nq),
        in_specs=[pl.BlockSpec((tq, LANES), lambda b, hh, i: (b * nq + i, h + hh)),
                  pl.BlockSpec((seq, LANES), lambda b, hh, i: (b, 2 * h + hh)),
                  pl.BlockSpec((seq, LANES), lambda b, hh, i: (b, 3 * h + hh)),
                  pl.BlockSpec((tq, 2 * tq), lambda b, hh, i: (0, 0))],
        out_specs=pl.BlockSpec((tq, LANES), lambda b, hh, i: (b * nq + i, hh)),
        out_shape=jax.ShapeDtypeStruct((t, h * LANES), BF16),
        compiler_params=pltpu.CompilerParams(
            dimension_semantics=("parallel", "parallel", "arbitrary"), vmem_limit_bytes=VMEM_LIMIT),
        name="stick_breaking",
    )(pbf, pbf, pbf, u)


def _merge_kernel(ohg_ref, osb_ref, ghg_ref, gsb_ref, x_ref, wbh_ref, wbs_ref, wo_ref,
                  lg_ref, lb_ref, x1_ref, *, alpha):
    y_hg = _dot(ohg_ref[...], wbh_ref[...])
    y_sb = _dot(osb_ref[...], wbs_ref[...])
    merged = _sigmoid(ghg_ref[...]) * y_hg + _sigmoid(gsb_ref[...]) * y_sb
    hmix = _dot(merged.astype(BF16), wo_ref[...])
    x1_ref[...] = _layer_norm(alpha * x_ref[...] + hmix, lg_ref[...], lb_ref[...])


def _merge(o_hg, o_sb, p32, x, wbh, wbs, wo, lg, lb, alpha, tm):
    t, d = x.shape
    row = lambda off: (lambda i: (i, off))
    full = lambda i: (0, 0)
    return pl.pallas_call(
        functools.partial(_merge_kernel, alpha=alpha),
        grid=(t // tm,),
        in_specs=[pl.BlockSpec((tm, d), row(0)), pl.BlockSpec((tm, d), row(0)),
                  pl.BlockSpec((tm, d), row(3)), pl.BlockSpec((tm, d), row(4)),
                  pl.BlockSpec((tm, d), row(0)),
                  pl.BlockSpec((d, d), full), pl.BlockSpec((d, d), full), pl.BlockSpec((d, d), full),
                  pl.BlockSpec((1, d), full), pl.BlockSpec((1, d), full)],
        out_specs=pl.BlockSpec((tm, d), row(0)),
        out_shape=jax.ShapeDtypeStruct((t, d), F32),
        compiler_params=pltpu.CompilerParams(
            dimension_semantics=("parallel",), vmem_limit_bytes=VMEM_LIMIT),
        name="merge_ln1",
    )(o_hg, o_sb, p32, p32, x, wbh, wbs, wo, lg, lb)


def _route_kernel(x_ref, wh_ref, wl_ref, bias_ref, u_ref, ones_ref,
                  idx_ref, rank_ref, w_ref, cnt_ref, carry, *, tile, n_exp):
    @pl.when(pl.program_id(0) == 0)
    def _():
        carry[...] = jnp.zeros_like(carry)

    x = x_ref[...]
    xh = x.astype(BF16)
    xl = (x - xh.astype(F32)).astype(BF16)
    wh = wh_ref[...]
    nt = functools.partial(lax.dot_general, dimension_numbers=NT_DIMS, preferred_element_type=F32)
    logits = nt(wh, xh) + nt(wh, xl) + nt(wl_ref[...], xh)
    scores = _sigmoid(logits)
    biased = scores + bias_ref[...]

    neg_inf = -jnp.inf
    gsz = n_exp // N_GROUPS
    groups = [biased[g * gsz:(g + 1) * gsz] for g in range(N_GROUPS)]
    gscore = []
    for xg in groups:
        m1 = jnp.max(xg, axis=0, keepdims=True)
        n1 = jnp.sum(jnp.where(xg == m1, 1.0, 0.0), axis=0, keepdims=True)
        m2 = jnp.max(jnp.where(xg < m1, xg, neg_inf), axis=0, keepdims=True)
        gscore.append(m1 + jnp.where(n1 >= 2.0, m1, m2))
    masked = []
    for g in range(N_GROUPS):
        beaten_by = jnp.zeros_like(gscore[g])
        for h in range(N_GROUPS):
            if h < g:
                beaten_by = beaten_by + jnp.where(gscore[h] >= gscore[g], 1.0, 0.0)
            elif h > g:
                beaten_by = beaten_by + jnp.where(gscore[h] > gscore[g], 1.0, 0.0)
        keep = jnp.broadcast_to(beaten_by, (gsz, tile)) < float(TOPK_GROUPS)
        masked.append(jnp.where(keep, groups[g], jnp.finfo(F32).min))
    cur = jnp.concatenate(masked, axis=0)

    eid = lax.broadcasted_iota(jnp.int32, (n_exp, tile), 0).astype(F32)
    chosen = jnp.zeros((n_exp, tile), F32)
    idxs, ws = [], []
    for _ in range(TOP_K):
        m = jnp.max(cur, axis=0, keepdims=True)
        ik = jnp.min(jnp.where(cur == m, eid, float(n_exp)), axis=0, keepdims=True)
        sel = eid == ik
        idxs.append(ik)
        ws.append(jnp.sum(jnp.where(sel, scores, 0.0), axis=0, keepdims=True))
        chosen = chosen + jnp.where(sel, 1.0, 0.0)
        cur = jnp.where(sel, neg_inf, cur)

    chosen_b = chosen.astype(BF16)
    run = carry[...]
    rank = _dot(chosen_b, u_ref[...]) + jnp.concatenate([run] * (tile // LANES), axis=1)
    ranks = [jnp.sum(jnp.where(eid == ik, rank, 0.0), axis=0, keepdims=True) for ik in idxs]
    run = run + _dot(chosen_b, ones_ref[...])
    carry[...] = run
    cnt_ref[...] = run

    wsum = ws[0]
    for wk in ws[1:]:
        wsum = wsum + wk
    idx_ref[...] = jnp.concatenate(idxs, axis=0).astype(jnp.int32)
    rank_ref[...] = jnp.concatenate(ranks, axis=0).astype(jnp.int32)
    w_ref[...] = jnp.concatenate([wk / wsum * ROUTED_SCALE for wk in ws], axis=0)


def _route(x1, w_router, router_bias, tile):
    t, d = x1.shape
    n_exp = w_router.shape[1]
    wt = w_router.astype(F32).T
    wh = wt.astype(BF16)
    wl = (wt - wh.astype(F32)).astype(BF16)
    r = jnp.arange(tile)
    u = (r[:, None] < r[None, :]).astype(BF16)
    ones = jnp.ones((tile, LANES), BF16)
    full = lambda i: (0, 0)
    tok = lambda i: (0, i)
    return pl.pallas_call(
        functools.partial(_route_kernel, tile=tile, n_exp=n_exp),
        grid=(t // tile,),
        in_specs=[pl.BlockSpec((tile, d), lambda i: (i, 0)),
                  pl.BlockSpec((n_exp, d), full), pl.BlockSpec((n_exp, d), full),
                  pl.BlockSpec((n_exp, 1), full),
                  pl.BlockSpec((tile, tile), full), pl.BlockSpec((tile, LANES), full)],
        out_specs=[pl.BlockSpec((TOP_K, tile), tok), pl.BlockSpec((TOP_K, tile), tok),
                   pl.BlockSpec((TOP_K, tile), tok), pl.BlockSpec((n_exp, LANES), full)],
        out_shape=[jax.ShapeDtypeStruct((TOP_K, t), jnp.int32), jax.ShapeDtypeStruct((TOP_K, t), jnp.int32),
                   jax.ShapeDtypeStruct((TOP_K, t), F32), jax.ShapeDtypeStruct((n_exp, LANES), F32)],
        scratch_shapes=[pltpu.VMEM((n_exp, LANES), F32)],
        compiler_params=pltpu.CompilerParams(
            dimension_semantics=("arbitrary",), vmem_limit_bytes=VMEM_LIMIT),
        name="router_topk",
    )(x1, wh, wl, router_bias.astype(F32).reshape(n_exp, 1), u, ones)


def _dispatch_kernel(pstart_ref, pend_ref, idx_hbm, rank_hbm, x_hbm, xs_hbm,
                     idx_s, rank_s, xbuf, zbuf, isem, lsem, dsem, zsem,
                     *, td, nchunk, mb, n_exp, nsteps, n_blocks):
    i = pl.program_id(0)
    rows = mb * nchunk
    trows = td * nchunk

    def idx_copies(step):
        slot = step % 2
        cols = pl.ds(step * td, td)
        return (pltpu.make_async_copy(idx_hbm.at[:, cols], idx_s.at[slot], isem.at[slot]),
                pltpu.make_async_copy(rank_hbm.at[:, cols], rank_s.at[slot], isem.at[slot]))

    def tile_load(step):
        slot = step % 3
        start = pl.multiple_of(step * trows, trows)
        return pltpu.make_async_copy(x_hbm.at[pl.ds(start, trows), :], xbuf.at[slot], lsem.at[slot])

    def wait_rows(step):
        slot = step % 3
        for _ in range(TOP_K):
            pltpu.make_async_copy(xbuf.at[slot], xbuf.at[slot], dsem.at[slot]).wait()

    @pl.when(i == 0)
    def _():
        for cp in idx_copies(0):
            cp.start()
        tile_load(0).start()
        zbuf[...] = jnp.zeros_like(zbuf)

        def zero_copy(e):
            start = pl.multiple_of((pend_ref[e] - mb) * nchunk, nchunk)
            return pltpu.make_async_copy(zbuf, xs_hbm.at[pl.ds(start, rows), :], zsem)

        def zstart(e, _):
            @pl.when(pend_ref[e] > pstart_ref[e])
            def _():
                zero_copy(e).start()
            return 0

        def zwait(e, _):
            @pl.when(pend_ref[e] > pstart_ref[e])
            def _():
                zero_copy(e).wait()
            return 0

        lax.fori_loop(0, n_exp, zstart, 0)
        lax.fori_loop(0, n_exp, zwait, 0)

        def tail_copy(b):
            start = pl.multiple_of(b * rows, rows)
            return pltpu.make_async_copy(zbuf, xs_hbm.at[pl.ds(start, rows), :], zsem)

        def tstart(b, _):
            tail_copy(b).start()
            return 0

        def twait(b, _):
            tail_copy(b).wait()
            return 0

        first_unused = pend_ref[n_exp - 1] // mb
        lax.fori_loop(first_unused, n_blocks, tstart, 0)
        lax.fori_loop(first_unused, n_blocks, twait, 0)

    for cp in idx_copies(i):
        cp.wait()

    @pl.when(i >= 2)
    def _():
        wait_rows(i - 2)

    @pl.when(i + 1 < nsteps)
    def _():
        for cp in idx_copies(i + 1):
            cp.start()
        tile_load(i + 1).start()

    tile_load(i).wait()
    slot = i % 2
    xslot = i % 3

    def body(r, _):
        src = pl.multiple_of(r * nchunk, nchunk)
        for k in range(TOP_K):
            e = idx_s[slot, k, r]
            dst = pl.multiple_of((pstart_ref[e] + rank_s[slot, k, r]) * nchunk, nchunk)
            pltpu.make_async_copy(xbuf.at[xslot, pl.ds(src, nchunk), :], xs_hbm.at[pl.ds(dst, nchunk), :],
                                  dsem.at[xslot]).start()
        return 0

    lax.fori_loop(0, td, body, 0, unroll=2)

    @pl.when(i == nsteps - 1)
    def _():
        @pl.when(i >= 1)
        def _():
            wait_rows(i - 1)

        wait_rows(i)


def _dispatch(x2, idx_t, rank_t, pstart, pend, p_rows, td):
    n_exp = pstart.shape[0]
    t = idx_t.shape[1]
    nchunk = x2.shape[0] // t
    nsteps = t // td
    grid_spec = pltpu.PrefetchScalarGridSpec(
        num_scalar_prefetch=2,
        grid=(nsteps,),
        in_specs=[pl.BlockSpec(memory_space=pl.ANY)] * 3,
        out_specs=pl.BlockSpec(memory_space=pl.ANY),
        scratch_shapes=[pltpu.SMEM((2, TOP_K, td), jnp.int32),
                        pltpu.SMEM((2, TOP_K, td), jnp.int32),
                        pltpu.VMEM((3, td * nchunk, LANES), F32),
                        pltpu.VMEM((MOE_BLOCK * nchunk, LANES), F32),
                        pltpu.SemaphoreType.DMA((2,)),
                        pltpu.SemaphoreType.DMA((3,)),
                        pltpu.SemaphoreType.DMA((3,)),
                        pltpu.SemaphoreType.DMA],
    )
    return pl.pallas_call(
        functools.partial(_dispatch_kernel, td=td, nchunk=nchunk, mb=MOE_BLOCK, n_exp=n_exp, nsteps=nsteps,
                          n_blocks=p_rows // MOE_BLOCK),
        grid_spec=grid_spec,
        out_shape=jax.ShapeDtypeStruct((p_rows * nchunk, LANES), F32),
        compiler_params=pltpu.CompilerParams(
            dimension_semantics=("arbitrary",), vmem_limit_bytes=VMEM_LIMIT),
        name="moe_dispatch",
    )(pstart, pend, idx_t, rank_t, x2)


def _experts_kernel(bexp_ref, nused_ref, xs_ref, wg_ref, wu_ref, wd_ref, ys_ref, wg_b, wu_b, wd_b,
                    *, mb, nchunk):
    i = pl.program_id(0)

    @pl.when(i < nused_ref[0])
    def _():
        new_expert = jnp.logical_or(i == 0, bexp_ref[i] != bexp_ref[jnp.maximum(i - 1, 0)])

        @pl.when(new_expert)
        def _():
            wg_b[...] = wg_ref[0].astype(BF16)
            wu_b[...] = wu_ref[0].astype(BF16)
            wd_b[...] = wd_ref[0].astype(BF16)

        xs = [xs_ref[pl.ds(c, mb, stride=nchunk), :] for c in range(nchunk)]
        xb = jnp.concatenate(xs, axis=1).astype(BF16)
        gate = _dot(xb, wg_b[...])
        up = _dot(xb, wu_b[...])
        hid = (gate * _sigmoid(gate)) * up
        y = _dot(hid.astype(BF16), wd_b[...])
        for c in range(nchunk):
            ys_ref[pl.ds(c, mb, stride=nchunk), :] = y[:, c * LANES:(c + 1) * LANES]

    @pl.when(i >= nused_ref[0])
    def _():
        ys_ref[...] = jnp.zeros_like(ys_ref)


def _experts(xs, block_expert, nused, w_gate, w_up, w_down):
    e, d, de = w_gate.shape
    nchunk = d // LANES
    mb = MOE_BLOCK
    rows = mb * nchunk
    n_blocks = xs.shape[0] // rows
    blk = lambda i, be, nu: (jnp.minimum(i, nu[0] - 1), 0)
    wsel = lambda i, be, nu: (be[i], 0, 0)
    grid_spec = pltpu.PrefetchScalarGridSpec(
        num_scalar_prefetch=2,
        grid=(n_blocks,),
        in_specs=[pl.BlockSpec((rows, LANES), blk),
                  pl.BlockSpec((1, d, de), wsel), pl.BlockSpec((1, d, de), wsel),
                  pl.BlockSpec((1, de, d), wsel)],
        out_specs=pl.BlockSpec((rows, LANES), lambda i, be, nu: (i, 0)),
        scratch_shapes=[pltpu.VMEM((d, de), BF16), pltpu.VMEM((d, de), BF16), pltpu.VMEM((de, d), BF16)],
    )
    return pl.pallas_call(
        functools.partial(_experts_kernel, mb=mb, nchunk=nchunk),
        grid_spec=grid_spec,
        out_shape=jax.ShapeDtypeStruct(xs.shape, F32),
        compiler_params=pltpu.CompilerParams(
            dimension_semantics=("arbitrary",), vmem_limit_bytes=VMEM_LIMIT),
        name="routed_experts",
    )(block_expert, nused, xs, w_gate, w_up, w_down)


def _final_kernel(pstart_ref, idx_hbm, rank_hbm, w_hbm, ys_hbm, x1_ref, wsg_ref, wsu_ref, wsd_ref,
                  lg_ref, lb_ref, o_ref, ybuf, rbuf, idx_s, rank_s, w_s, isem, gsem,
                  *, tm, nchunk, alpha, nsteps):
    i = pl.program_id(0)

    def idx_copies(step):
        slot = step % 3
        cols = pl.ds(step * tm, tm)
        return (pltpu.make_async_copy(idx_hbm.at[:, cols], idx_s.at[slot], isem.at[slot]),
                pltpu.make_async_copy(rank_hbm.at[:, cols], rank_s.at[slot], isem.at[slot]),
                pltpu.make_async_copy(w_hbm.at[:, cols], w_s.at[slot], isem.at[slot]))

    def start_gather(step):
        islot = step % 3
        slot = step % 2

        def body(r, _):
            for k in range(TOP_K):
                e = idx_s[islot, k, r]
                src = pl.multiple_of((pstart_ref[e] + rank_s[islot, k, r]) * nchunk, nchunk)
                pltpu.make_async_copy(ys_hbm.at[pl.ds(src, nchunk), :],
                                      ybuf.at[slot, pl.ds((r * TOP_K + k) * nchunk, nchunk), :],
                                      gsem.at[slot]).start()
            return 0

        lax.fori_loop(0, tm, body, 0, unroll=2)

    @pl.when(i == 0)
    def _():
        for cp in idx_copies(0):
            cp.start()
        for cp in idx_copies(0):
            cp.wait()
        start_gather(0)

        @pl.when(nsteps > 1)
        def _():
            for cp in idx_copies(1):
                cp.start()

    @pl.when(i + 1 < nsteps)
    def _():
        for cp in idx_copies(i + 1):
            cp.wait()
        start_gather(i + 1)

    @pl.when(i + 2 < nsteps)
    def _():
        for cp in idx_copies(i + 2):
            cp.start()

    slot = i % 2
    islot = i % 3
    pltpu.make_async_copy(ybuf.at[slot], ybuf.at[slot], gsem.at[slot]).wait()

    def combine(r, _):
        acc = jnp.zeros((nchunk, LANES), F32)
        for k in range(TOP_K):
            acc = acc + ybuf[slot, pl.ds((r * TOP_K + k) * nchunk, nchunk), :] * w_s[islot, k, r]
        rbuf[pl.ds(r * nchunk, nchunk), :] = acc
        return 0

    lax.fori_loop(0, tm, combine, 0, unroll=2)
    routed = jnp.concatenate([rbuf[pl.ds(c, tm, stride=nchunk), :] for c in range(nchunk)], axis=1)

    x1 = x1_ref[...]
    xb = x1.astype(BF16)
    gate = _dot(xb, wsg_ref[...])
    up = _dot(xb, wsu_ref[...])
    shared = _dot(((gate * _sigmoid(gate)) * up).astype(BF16), wsd_ref[...])
    o_ref[...] = _layer_norm(alpha * x1 + (routed + shared), lg_ref[...], lb_ref[...])


def _final(ys, idx_t, rank_t, w_t, pstart, x1, wsg, wsu, wsd, lg, lb, alpha, tm):
    t, d = x1.shape
    nchunk = d // LANES
    ds = wsg.shape[1]
    nsteps = t // tm
    full = lambda i, ps: (0, 0)
    grid_spec = pltpu.PrefetchScalarGridSpec(
        num_scalar_prefetch=1,
        grid=(nsteps,),
        in_specs=[pl.BlockSpec(memory_space=pl.ANY)] * 4 + [
            pl.BlockSpec((tm, d), lambda i, ps: (i, 0)),
            pl.BlockSpec((d, ds), full), pl.BlockSpec((d, ds), full), pl.BlockSpec((ds, d), full),
            pl.BlockSpec((1, d), full), pl.BlockSpec((1, d), full)],
        out_specs=pl.BlockSpec((tm, d), lambda i, ps: (i, 0)),
        scratch_shapes=[pltpu.VMEM((2, tm * TOP_K * nchunk, LANES), F32),
                        pltpu.VMEM((tm * nchunk, LANES), F32),
                        pltpu.SMEM((3, TOP_K, tm), jnp.int32),
                        pltpu.SMEM((3, TOP_K, tm), jnp.int32),
                        pltpu.SMEM((3, TOP_K, tm), F32),
                        pltpu.SemaphoreType.DMA((3,)),
                        pltpu.SemaphoreType.DMA((2,))],
    )
    return pl.pallas_call(
        functools.partial(_final_kernel, tm=tm, nchunk=nchunk, alpha=alpha, nsteps=nsteps),
        grid_spec=grid_spec,
        out_shape=jax.ShapeDtypeStruct((t, d), F32),
        compiler_params=pltpu.CompilerParams(
            dimension_semantics=("arbitrary",), vmem_limit_bytes=VMEM_LIMIT),
        name="combine_shared_ln2",
    )(pstart, idx_t, rank_t, w_t, ys, x1, wsg, wsu, wsd, lg, lb)


def _block_tables(counts, n_rows):
    mb = MOE_BLOCK
    n_exp = counts.shape[0]
    padded = (counts + mb - 1) // mb * mb
    pad_end = jnp.cumsum(padded)
    pad_start = pad_end - padded
    n_blocks = n_rows // mb
    block_expert = jnp.minimum(
        jnp.searchsorted(pad_end, jnp.arange(n_blocks, dtype=jnp.int32) * mb, side='right'),
        n_exp - 1).astype(jnp.int32)
    nused = (pad_end[-1:] // mb).astype(jnp.int32)
    return pad_start.astype(jnp.int32), pad_end.astype(jnp.int32), block_expert, nused


def _tile(n, pref):
    while n % pref:
        pref //= 2
    return pref


def kernel(x, w_in, lower_bounds, hg_norm_g, w_branch_hg, w_branch_sb, w_out, ln1_g, ln1_b,
           w_router, router_bias, w_exp_gate, w_exp_up, w_exp_down,
           w_sh_gate, w_sh_up, w_sh_down, ln2_g, ln2_b):
    depth = w_in.shape[0]
    assert depth == 1, "single-layer block only"
    batch, seq, d = x.shape
    t = batch * seq
    n_exp = w_router.shape[-1]
    alpha = (2.0 * depth) ** 0.25
    nchunk = d // LANES

    lb = jnp.cumsum(jax.nn.softmax(lower_bounds.astype(F32), axis=0), axis=0)[0]
    xf = x.reshape(t, d)

    w = w_in[0]
    c = d
    w32 = jnp.concatenate([w[:, 0:2 * c], w[:, 3 * c:4 * c], w[:, 7 * c:9 * c]], axis=1).astype(BF16)
    wbf = jnp.concatenate([w[:, 2 * c:3 * c], w[:, 4 * c:7 * c]], axis=1).astype(BF16)
    tm = _tile(t, 1024)
    p32 = _matmul(xf, w32, F32, tm, 512)
    pbf = _matmul(xf, wbf, BF16, tm, 512)

    o_hg = _hgrn(p32, pbf, lb, hg_norm_g[0].astype(F32), batch, seq, _tile(seq, 256))
    o_sb = _sb_attention(pbf, batch, seq, _tile(seq, 256))

    x1 = _merge(o_hg, o_sb, p32, xf,
                w_branch_hg[0].astype(BF16), w_branch_sb[0].astype(BF16), w_out[0].astype(BF16),
                ln1_g[0].reshape(1, d).astype(F32), ln1_b[0].reshape(1, d).astype(F32),
                alpha, _tile(t, 512))

    idx_t, rank_t, w_t, cnt = _route(x1, w_router[0], router_bias[0], _tile(t, 256))
    p_rows = t * TOP_K + n_exp * MOE_BLOCK
    pstart, pend, block_expert, nused = _block_tables(cnt[:, 0].astype(jnp.int32), p_rows)
    xs = _dispatch(x1.reshape(t * nchunk, LANES), idx_t, rank_t, pstart, pend, p_rows, _tile(t, 128))
    ys = _experts(xs, block_expert, nused, w_exp_gate[0], w_exp_up[0], w_exp_down[0])
    out = _final(ys, idx_t, rank_t, w_t, pstart, x1,
                 w_sh_gate[0].astype(BF16), w_sh_up[0].astype(BF16), w_sh_down[0].astype(BF16),
                 ln2_g[0].reshape(1, d).astype(F32), ln2_b[0].reshape(1, d).astype(F32),
                 alpha, _tile(t, 128))
    return out.reshape(batch, seq, d)
```

```python
import functools

import jax
import jax.numpy as jnp
from jax import lax
from jax.experimental import pallas as pl
from jax.experimental.pallas import tpu as pltpu

F32 = jnp.float32
BF16 = jnp.bfloat16

LANES = 128
SUBLANES = 8
VMEM_LIMIT = 48 * 1024 * 1024

HG_HEADS = 8
HG_CHUNK = 32
SB_HEADS = 8
N_GROUPS = 8
TOPK_GROUPS = 4
TOP_K = 8
ROUTED_SCALE = 2.5
MOE_BLOCK = 128
LN_EPS = 1e-5
RMS_EPS = 1e-6
STICK_DEAD = 110.0

NT_DIMS = (((1,), (1,)), ((), ()))
TN_DIMS = (((0,), (0,)), ((), ()))


def _dot(a, b):
    return jnp.dot(a, b, preferred_element_type=F32)


def _sigmoid(x):
    return 1.0 / (1.0 + jnp.exp(-x))


def _split3(x):
    hi = x.astype(BF16)
    r1 = x - hi.astype(F32)
    mid = r1.astype(BF16)
    lo = (r1 - mid.astype(F32)).astype(BF16)
    return hi, mid, lo


def _layer_norm(r, g, b):
    mu = jnp.mean(r, axis=-1, keepdims=True)
    d = r - mu
    var = jnp.mean(d * d, axis=-1, keepdims=True)
    return d * lax.rsqrt(var + LN_EPS) * g + b


def _mm_kernel(x_ref, w_ref, o_ref):
    o_ref[...] = _dot(x_ref[...].astype(BF16), w_ref[...]).astype(o_ref.dtype)


def _matmul(x, w, out_dtype, tm, tn):
    m, k = x.shape
    n = w.shape[1]
    return pl.pallas_call(
        _mm_kernel,
        grid=(m // tm, n // tn),
        in_specs=[pl.BlockSpec((tm, k), lambda i, j: (i, 0)),
                  pl.BlockSpec((k, tn), lambda i, j: (0, j))],
        out_specs=pl.BlockSpec((tm, tn), lambda i, j: (i, j)),
        out_shape=jax.ShapeDtypeStruct((m, n), out_dtype),
        compiler_params=pltpu.CompilerParams(
            dimension_semantics=("parallel", "arbitrary"), vmem_limit_bytes=VMEM_LIMIT),
        name="in_proj",
    )(x, w)


def _hgrn_kernel(lb_ref, g_ref, q_ref, f_ref, v_ref, og_ref, o_ref, st_ref, *, ts, chunk):
    @pl.when(pl.program_id(2) == 0)
    def _():
        st_ref[...] = jnp.zeros_like(st_ref)

    lb = lb_ref[0]
    hf = f_ref[...]
    log_f = jnp.log(lb + (1.0 - lb) * _sigmoid(hf))
    k_in = (1.0 - lb) * _sigmoid(-hf)

    shift = chunk.bit_length() - 1
    row = lax.broadcasted_iota(jnp.int32, (ts, ts), 0)
    col = lax.broadcasted_iota(jnp.int32, (ts, ts), 1)
    same = (row >> shift) == (col >> shift)
    causal = jnp.logical_and(same, col <= row)
    tri = jnp.where(causal, 1.0, 0.0).astype(BF16)
    ones = jnp.where(same, 1.0, 0.0).astype(BF16)

    parts = jnp.concatenate(_split3(log_f), axis=1)
    cs = _dot(tri, parts)
    tot = _dot(ones, parts)
    d = LANES
    b = cs[:, :d] + cs[:, d:2 * d] + cs[:, 2 * d:]
    b_end = tot[:, :d] + tot[:, d:2 * d] + tot[:, 2 * d:]

    q = q_ref[...]
    q_start = (q * jnp.exp(b)).astype(BF16)
    q_end = (q * jnp.exp(b - b_end)).astype(BF16)
    k_end = (k_in * jnp.exp(b_end - b)).astype(BF16)
    v = v_ref[...]

    scores = lax.dot_general(q_end, k_end, NT_DIMS, preferred_element_type=F32)
    scores = jnp.where(causal, scores, 0.0)
    o_intra = _dot(scores.astype(BF16), v)

    dec = jnp.exp(b_end)
    st = st_ref[...]
    outs = []
    for c in range(ts // chunk):
        lo = c * chunk
        outs.append(lax.dot_general(q_start[lo:lo + chunk], st.astype(BF16), NT_DIMS,
                                    preferred_element_type=F32))
        kv = lax.dot_general(v[lo:lo + chunk], k_end[lo:lo + chunk], TN_DIMS,
                             preferred_element_type=F32)
        st = dec[lo:lo + 1, :] * st + kv
    st_ref[...] = st
    o = o_intra + jnp.concatenate(outs, axis=0)

    o = o * lax.rsqrt(jnp.mean(o * o, axis=-1, keepdims=True) + RMS_EPS)
    o = o * g_ref[0]
    hg = og_ref[...]
    o_ref[...] = (o * (hg * _sigmoid(hg))).astype(o_ref.dtype)


def _hgrn(p32, pbf, lb, g, batch, seq, ts):
    t = batch * seq
    h = HG_HEADS
    ns = seq // ts
    row = lambda off: (lambda b, hh, s: (b * ns + s, off + hh))
    par = lambda b, hh, s: (hh, 0, 0)
    return pl.pallas_call(
        functools.partial(_hgrn_kernel, ts=ts, chunk=HG_CHUNK),
        grid=(batch, h, ns),
        in_specs=[pl.BlockSpec((1, 1, LANES), par),
                  pl.BlockSpec((1, 1, LANES), par),
                  pl.BlockSpec((ts, LANES), row(0)),
                  pl.BlockSpec((ts, LANES), row(h)),
                  pl.BlockSpec((ts, LANES), row(0)),
                  pl.BlockSpec((ts, LANES), row(2 * h))],
        out_specs=pl.BlockSpec((ts, LANES), row(0)),
        out_shape=jax.ShapeDtypeStruct((t, h * LANES), BF16),
        scratch_shapes=[pltpu.VMEM((LANES, LANES), F32)],
        compiler_params=pltpu.CompilerParams(
            dimension_semantics=("parallel", "parallel", "arbitrary"), vmem_limit_bytes=VMEM_LIMIT),
        name="hgrn2",
    )(lb.reshape(h, 1, LANES), g.reshape(h, 1, LANES), p32, p32, pbf, p32)


def _sb_kernel(q_ref, k_ref, v_ref, u_ref, o_ref, *, tq, scale):
    i = pl.program_id(2)
    q = q_ref[...]
    u = u_ref[...]
    rep = tq // LANES

    def block(j, carry, acc, masked, live=None):
        start = pl.multiple_of(j * tq, tq)
        kj = k_ref[pl.ds(start, tq), :]
        vj = v_ref[pl.ds(start, tq), :]
        z = lax.dot_general(q, kj, NT_DIMS, preferred_element_type=F32) * scale
        ls_neg = -(jnp.maximum(z, 0.0) + jnp.log1p(jnp.exp(-jnp.abs(z))))
        if masked:
            row = lax.broadcasted_iota(jnp.int32, (tq, tq), 0)
            col = lax.broadcasted_iota(jnp.int32, (tq, tq), 1)
            before = col < row
            log_keep = jnp.where(before, ls_neg, 0.0)
        elif live is not None:
            log_keep = ls_neg * live
        else:
            log_keep = ls_neg
        hi = log_keep.astype(BF16)
        lo = (log_keep - hi.astype(F32)).astype(BF16)
        r = _dot(hi, u) + _dot(lo, u)
        stick = r[:, :tq] + jnp.concatenate([carry] * rep, axis=1)
        w = jnp.exp(ls_neg + z + stick)
        if masked:
            w = jnp.where(before, w, 0.0)
        elif live is not None:
            w = w * live
        acc = acc + _dot(w.astype(BF16), vj)
        carry = carry + r[:, tq:tq + LANES]
        return carry, acc

    def some_row_alive(carry):
        return (jnp.max(carry) > -STICK_DEAD).astype(jnp.int32)

    zero = jnp.zeros((tq, LANES), F32)
    carry, acc = block(i, zero, zero, True)
    carry, acc = block(jnp.maximum(i - 1, 0), carry, acc, False, live=(i > 0).astype(F32))

    def cond(state):
        return jnp.logical_and(state[0] >= 0, state[1] > 0)

    def body(state):
        j, _, carry, acc = state
        carry, acc = block(j, carry, acc, False)
        return j - 1, some_row_alive(carry), carry, acc

    _, _, carry, acc = lax.while_loop(cond, body, (i - 2, some_row_alive(carry), carry, acc))
    o_ref[...] = acc.astype(o_ref.dtype)


def _sb_attention(pbf, batch, seq, tq):
    t = batch * seq
    h = SB_HEADS
    nq = seq // tq
    r = jnp.arange(tq)
    later = (r[:, None] > r[None, :]).astype(BF16)
    u = jnp.concatenate([later, jnp.ones((tq, tq), BF16)], axis=1)
    return pl.pallas_call(
        functools.partial(_sb_kernel, tq=tq, scale=float(LANES) ** -0.5),
        grid=(batch, h, nq),
        in_specs=[pl.BlockSpec((tq, LANES), lambda b, hh, i: (b * nq + i, h + hh)),
                  pl.BlockSpec((seq, LANES), lambda b, hh, i: (b, 2 * h + hh)),
                  pl.BlockSpec((seq, LANES), lambda b, hh, i: (b, 3 * h + hh)),
                  pl.BlockSpec((tq, 2 * tq), lambda b, hh, i: (0, 0))],
        out_specs=pl.BlockSpec((tq, LANES), lambda b, hh, i: (b * nq + i, hh)),
        out_shape=jax.ShapeDtypeStruct((t, h * LANES), BF16),
        compiler_params=pltpu.CompilerParams(
            dimension_semantics=("parallel", "parallel", "arbitrary"), vmem_limit_bytes=VMEM_LIMIT),
        name="stick_breaking",
    )(pbf, pbf, pbf, u)


def _merge_kernel(ohg_ref, osb_ref, ghg_ref, gsb_ref, x_ref, wbh_ref, wbs_ref, wo_ref,
                  lg_ref, lb_ref, x1_ref, *, alpha):
    y_hg = _dot(ohg_ref[...], wbh_ref[...])
    y_sb = _dot(osb_ref[...], wbs_ref[...])
    merged = _sigmoid(ghg_ref[...]) * y_hg + _sigmoid(gsb_ref[...]) * y_sb
    hmix = _dot(merged.astype(BF16), wo_ref[...])
    x1_ref[...] = _layer_norm(alpha * x_ref[...] + hmix, lg_ref[...], lb_ref[...])


def _merge(o_hg, o_sb, p32, x, wbh, wbs, wo, lg, lb, alpha, tm):
    t, d = x.shape
    row = lambda off: (lambda i: (i, off))
    full = lambda i: (0, 0)
    return pl.pallas_call(
        functools.partial(_merge_kernel, alpha=alpha),
        grid=(t // tm,),
        in_specs=[pl.BlockSpec((tm, d), row(0)), pl.BlockSpec((tm, d), row(0)),
                  pl.BlockSpec((tm, d), row(3)), pl.BlockSpec((tm, d), row(4)),
                  pl.BlockSpec((tm, d), row(0)),
                  pl.BlockSpec((d, d), full), pl.BlockSpec((d, d), full), pl.BlockSpec((d, d), full),
                  pl.BlockSpec((1, d), full), pl.BlockSpec((1, d), full)],
        out_specs=pl.BlockSpec((tm, d), row(0)),
        out_shape=jax.ShapeDtypeStruct((t, d), F32),
        compiler_params=pltpu.CompilerParams(
            dimension_semantics=("parallel",), vmem_limit_bytes=VMEM_LIMIT),
        name="merge_ln1",
    )(o_hg, o_sb, p32, p32, x, wbh, wbs, wo, lg, lb)


def _route_kernel(x_ref, wh_ref, wl_ref, bias_ref, u_ref, ones_ref,
                  idx_ref, rank_ref, w_ref, cnt_ref, carry, *, tile, n_exp):
    @pl.when(pl.program_id(0) == 0)
    def _():
        carry[...] = jnp.zeros_like(carry)

    x = x_ref[...]
    xh = x.astype(BF16)
    xl = (x - xh.astype(F32)).astype(BF16)
    wh = wh_ref[...]
    nt = functools.partial(lax.dot_general, dimension_numbers=NT_DIMS, preferred_element_type=F32)
    logits = nt(wh, xh) + nt(wh, xl) + nt(wl_ref[...], xh)
    scores = _sigmoid(logits)
    biased = scores + bias_ref[...]

    neg_inf = -jnp.inf
    gsz = n_exp // N_GROUPS
    groups = [biased[g * gsz:(g + 1) * gsz] for g in range(N_GROUPS)]
    gscore = []
    for xg in groups:
        m1 = jnp.max(xg, axis=0, keepdims=True)
        n1 = jnp.sum(jnp.where(xg == m1, 1.0, 0.0), axis=0, keepdims=True)
        m2 = jnp.max(jnp.where(xg < m1, xg, neg_inf), axis=0, keepdims=True)
        gscore.append(m1 + jnp.where(n1 >= 2.0, m1, m2))
    masked = []
    for g in range(N_GROUPS):
        beaten_by = jnp.zeros_like(gscore[g])
        for h in range(N_GROUPS):
            if h < g:
                beaten_by = beaten_by + jnp.where(gscore[h] >= gscore[g], 1.0, 0.0)
            elif h > g:
                beaten_by = beaten_by + jnp.where(gscore[h] > gscore[g], 1.0, 0.0)
        keep = jnp.broadcast_to(beaten_by, (gsz, tile)) < float(TOPK_GROUPS)
        masked.append(jnp.where(keep, groups[g], jnp.finfo(F32).min))
    cur = jnp.concatenate(masked, axis=0)

    eid = lax.broadcasted_iota(jnp.int32, (n_exp, tile), 0).astype(F32)
    chosen = jnp.zeros((n_exp, tile), F32)
    idxs, ws = [], []
    for _ in range(TOP_K):
        m = jnp.max(cur, axis=0, keepdims=True)
        ik = jnp.min(jnp.where(cur == m, eid, float(n_exp)), axis=0, keepdims=True)
        sel = eid == ik
        idxs.append(ik)
        ws.append(jnp.sum(jnp.where(sel, scores, 0.0), axis=0, keepdims=True))
        chosen = chosen + jnp.where(sel, 1.0, 0.0)
        cur = jnp.where(sel, neg_inf, cur)

    chosen_b = chosen.astype(BF16)
    run = carry[...]
    rank = _dot(chosen_b, u_ref[...]) + jnp.concatenate([run] * (tile // LANES), axis=1)
    ranks = [jnp.sum(jnp.where(eid == ik, rank, 0.0), axis=0, keepdims=True) for ik in idxs]
    run = run + _dot(chosen_b, ones_ref[...])
    carry[...] = run
    cnt_ref[...] = run

    wsum = ws[0]
    for wk in ws[1:]:
        wsum = wsum + wk
    idx_ref[...] = jnp.concatenate(idxs, axis=0).astype(jnp.int32)
    rank_ref[...] = jnp.concatenate(ranks, axis=0).astype(jnp.int32)
    w_ref[...] = jnp.concatenate([wk / wsum * ROUTED_SCALE for wk in ws], axis=0)


def _route(x1, w_router, router_bias, tile):
    t, d = x1.shape
    n_exp = w_router.shape[1]
    wt = w_router.astype(F32).T
    wh = wt.astype(BF16)
    wl = (wt - wh.astype(F32)).astype(BF16)
    r = jnp.arange(tile)
    u = (r[:, None] < r[None, :]).astype(BF16)
    ones = jnp.ones((tile, LANES), BF16)
    full = lambda i: (0, 0)
    tok = lambda i: (0, i)
    return pl.pallas_call(
        functools.partial(_route_kernel, tile=tile, n_exp=n_exp),
        grid=(t // tile,),
        in_specs=[pl.BlockSpec((tile, d), lambda i: (i, 0)),
                  pl.BlockSpec((n_exp, d), full), pl.BlockSpec((n_exp, d), full),
                  pl.BlockSpec((n_exp, 1), full),
                  pl.BlockSpec((tile, tile), full), pl.BlockSpec((tile, LANES), full)],
        out_specs=[pl.BlockSpec((TOP_K, tile), tok), pl.BlockSpec((TOP_K, tile), tok),
                   pl.BlockSpec((TOP_K, tile), tok), pl.BlockSpec((n_exp, LANES), full)],
        out_shape=[jax.ShapeDtypeStruct((TOP_K, t), jnp.int32), jax.ShapeDtypeStruct((TOP_K, t), jnp.int32),
                   jax.ShapeDtypeStruct((TOP_K, t), F32), jax.ShapeDtypeStruct((n_exp, LANES), F32)],
        scratch_shapes=[pltpu.VMEM((n_exp, LANES), F32)],
        compiler_params=pltpu.CompilerParams(
            dimension_semantics=("arbitrary",), vmem_limit_bytes=VMEM_LIMIT),
        name="router_topk",
    )(x1, wh, wl, router_bias.astype(F32).reshape(n_exp, 1), u, ones)


def _dispatch_kernel(pstart_ref, pend_ref, idx_hbm, rank_hbm, x_hbm, xs_hbm,
                     idx_s, rank_s, xbuf, zbuf, isem, lsem, dsem, zsem,
                     *, td, nchunk, mb, n_exp, nsteps, n_blocks):
    i = pl.program_id(0)
    rows = mb * nchunk
    trows = td * nchunk

    def idx_copies(step):
        slot = step % 2
        cols = pl.ds(step * td, td)
        return (pltpu.make_async_copy(idx_hbm.at[:, cols], idx_s.at[slot], isem.at[slot]),
                pltpu.make_async_copy(rank_hbm.at[:, cols], rank_s.at[slot], isem.at[slot]))

    def tile_load(step):
        slot = step % 3
        start = pl.multiple_of(step * trows, trows)
        return pltpu.make_async_copy(x_hbm.at[pl.ds(start, trows), :], xbuf.at[slot], lsem.at[slot])

    def wait_rows(step):
        slot = step % 3
        for _ in range(TOP_K):
            pltpu.make_async_copy(xbuf.at[slot], xbuf.at[slot], dsem.at[slot]).wait()

    @pl.when(i == 0)
    def _():
        for cp in idx_copies(0):
            cp.start()
        tile_load(0).start()
        zbuf[...] = jnp.zeros_like(zbuf)

        def zero_copy(e):
            start = pl.multiple_of((pend_ref[e] - mb) * nchunk, nchunk)
            return pltpu.make_async_copy(zbuf, xs_hbm.at[pl.ds(start, rows), :], zsem)

        def zstart(e, _):
            @pl.when(pend_ref[e] > pstart_ref[e])
            def _():
                zero_copy(e).start()
            return 0

        def zwait(e, _):
            @pl.when(pend_ref[e] > pstart_ref[e])
            def _():
                zero_copy(e).wait()
            return 0

        lax.fori_loop(0, n_exp, zstart, 0)
        lax.fori_loop(0, n_exp, zwait, 0)

        def tail_copy(b):
            start = pl.multiple_of(b * rows, rows)
            return pltpu.make_async_copy(zbuf, xs_hbm.at[pl.ds(start, rows), :], zsem)

        def tstart(b, _):
            tail_copy(b).start()
            return 0

        def twait(b, _):
            tail_copy(b).wait()
            return 0

        first_unused = pend_ref[n_exp - 1] // mb
        lax.fori_loop(first_unused, n_blocks, tstart, 0)
        lax.fori_loop(first_unused, n_blocks, twait, 0)

    for cp in idx_copies(i):
        cp.wait()

    @pl.when(i >= 2)
    def _():
        wait_rows(i - 2)

    @pl.when(i + 1 < nsteps)
    def _():
        for cp in idx_copies(i + 1):
            cp.start()
        tile_load(i + 1).start()

    tile_load(i).wait()
    slot = i % 2
    xslot = i % 3

    def body(r, _):
        src = pl.multiple_of(r * nchunk, nchunk)
        for k in range(TOP_K):
            e = idx_s[slot, k, r]
            dst = pl.multiple_of((pstart_ref[e] + rank_s[slot, k, r]) * nchunk, nchunk)
            pltpu.make_async_copy(xbuf.at[xslot, pl.ds(src, nchunk), :], xs_hbm.at[pl.ds(dst, nchunk), :],
                                  dsem.at[xslot]).start()
        return 0

    lax.fori_loop(0, td, body, 0, unroll=2)

    @pl.when(i == nsteps - 1)
    def _():
        @pl.when(i >= 1)
        def _():
            wait_rows(i - 1)

        wait_rows(i)


def _dispatch(x2, idx_t, rank_t, pstart, pend, p_rows, td):
    n_exp = pstart.shape[0]
    t = idx_t.shape[1]
    nchunk = x2.shape[0] // t
    nsteps = t // td
    grid_spec = pltpu.PrefetchScalarGridSpec(
        num_scalar_prefetch=2,
        grid=(nsteps,),
        in_specs=[pl.BlockSpec(memory_space=pl.ANY)] * 3,
        out_specs=pl.BlockSpec(memory_space=pl.ANY),
        scratch_shapes=[pltpu.SMEM((2, TOP_K, td), jnp.int32),
                        pltpu.SMEM((2, TOP_K, td), jnp.int32),
                        pltpu.VMEM((3, td * nchunk, LANES), F32),
                        pltpu.VMEM((MOE_BLOCK * nchunk, LANES), F32),
                        pltpu.SemaphoreType.DMA((2,)),
                        pltpu.SemaphoreType.DMA((3,)),
                        pltpu.SemaphoreType.DMA((3,)),
                        pltpu.SemaphoreType.DMA],
    )
    return pl.pallas_call(
        functools.partial(_dispatch_kernel, td=td, nchunk=nchunk, mb=MOE_BLOCK, n_exp=n_exp, nsteps=nsteps,
                          n_blocks=p_rows // MOE_BLOCK),
        grid_spec=grid_spec,
        out_shape=jax.ShapeDtypeStruct((p_rows * nchunk, LANES), F32),
        compiler_params=pltpu.CompilerParams(
            dimension_semantics=("arbitrary",), vmem_limit_bytes=VMEM_LIMIT),
        name="moe_dispatch",
    )(pstart, pend, idx_t, rank_t, x2)


def _experts_kernel(bexp_ref, nused_ref, xs_ref, wg_ref, wu_ref, wd_ref, ys_ref, wg_b, wu_b, wd_b,
                    *, mb, nchunk):
    i = pl.program_id(0)

    @pl.when(i < nused_ref[0])
    def _():
        new_expert = jnp.logical_or(i == 0, bexp_ref[i] != bexp_ref[jnp.maximum(i - 1, 0)])

        @pl.when(new_expert)
        def _():
            wg_b[...] = wg_ref[0].astype(BF16)
            wu_b[...] = wu_ref[0].astype(BF16)
            wd_b[...] = wd_ref[0].astype(BF16)

        xs = [xs_ref[pl.ds(c, mb, stride=nchunk), :] for c in range(nchunk)]
        xb = jnp.concatenate(xs, axis=1).astype(BF16)
        gate = _dot(xb, wg_b[...])
        up = _dot(xb, wu_b[...])
        hid = (gate * _sigmoid(gate)) * up
        y = _dot(hid.astype(BF16), wd_b[...])
        for c in range(nchunk):
            ys_ref[pl.ds(c, mb, stride=nchunk), :] = y[:, c * LANES:(c + 1) * LANES]

    @pl.when(i >= nused_ref[0])
    def _():
        ys_ref[...] = jnp.zeros_like(ys_ref)


def _experts(xs, block_expert, nused, w_gate, w_up, w_down):
    e, d, de = w_gate.shape
    nchunk = d // LANES
    mb = MOE_BLOCK
    rows = mb * nchunk
    n_blocks = xs.shape[0] // rows
    blk = lambda i, be, nu: (jnp.minimum(i, nu[0] - 1), 0)
    wsel = lambda i, be, nu: (be[i], 0, 0)
    grid_spec = pltpu.PrefetchScalarGridSpec(
        num_scalar_prefetch=2,
        grid=(n_blocks,),
        in_specs=[pl.BlockSpec((rows, LANES), blk),
                  pl.BlockSpec((1, d, de), wsel), pl.BlockSpec((1, d, de), wsel),
                  pl.BlockSpec((1, de, d), wsel)],
        out_specs=pl.BlockSpec((rows, LANES), lambda i, be, nu: (i, 0)),
        scratch_shapes=[pltpu.VMEM((d, de), BF16), pltpu.VMEM((d, de), BF16), pltpu.VMEM((de, d), BF16)],
    )
    return pl.pallas_call(
        functools.partial(_experts_kernel, mb=mb, nchunk=nchunk),
        grid_spec=grid_spec,
        out_shape=jax.ShapeDtypeStruct(xs.shape, F32),
        compiler_params=pltpu.CompilerParams(
            dimension_semantics=("arbitrary",), vmem_limit_bytes=VMEM_LIMIT),
        name="routed_experts",
    )(block_expert, nused, xs, w_gate, w_up, w_down)


def _final_kernel(pstart_ref, idx_hbm, rank_hbm, w_hbm, ys_hbm, x1_ref, wsg_ref, wsu_ref, wsd_ref,
                  lg_ref, lb_ref, o_ref, ybuf, rbuf, idx_s, rank_s, w_s, isem, gsem,
                  *, tm, nchunk, alpha, nsteps):
    i = pl.program_id(0)

    def idx_copies(step):
        slot = step % 3
        cols = pl.ds(step * tm, tm)
        return (pltpu.make_async_copy(idx_hbm.at[:, cols], idx_s.at[slot], isem.at[slot]),
                pltpu.make_async_copy(rank_hbm.at[:, cols], rank_s.at[slot], isem.at[slot]),
                pltpu.make_async_copy(w_hbm.at[:, cols], w_s.at[slot], isem.at[slot]))

    def start_gather(step):
        islot = step % 3
        slot = step % 2

        def body(r, _):
            for k in range(TOP_K):
                e = idx_s[islot, k, r]
                src = pl.multiple_of((pstart_ref[e] + rank_s[islot, k, r]) * nchunk, nchunk)
                pltpu.make_async_copy(ys_hbm.at[pl.ds(src, nchunk), :],
                                      ybuf.at[slot, pl.ds((r * TOP_K + k) * nchunk, nchunk), :],
                                      gsem.at[slot]).start()
            return 0

        lax.fori_loop(0, tm, body, 0, unroll=2)

    @pl.when(i == 0)
    def _():
        for cp in idx_copies(0):
            cp.start()
        for cp in idx_copies(0):
            cp.wait()
        start_gather(0)

        @pl.when(nsteps > 1)
        def _():
            for cp in idx_copies(1):
                cp.start()

    @pl.when(i + 1 < nsteps)
    def _():
        for cp in idx_copies(i + 1):
            cp.wait()
        start_gather(i + 1)

    @pl.when(i + 2 < nsteps)
    def _():
        for cp in idx_copies(i + 2):
            cp.start()

    slot = i % 2
    islot = i % 3
    pltpu.make_async_copy(ybuf.at[slot], ybuf.at[slot], gsem.at[slot]).wait()

    def combine(r, _):
        acc = jnp.zeros((nchunk, LANES), F32)
        for k in range(TOP_K):
            acc = acc + ybuf[slot, pl.ds((r * TOP_K + k) * nchunk, nchunk), :] * w_s[islot, k, r]
        rbuf[pl.ds(r * nchunk, nchunk), :] = acc
        return 0

    lax.fori_loop(0, tm, combine, 0, unroll=2)
    routed = jnp.concatenate([rbuf[pl.ds(c, tm, stride=nchunk), :] for c in range(nchunk)], axis=1)

    x1 = x1_ref[...]
    xb = x1.astype(BF16)
    gate = _dot(xb, wsg_ref[...])
    up = _dot(xb, wsu_ref[...])
    shared = _dot(((gate * _sigmoid(gate)) * up).astype(BF16), wsd_ref[...])
    o_ref[...] = _layer_norm(alpha * x1 + (routed + shared), lg_ref[...], lb_ref[...])


def _final(ys, idx_t, rank_t, w_t, pstart, x1, wsg, wsu, wsd, lg, lb, alpha, tm):
    t, d = x1.shape
    nchunk = d // LANES
    ds = wsg.shape[1]
    nsteps = t // tm
    full = lambda i, ps: (0, 0)
    grid_spec = pltpu.PrefetchScalarGridSpec(
        num_scalar_prefetch=1,
        grid=(nsteps,),
        in_specs=[pl.BlockSpec(memory_space=pl.ANY)] * 4 + [
            pl.BlockSpec((tm, d), lambda i, ps: (i, 0)),
            pl.BlockSpec((d, ds), full), pl.BlockSpec((d, ds), full), pl.BlockSpec((ds, d), full),
            pl.BlockSpec((1, d), full), pl.BlockSpec((1, d), full)],
        out_specs=pl.BlockSpec((tm, d), lambda i, ps: (i, 0)),
        scratch_shapes=[pltpu.VMEM((2, tm * TOP_K * nchunk, LANES), F32),
                        pltpu.VMEM((tm * nchunk, LANES), F32),
                        pltpu.SMEM((3, TOP_K, tm), jnp.int32),
                        pltpu.SMEM((3, TOP_K, tm), jnp.int32),
                        pltpu.SMEM((3, TOP_K, tm), F32),
                        pltpu.SemaphoreType.DMA((3,)),
                        pltpu.SemaphoreType.DMA((2,))],
    )
    return pl.pallas_call(
        functools.partial(_final_kernel, tm=tm, nchunk=nchunk, alpha=alpha, nsteps=nsteps),
        grid_spec=grid_spec,
        out_shape=jax.ShapeDtypeStruct((t, d), F32),
        compiler_params=pltpu.CompilerParams(
            dimension_semantics=("arbitrary",), vmem_limit_bytes=VMEM_LIMIT),
        name="combine_shared_ln2",
    )(pstart, idx_t, rank_t, w_t, ys, x1, wsg, wsu, wsd, lg, lb)


def _block_tables(counts, n_rows):
    mb = MOE_BLOCK
    n_exp = counts.shape[0]
    padded = (counts + mb - 1) // mb * mb
    pad_end = jnp.cumsum(padded)
    pad_start = pad_end - padded
    n_blocks = n_rows // mb
    block_expert = jnp.minimum(
        jnp.searchsorted(pad_end, jnp.arange(n_blocks, dtype=jnp.int32) * mb, side='right'),
        n_exp - 1).astype(jnp.int32)
    nused = (pad_end[-1:] // mb).astype(jnp.int32)
    return pad_start.astype(jnp.int32), pad_end.astype(jnp.int32), block_expert, nused


def _tile(n, pref):
    while n % pref:
        pref //= 2
    return pref


def kernel(x, w_in, lower_bounds, hg_norm_g, w_branch_hg, w_branch_sb, w_out, ln1_g, ln1_b,
           w_router, router_bias, w_exp_gate, w_exp_up, w_exp_down,
           w_sh_gate, w_sh_up, w_sh_down, ln2_g, ln2_b):
    depth = w_in.shape[0]
    assert depth == 1, "single-layer block only"
    batch, seq, d = x.shape
    t = batch * seq
    n_exp = w_router.shape[-1]
    alpha = (2.0 * depth) ** 0.25
    nchunk = d // LANES

    lb = jnp.cumsum(jax.nn.softmax(lower_bounds.astype(F32), axis=0), axis=0)[0]
    xf = x.reshape(t, d)

    w = w_in[0]
    c = d
    w32 = jnp.concatenate([w[:, 0:2 * c], w[:, 3 * c:4 * c], w[:, 7 * c:9 * c]], axis=1).astype(BF16)
    wbf = jnp.concatenate([w[:, 2 * c:3 * c], w[:, 4 * c:7 * c]], axis=1).astype(BF16)
    tm = _tile(t, 1024)
    p32 = _matmul(xf, w32, F32, tm, 512)
    pbf = _matmul(xf, wbf, BF16, tm, 512)

    o_hg = _hgrn(p32, pbf, lb, hg_norm_g[0].astype(F32), batch, seq, _tile(seq, 256))
    o_sb = _sb_attention(pbf, batch, seq, _tile(seq, 256))

    x1 = _merge(o_hg, o_sb, p32, xf,
                w_branch_hg[0].astype(BF16), w_branch_sb[0].astype(BF16), w_out[0].astype(BF16),
                ln1_g[0].reshape(1, d).astype(F32), ln1_b[0].reshape(1, d).astype(F32),
                alpha, _tile(t, 512))

    idx_t, rank_t, w_t, cnt = _route(x1, w_router[0], router_bias[0], _tile(t, 256))
    p_rows = t * TOP_K + n_exp * MOE_BLOCK
    pstart, pend, block_expert, nused = _block_tables(cnt[:, 0].astype(jnp.int32), p_rows)
    xs = _dispatch(x1.reshape(t * nchunk, LANES), idx_t, rank_t, pstart, pend, p_rows, _tile(t, 128))
    ys = _experts(xs, block_expert, nused, w_exp_gate[0], w_exp_up[0], w_exp_down[0])
    out = _final(ys, idx_t, rank_t, w_t, pstart, x1,
                 w_sh_gate[0].astype(BF16), w_sh_up[0].astype(BF16), w_sh_down[0].astype(BF16),
                 ln2_g[0].reshape(1, d).astype(F32), ln2_b[0].reshape(1, d).astype(F32),
                 alpha, _tile(t, 128))
    return out.reshape(batch, seq, d)
```

```python
import functools

import jax
import jax.numpy as jnp
from jax import lax
from jax.experimental import pallas as pl
from jax.experimental.pallas import tpu as pltpu

F32 = jnp.float32
BF16 = jnp.bfloat16

LANES = 128
SUBLANES = 8
VMEM_LIMIT = 48 * 1024 * 1024

HG_HEADS = 8
HG_CHUNK = 32
SB_HEADS = 8
N_GROUPS = 8
TOPK_GROUPS = 4
TOP_K = 8
ROUTED_SCALE = 2.5
MOE_BLOCK = 256
LN_EPS = 1e-5
RMS_EPS = 1e-6
STICK_DEAD = 110.0

NT_DIMS = (((1,), (1,)), ((), ()))
TN_DIMS = (((0,), (0,)), ((), ()))


def _dot(a, b):
    return jnp.dot(a, b, preferred_element_type=F32)


def _sigmoid(x):
    return 1.0 / (1.0 + jnp.exp(-x))


def _split3(x):
    hi = x.astype(BF16)
    r1 = x - hi.astype(F32)
    mid = r1.astype(BF16)
    lo = (r1 - mid.astype(F32)).astype(BF16)
    return hi, mid, lo


def _layer_norm(r, g, b):
    mu = jnp.mean(r, axis=-1, keepdims=True)
    d = r - mu
    var = jnp.mean(d * d, axis=-1, keepdims=True)
    return d * lax.rsqrt(var + LN_EPS) * g + b


def _mm_kernel(x_ref, w_ref, o_ref):
    o_ref[...] = _dot(x_ref[...].astype(BF16), w_ref[...]).astype(o_ref.dtype)


def _matmul(x, w, out_dtype, tm, tn):
    m, k = x.shape
    n = w.shape[1]
    return pl.pallas_call(
        _mm_kernel,
        grid=(m // tm, n // tn),
        in_specs=[pl.BlockSpec((tm, k), lambda i, j: (i, 0)),
                  pl.BlockSpec((k, tn), lambda i, j: (0, j))],
        out_specs=pl.BlockSpec((tm, tn), lambda i, j: (i, j)),
        out_shape=jax.ShapeDtypeStruct((m, n), out_dtype),
        compiler_params=pltpu.CompilerParams(
            dimension_semantics=("parallel", "arbitrary"), vmem_limit_bytes=VMEM_LIMIT),
        name="in_proj",
    )(x, w)


def _hgrn_kernel(lb_ref, g_ref, q_ref, f_ref, v_ref, og_ref, o_ref, st_ref, *, ts, chunk, hp):
    @pl.when(pl.program_id(2) == 0)
    def _():
        st_ref[...] = jnp.zeros_like(st_ref)

    shift = chunk.bit_length() - 1
    row = lax.broadcasted_iota(jnp.int32, (ts, ts), 0)
    col = lax.broadcasted_iota(jnp.int32, (ts, ts), 1)
    same = (row >> shift) == (col >> shift)
    causal = jnp.logical_and(same, col <= row)
    tri = jnp.where(causal, 1.0, 0.0).astype(BF16)
    ones = jnp.where(same, 1.0, 0.0).astype(BF16)
    d = LANES

    for hh in range(hp):
        cols = slice(hh * d, (hh + 1) * d)
        lb = lb_ref[0, :, cols]
        hf = f_ref[:, cols]
        log_f = jnp.log(lb + (1.0 - lb) * _sigmoid(hf))
        k_in = (1.0 - lb) * _sigmoid(-hf)

        parts = jnp.concatenate(_split3(log_f), axis=1)
        cs = _dot(tri, parts)
        tot = _dot(ones, parts)
        b = cs[:, :d] + cs[:, d:2 * d] + cs[:, 2 * d:]
        b_end = tot[:, :d] + tot[:, d:2 * d] + tot[:, 2 * d:]

        q = q_ref[:, cols]
        q_start = (q * jnp.exp(b)).astype(BF16)
        q_end = (q * jnp.exp(b - b_end)).astype(BF16)
        k_end = (k_in * jnp.exp(b_end - b)).astype(BF16)
        v = v_ref[:, cols]

        scores = lax.dot_general(q_end, k_end, NT_DIMS, preferred_element_type=F32)
        scores = jnp.where(causal, scores, 0.0)
        o_intra = _dot(scores.astype(BF16), v)

        dec = jnp.exp(b_end)
        st = st_ref[hh]
        outs = []
        for c in range(ts // chunk):
            lo = c * chunk
            outs.append(lax.dot_general(q_start[lo:lo + chunk], st.astype(BF16), NT_DIMS,
                                        preferred_element_type=F32))
            kv = lax.dot_general(v[lo:lo + chunk], k_end[lo:lo + chunk], TN_DIMS,
                                 preferred_element_type=F32)
            st = dec[lo:lo + 1, :] * st + kv
        st_ref[hh] = st
        o = o_intra + jnp.concatenate(outs, axis=0)

        o = o * lax.rsqrt(jnp.mean(o * o, axis=-1, keepdims=True) + RMS_EPS)
        o = o * g_ref[0, :, cols]
        hg = og_ref[:, cols]
        o_ref[:, cols] = (o * (hg * _sigmoid(hg))).astype(o_ref.dtype)


def _hgrn(p32, pbf, lb, g, batch, seq, ts, hp):
    t = batch * seq
    h = HG_HEADS
    ng = h // hp
    w = hp * LANES
    ns = seq // ts
    row = lambda off: (lambda b, hg, s: (b * ns + s, off + hg))
    par = lambda b, hg, s: (hg, 0, 0)
    return pl.pallas_call(
        functools.partial(_hgrn_kernel, ts=ts, chunk=HG_CHUNK, hp=hp),
        grid=(batch, ng, ns),
        in_specs=[pl.BlockSpec((1, 1, w), par),
                  pl.BlockSpec((1, 1, w), par),
                  pl.BlockSpec((ts, w), row(0)),
                  pl.BlockSpec((ts, w), row(ng)),
                  pl.BlockSpec((ts, w), row(0)),
                  pl.BlockSpec((ts, w), row(2 * ng))],
        out_specs=pl.BlockSpec((ts, w), row(0)),
        out_shape=jax.ShapeDtypeStruct((t, h * LANES), BF16),
        scratch_shapes=[pltpu.VMEM((hp, LANES, LANES), F32)],
        compiler_params=pltpu.CompilerParams(
            dimension_semantics=("parallel", "parallel", "arbitrary"), vmem_limit_bytes=VMEM_LIMIT),
        name="hgrn2",
    )(lb.reshape(ng, 1, w), g.reshape(ng, 1, w), p32, p32, pbf, p32)


def _sb_kernel(q_ref, k_ref, v_ref, u_ref, o_ref, *, tq, scale):
    i = pl.program_id(2)
    q = q_ref[...]
    u = u_ref[...]
    rep = tq // LANES

    def block(j, carry, acc, masked, live=None):
        start = pl.multiple_of(j * tq, tq)
        kj = k_ref[pl.ds(start, tq), :]
        vj = v_ref[pl.ds(start, tq), :]
        z = lax.dot_general(q, kj, NT_DIMS, preferred_element_type=F32) * scale
        ls_neg = -(jnp.maximum(z, 0.0) + jnp.log1p(jnp.exp(-jnp.abs(z))))
        if masked:
            row = lax.broadcasted_iota(jnp.int32, (tq, tq), 0)
            col = lax.broadcasted_iota(jnp.int32, (tq, tq), 1)
            before = col < row
            log_keep = jnp.where(before, ls_neg, 0.0)
        elif live is not None:
            log_keep = ls_neg * live
        else:
            log_keep = ls_neg
        hi = log_keep.astype(BF16)
        lo = (log_keep - hi.astype(F32)).astype(BF16)
        r = _dot(hi, u) + _dot(lo, u)
        stick = r[:, :tq] + jnp.concatenate([carry] * rep, axis=1)
        w = jnp.exp(ls_neg + z + stick)
        if masked:
            w = jnp.where(before, w, 0.0)
        elif live is not None:
            w = w * live
        acc = acc + _dot(w.astype(BF16), vj)
        carry = carry + r[:, tq:tq + LANES]
        return carry, acc

    def some_row_alive(carry):
        return (jnp.max(carry) > -STICK_DEAD).astype(jnp.int32)

    zero = jnp.zeros((tq, LANES), F32)
    carry, acc = block(i, zero, zero, True)
    carry, acc = block(jnp.maximum(i - 1, 0), carry, acc, False, live=(i > 0).astype(F32))

    def cond(state):
        return jnp.logical_and(state[0] >= 0, state[1] > 0)

    def body(state):
        j, _, carry, acc = state
        carry, acc = block(j, carry, acc, False)
        return j - 1, some_row_alive(carry), carry, acc

    _, _, carry, acc = lax.while_loop(cond, body, (i - 2, some_row_alive(carry), carry, acc))
    o_ref[...] = acc.astype(o_ref.dtype)


def _sb_attention(pbf, batch, seq, tq):
    t = batch * seq
    h = SB_HEADS
    nq = seq // tq
    r = jnp.arange(tq)
    later = (r[:, None] > r[None, :]).astype(BF16)
    u = jnp.concatenate([later, jnp.ones((tq, tq), BF16)], axis=1)
    return pl.pallas_call(
        functools.partial(_sb_kernel, tq=tq, scale=float(LANES) ** -0.5),
        grid=(batch, h, nq),
        in_specs=[pl.BlockSpec((tq, LANES), lambda b, hh, i: (b * nq + i, h + hh)),
                  pl.BlockSpec((seq, LANES), lambda b, hh, i: (b, 2 * h + hh)),
                  pl.BlockSpec((seq, LANES), lambda b, hh, i: (b, 3 * h + hh)),
                  pl.BlockSpec((tq, 2 * tq), lambda b, hh, i: (0, 0))],
        out_specs=pl.BlockSpec((tq, LANES), lambda b, hh, i: (b * nq + i, hh)),
        out_shape=jax.ShapeDtypeStruct((t, h * LANES), BF16),
        compiler_params=pltpu.CompilerParams(
            dimension_semantics=("parallel", "parallel", "arbitrary"), vmem_limit_bytes=VMEM_LIMIT),
        name="stick_breaking",
    )(pbf, pbf, pbf, u)


def _merge_kernel(ohg_ref, osb_ref, ghg_ref, gsb_ref, x_ref, wbh_ref, wbs_ref, wo_ref,
                  lg_ref, lb_ref, x1_ref, x1s_ref, *, alpha, tm, nchunk):
    y_hg = _dot(ohg_ref[...], wbh_ref[...])
    y_sb = _dot(osb_ref[...], wbs_ref[...])
    merged = _sigmoid(ghg_ref[...]) * y_hg + _sigmoid(gsb_ref[...]) * y_sb
    hmix = _dot(merged.astype(BF16), wo_ref[...])
    x1 = _layer_norm(alpha * x_ref[...] + hmix, lg_ref[...], lb_ref[...])
    x1_ref[...] = x1
    for c in range(nchunk):
        x1s_ref[pl.ds(c, tm, stride=nchunk), :] = x1[:, c * LANES:(c + 1) * LANES]


def _merge(o_hg, o_sb, p32, x, wbh, wbs, wo, lg, lb, alpha, tm):
    t, d = x.shape
    nchunk = d // LANES
    row = lambda off: (lambda i: (i, off))
    full = lambda i: (0, 0)
    return pl.pallas_call(
        functools.partial(_merge_kernel, alpha=alpha, tm=tm, nchunk=nchunk),
        grid=(t // tm,),
        in_specs=[pl.BlockSpec((tm, d), row(0)), pl.BlockSpec((tm, d), row(0)),
                  pl.BlockSpec((tm, d), row(3)), pl.BlockSpec((tm, d), row(4)),
                  pl.BlockSpec((tm, d), row(0)),
                  pl.BlockSpec((d, d), full), pl.BlockSpec((d, d), full), pl.BlockSpec((d, d), full),
                  pl.BlockSpec((1, d), full), pl.BlockSpec((1, d), full)],
        out_specs=[pl.BlockSpec((tm, d), row(0)), pl.BlockSpec((tm * nchunk, LANES), row(0))],
        out_shape=[jax.ShapeDtypeStruct((t, d), F32), jax.ShapeDtypeStruct((t * nchunk, LANES), F32)],
        compiler_params=pltpu.CompilerParams(
            dimension_semantics=("parallel",), vmem_limit_bytes=VMEM_LIMIT),
        name="merge_ln1",
    )(o_hg, o_sb, p32, p32, x, wbh, wbs, wo, lg, lb)


def _route_kernel(x_ref, wh_ref, wl_ref, bias_ref, u_ref, ones_ref,
                  idx_ref, rank_ref, w_ref, cnt_ref, carry, *, tile, n_exp):
    @pl.when(pl.program_id(0) == 0)
    def _():
        carry[...] = jnp.zeros_like(carry)

    x = x_ref[...]
    xh = x.astype(BF16)
    xl = (x - xh.astype(F32)).astype(BF16)
    wh = wh_ref[...]
    nt = functools.partial(lax.dot_general, dimension_numbers=NT_DIMS, preferred_element_type=F32)
    logits = nt(wh, xh) + nt(wh, xl) + nt(wl_ref[...], xh)
    scores = _sigmoid(logits)
    biased = scores + bias_ref[...]

    neg_inf = -jnp.inf
    gsz = n_exp // N_GROUPS
    groups = [biased[g * gsz:(g + 1) * gsz] for g in range(N_GROUPS)]
    gscore = []
    for xg in groups:
        m1 = jnp.max(xg, axis=0, keepdims=True)
        n1 = jnp.sum(jnp.where(xg == m1, 1.0, 0.0), axis=0, keepdims=True)
        m2 = jnp.max(jnp.where(xg < m1, xg, neg_inf), axis=0, keepdims=True)
        gscore.append(m1 + jnp.where(n1 >= 2.0, m1, m2))
    masked = []
    for g in range(N_GROUPS):
        beaten_by = jnp.zeros_like(gscore[g])
        for h in range(N_GROUPS):
            if h < g:
                beaten_by = beaten_by + jnp.where(gscore[h] >= gscore[g], 1.0, 0.0)
            elif h > g:
                beaten_by = beaten_by + jnp.where(gscore[h] > gscore[g], 1.0, 0.0)
        keep = jnp.broadcast_to(beaten_by, (gsz, tile)) < float(TOPK_GROUPS)
        masked.append(jnp.where(keep, groups[g], jnp.finfo(F32).min))
    cur = jnp.concatenate(masked, axis=0)

    eid = lax.broadcasted_iota(jnp.int32, (n_exp, tile), 0).astype(F32)
    chosen = jnp.zeros((n_exp, tile), F32)
    idxs, ws = [], []
    for _ in range(TOP_K):
        m = jnp.max(cur, axis=0, keepdims=True)
        ik = jnp.min(jnp.where(cur == m, eid, float(n_exp)), axis=0, keepdims=True)
        sel = eid == ik
        idxs.append(ik)
        ws.append(jnp.sum(jnp.where(sel, scores, 0.0), axis=0, keepdims=True))
        chosen = chosen + jnp.where(sel, 1.0, 0.0)
        cur = jnp.where(sel, neg_inf, cur)

    chosen_b = chosen.astype(BF16)
    run = carry[...]
    rank = _dot(chosen_b, u_ref[...]) + jnp.concatenate([run] * (tile // LANES), axis=1)
    ranks = [jnp.sum(jnp.where(eid == ik, rank, 0.0), axis=0, keepdims=True) for ik in idxs]
    run = run + _dot(chosen_b, ones_ref[...])
    carry[...] = run
    cnt_ref[...] = run

    wsum = ws[0]
    for wk in ws[1:]:
        wsum = wsum + wk
    idx_ref[...] = jnp.concatenate(idxs, axis=0).astype(jnp.int32)
    rank_ref[...] = jnp.concatenate(ranks, axis=0).astype(jnp.int32)
    w_ref[...] = jnp.concatenate([wk / wsum * ROUTED_SCALE for wk in ws], axis=0)


def _route(x1, w_router, router_bias, tile):
    t, d = x1.shape
    n_exp = w_router.shape[1]
    wt = w_router.astype(F32).T
    wh = wt.astype(BF16)
    wl = (wt - wh.astype(F32)).astype(BF16)
    r = jnp.arange(tile)
    u = (r[:, None] < r[None, :]).astype(BF16)
    ones = jnp.ones((tile, LANES), BF16)
    full = lambda i: (0, 0)
    tok = lambda i: (0, i)
    return pl.pallas_call(
        functools.partial(_route_kernel, tile=tile, n_exp=n_exp),
        grid=(t // tile,),
        in_specs=[pl.BlockSpec((tile, d), lambda i: (i, 0)),
                  pl.BlockSpec((n_exp, d), full), pl.BlockSpec((n_exp, d), full),
                  pl.BlockSpec((n_exp, 1), full),
                  pl.BlockSpec((tile, tile), full), pl.BlockSpec((tile, LANES), full)],
        out_specs=[pl.BlockSpec((TOP_K, tile), tok), pl.BlockSpec((TOP_K, tile), tok),
                   pl.BlockSpec((TOP_K, tile), tok), pl.BlockSpec((n_exp, LANES), full)],
        out_shape=[jax.ShapeDtypeStruct((TOP_K, t), jnp.int32), jax.ShapeDtypeStruct((TOP_K, t), jnp.int32),
                   jax.ShapeDtypeStruct((TOP_K, t), F32), jax.ShapeDtypeStruct((n_exp, LANES), F32)],
        scratch_shapes=[pltpu.VMEM((n_exp, LANES), F32)],
        compiler_params=pltpu.CompilerParams(
            dimension_semantics=("arbitrary",), vmem_limit_bytes=VMEM_LIMIT),
        name="router_topk",
    )(x1, wh, wl, router_bias.astype(F32).reshape(n_exp, 1), u, ones)


def _dispatch_kernel(pstart_ref, pend_ref, idx_hbm, rank_hbm, x_hbm, xs_hbm,
                     idx_s, rank_s, xbuf, zbuf, isem, lsem, dsem, zsem,
                     *, td, nchunk, mb, n_exp, nsteps, n_blocks):
    i = pl.program_id(0)
    rows = mb * nchunk
    trows = td * nchunk

    def idx_copies(step):
        slot = step % 2
        cols = pl.ds(step * td, td)
        return (pltpu.make_async_copy(idx_hbm.at[:, cols], idx_s.at[slot], isem.at[slot]),
                pltpu.make_async_copy(rank_hbm.at[:, cols], rank_s.at[slot], isem.at[slot]))

    def tile_load(step):
        slot = step % 3
        start = pl.multiple_of(step * trows, trows)
        return pltpu.make_async_copy(x_hbm.at[pl.ds(start, trows), :], xbuf.at[slot], lsem.at[slot])

    def wait_rows(step):
        slot = step % 3
        for _ in range(TOP_K):
            pltpu.make_async_copy(xbuf.at[slot], xbuf.at[slot], dsem.at[slot]).wait()

    @pl.when(i == 0)
    def _():
        for cp in idx_copies(0):
            cp.start()
        tile_load(0).start()
        zbuf[...] = jnp.zeros_like(zbuf)

        def zero_copy(e):
            start = pl.multiple_of((pend_ref[e] - mb) * nchunk, nchunk)
            return pltpu.make_async_copy(zbuf, xs_hbm.at[pl.ds(start, rows), :], zsem)

        def zstart(e, _):
            @pl.when(pend_ref[e] > pstart_ref[e])
            def _():
                zero_copy(e).start()
            return 0

        def zwait(e, _):
            @pl.when(pend_ref[e] > pstart_ref[e])
            def _():
                zero_copy(e).wait()
            return 0

        lax.fori_loop(0, n_exp, zstart, 0)
        lax.fori_loop(0, n_exp, zwait, 0)

        def tail_copy(b):
            start = pl.multiple_of(b * rows, rows)
            return pltpu.make_async_copy(zbuf, xs_hbm.at[pl.ds(start, rows), :], zsem)

        def tstart(b, _):
            tail_copy(b).start()
            return 0

        def twait(b, _):
            tail_copy(b).wait()
            return 0

        first_unused = pend_ref[n_exp - 1] // mb
        lax.fori_loop(first_unused, n_blocks, tstart, 0)
        lax.fori_loop(first_unused, n_blocks, twait, 0)

    for cp in idx_copies(i):
        cp.wait()

    @pl.when(i >= 2)
    def _():
        wait_rows(i - 2)

    @pl.when(i + 1 < nsteps)
    def _():
        for cp in idx_copies(i + 1):
            cp.start()
        tile_load(i + 1).start()

    tile_load(i).wait()
    slot = i % 2
    xslot = i % 3

    def body(r, _):
        src = pl.multiple_of(r * nchunk, nchunk)
        for k in range(TOP_K):
            e = idx_s[slot, k, r]
            dst = pl.multiple_of((pstart_ref[e] + rank_s[slot, k, r]) * nchunk, nchunk)
            pltpu.make_async_copy(xbuf.at[xslot, pl.ds(src, nchunk), :], xs_hbm.at[pl.ds(dst, nchunk), :],
                                  dsem.at[xslot]).start()
        return 0

    lax.fori_loop(0, td, body, 0, unroll=2)

    @pl.when(i == nsteps - 1)
    def _():
        @pl.when(i >= 1)
        def _():
            wait_rows(i - 1)

        wait_rows(i)


def _dispatch(x2, idx_t, rank_t, pstart, pend, p_rows, td):
    n_exp = pstart.shape[0]
    t = idx_t.shape[1]
    nchunk = x2.shape[0] // t
    nsteps = t // td
    grid_spec = pltpu.PrefetchScalarGridSpec(
        num_scalar_prefetch=2,
        grid=(nsteps,),
        in_specs=[pl.BlockSpec(memory_space=pl.ANY)] * 3,
        out_specs=pl.BlockSpec(memory_space=pl.ANY),
        scratch_shapes=[pltpu.SMEM((2, TOP_K, td), jnp.int32),
                        pltpu.SMEM((2, TOP_K, td), jnp.int32),
                        pltpu.VMEM((3, td * nchunk, LANES), F32),
                        pltpu.VMEM((MOE_BLOCK * nchunk, LANES), F32),
                        pltpu.SemaphoreType.DMA((2,)),
                        pltpu.SemaphoreType.DMA((3,)),
                        pltpu.SemaphoreType.DMA((3,)),
                        pltpu.SemaphoreType.DMA],
    )
    return pl.pallas_call(
        functools.partial(_dispatch_kernel, td=td, nchunk=nchunk, mb=MOE_BLOCK, n_exp=n_exp, nsteps=nsteps,
                          n_blocks=p_rows // MOE_BLOCK),
        grid_spec=grid_spec,
        out_shape=jax.ShapeDtypeStruct((p_rows * nchunk, LANES), F32),
        compiler_params=pltpu.CompilerParams(
            dimension_semantics=("arbitrary",), vmem_limit_bytes=VMEM_LIMIT),
        name="moe_dispatch",
    )(pstart, pend, idx_t, rank_t, x2)


def _experts_kernel(bexp_ref, nused_ref, xs_ref, wg_ref, wu_ref, wd_ref, ys_ref, wg_b, wu_b, wd_b,
                    *, mb, nchunk):
    i = pl.program_id(0)

    @pl.when(i < nused_ref[0])
    def _():
        new_expert = jnp.logical_or(i == 0, bexp_ref[i] != bexp_ref[jnp.maximum(i - 1, 0)])

        @pl.when(new_expert)
        def _():
            wg_b[...] = wg_ref[0].astype(BF16)
            wu_b[...] = wu_ref[0].astype(BF16)
            wd_b[...] = wd_ref[0].astype(BF16)

        xs = [xs_ref[pl.ds(c, mb, stride=nchunk), :] for c in range(nchunk)]
        xb = jnp.concatenate(xs, axis=1).astype(BF16)
        gate = _dot(xb, wg_b[...])
        up = _dot(xb, wu_b[...])
        hid = (gate * _sigmoid(gate)) * up
        y = _dot(hid.astype(BF16), wd_b[...])
        for c in range(nchunk):
            ys_ref[pl.ds(c, mb, stride=nchunk), :] = y[:, c * LANES:(c + 1) * LANES]

    @pl.when(i >= nused_ref[0])
    def _():
        ys_ref[...] = jnp.zeros_like(ys_ref)


def _experts(xs, block_expert, nused, w_gate, w_up, w_down):
    e, d, de = w_gate.shape
    nchunk = d // LANES
    mb = MOE_BLOCK
    rows = mb * nchunk
    n_blocks = xs.shape[0] // rows
    blk = lambda i, be, nu: (jnp.minimum(i, nu[0] - 1), 0)
    wsel = lambda i, be, nu: (be[i], 0, 0)
    grid_spec = pltpu.PrefetchScalarGridSpec(
        num_scalar_prefetch=2,
        grid=(n_blocks,),
        in_specs=[pl.BlockSpec((rows, LANES), blk),
                  pl.BlockSpec((1, d, de), wsel), pl.BlockSpec((1, d, de), wsel),
                  pl.BlockSpec((1, de, d), wsel)],
        out_specs=pl.BlockSpec((rows, LANES), lambda i, be, nu: (i, 0)),
        scratch_shapes=[pltpu.VMEM((d, de), BF16), pltpu.VMEM((d, de), BF16), pltpu.VMEM((de, d), BF16)],
    )
    return pl.pallas_call(
        functools.partial(_experts_kernel, mb=mb, nchunk=nchunk),
        grid_spec=grid_spec,
        out_shape=jax.ShapeDtypeStruct(xs.shape, F32),
        compiler_params=pltpu.CompilerParams(
            dimension_semantics=("arbitrary",), vmem_limit_bytes=VMEM_LIMIT),
        name="routed_experts",
    )(block_expert, nused, xs, w_gate, w_up, w_down)


def _final_kernel(pstart_ref, idx_hbm, rank_hbm, w_hbm, ys_hbm, x1_ref, wsg_ref, wsu_ref, wsd_ref,
                  lg_ref, lb_ref, o_ref, ybuf, rbuf, idx_s, rank_s, w_s, isem, gsem,
                  *, tm, nchunk, alpha, nsteps):
    i = pl.program_id(0)

    def idx_copies(step):
        slot = step % 3
        cols = pl.ds(step * tm, tm)
        return (pltpu.make_async_copy(idx_hbm.at[:, cols], idx_s.at[slot], isem.at[slot]),
                pltpu.make_async_copy(rank_hbm.at[:, cols], rank_s.at[slot], isem.at[slot]),
                pltpu.make_async_copy(w_hbm.at[:, cols], w_s.at[slot], isem.at[slot]))

    def start_gather(step):
        islot = step % 3
        slot = step % 2

        def body(r, _):
            for k in range(TOP_K):
                e = idx_s[islot, k, r]
                src = pl.multiple_of((pstart_ref[e] + rank_s[islot, k, r]) * nchunk, nchunk)
                pltpu.make_async_copy(ys_hbm.at[pl.ds(src, nchunk), :],
                                      ybuf.at[slot, pl.ds((r * TOP_K + k) * nchunk, nchunk), :],
                                      gsem.at[slot]).start()
            return 0

        lax.fori_loop(0, tm, body, 0, unroll=2)

    @pl.when(i == 0)
    def _():
        for cp in idx_copies(0):
            cp.start()
        for cp in idx_copies(0):
            cp.wait()
        start_gather(0)

        @pl.when(nsteps > 1)
        def _():
            for cp in idx_copies(1):
                cp.start()

    @pl.when(i + 1 < nsteps)
    def _():
        for cp in idx_copies(i + 1):
            cp.wait()
        start_gather(i + 1)

    @pl.when(i + 2 < nsteps)
    def _():
        for cp in idx_copies(i + 2):
            cp.start()

    slot = i % 2
    islot = i % 3
    pltpu.make_async_copy(ybuf.at[slot], ybuf.at[slot], gsem.at[slot]).wait()

    def combine(r, _):
        acc = jnp.zeros((nchunk, LANES), F32)
        for k in range(TOP_K):
            acc = acc + ybuf[slot, pl.ds((r * TOP_K + k) * nchunk, nchunk), :] * w_s[islot, k, r]
        rbuf[pl.ds(r * nchunk, nchunk), :] = acc
        return 0

    lax.fori_loop(0, tm, combine, 0, unroll=2)
    routed = jnp.concatenate([rbuf[pl.ds(c, tm, stride=nchunk), :] for c in range(nchunk)], axis=1)

    x1 = x1_ref[...]
    xb = x1.astype(BF16)
    gate = _dot(xb, wsg_ref[...])
    up = _dot(xb, wsu_ref[...])
    shared = _dot(((gate * _sigmoid(gate)) * up).astype(BF16), wsd_ref[...])
    o_ref[...] = _layer_norm(alpha * x1 + (routed + shared), lg_ref[...], lb_ref[...])


def _final(ys, idx_t, rank_t, w_t, pstart, x1, wsg, wsu, wsd, lg, lb, alpha, tm):
    t, d = x1.shape
    nchunk = d // LANES
    ds = wsg.shape[1]
    nsteps = t // tm
    full = lambda i, ps: (0, 0)
    grid_spec = pltpu.PrefetchScalarGridSpec(
        num_scalar_prefetch=1,
        grid=(nsteps,),
        in_specs=[pl.BlockSpec(memory_space=pl.ANY)] * 4 + [
            pl.BlockSpec((tm, d), lambda i, ps: (i, 0)),
            pl.BlockSpec((d, ds), full), pl.BlockSpec((d, ds), full), pl.BlockSpec((ds, d), full),
            pl.BlockSpec((1, d), full), pl.BlockSpec((1, d), full)],
        out_specs=pl.BlockSpec((tm, d), lambda i, ps: (i, 0)),
        scratch_shapes=[pltpu.VMEM((2, tm * TOP_K * nchunk, LANES), F32),
                        pltpu.VMEM((tm * nchunk, LANES), F32),
                        pltpu.SMEM((3, TOP_K, tm), jnp.int32),
                        pltpu.SMEM((3, TOP_K, tm), jnp.int32),
                        pltpu.SMEM((3, TOP_K, tm), F32),
                        pltpu.SemaphoreType.DMA((3,)),
                        pltpu.SemaphoreType.DMA((2,))],
    )
    return pl.pallas_call(
        functools.partial(_final_kernel, tm=tm, nchunk=nchunk, alpha=alpha, nsteps=nsteps),
        grid_spec=grid_spec,
        out_shape=jax.ShapeDtypeStruct((t, d), F32),
        compiler_params=pltpu.CompilerParams(
            dimension_semantics=("arbitrary",), vmem_limit_bytes=VMEM_LIMIT),
        name="combine_shared_ln2",
    )(pstart, idx_t, rank_t, w_t, ys, x1, wsg, wsu, wsd, lg, lb)


def _block_tables(counts, n_rows):
    mb = MOE_BLOCK
    n_exp = counts.shape[0]
    padded = (counts + mb - 1) // mb * mb
    pad_end = jnp.cumsum(padded)
    pad_start = pad_end - padded
    n_blocks = n_rows // mb
    first_row = jnp.arange(n_blocks, dtype=jnp.int32) * mb
    block_expert = jnp.minimum(
        jnp.sum((pad_end[None, :] <= first_row[:, None]).astype(jnp.int32), axis=1), n_exp - 1)
    nused = (pad_end[-1:] // mb).astype(jnp.int32)
    return pad_start.astype(jnp.int32), pad_end.astype(jnp.int32), block_expert, nused


def _tile(n, pref):
    while n % pref:
        pref //= 2
    return pref


def kernel(x, w_in, lower_bounds, hg_norm_g, w_branch_hg, w_branch_sb, w_out, ln1_g, ln1_b,
           w_router, router_bias, w_exp_gate, w_exp_up, w_exp_down,
           w_sh_gate, w_sh_up, w_sh_down, ln2_g, ln2_b):
    depth = w_in.shape[0]
    assert depth == 1, "single-layer block only"
    batch, seq, d = x.shape
    t = batch * seq
    n_exp = w_router.shape[-1]
    alpha = (2.0 * depth) ** 0.25
    nchunk = d // LANES

    lb = jnp.cumsum(jax.nn.softmax(lower_bounds.astype(F32), axis=0), axis=0)[0]
    xf = x.reshape(t, d)

    w = w_in[0]
    c = d
    w32 = jnp.concatenate([w[:, 0:2 * c], w[:, 3 * c:4 * c], w[:, 7 * c:9 * c]], axis=1).astype(BF16)
    wbf = jnp.concatenate([w[:, 2 * c:3 * c], w[:, 4 * c:7 * c]], axis=1).astype(BF16)
    tm = _tile(t, 1024)
    p32 = _matmul(xf, w32, F32, tm, 512)
    pbf = _matmul(xf, wbf, BF16, tm, 512)

    o_hg = _hgrn(p32, pbf, lb, hg_norm_g[0].astype(F32), batch, seq, _tile(seq, 256), 4)
    o_sb = _sb_attention(pbf, batch, seq, _tile(seq, 256))

    x1, x1s = _merge(o_hg, o_sb, p32, xf,
                w_branch_hg[0].astype(BF16), w_branch_sb[0].astype(BF16), w_out[0].astype(BF16),
                ln1_g[0].reshape(1, d).astype(F32), ln1_b[0].reshape(1, d).astype(F32),
                alpha, _tile(t, 512))

    idx_t, rank_t, w_t, cnt = _route(x1, w_router[0], router_bias[0], _tile(t, 256))
    p_rows = t * TOP_K + n_exp * MOE_BLOCK
    pstart, pend, block_expert, nused = _block_tables(cnt[:, 0].astype(jnp.int32), p_rows)
    xs = _dispatch(x1s, idx_t, rank_t, pstart, pend, p_rows, _tile(t, 128))
    ys = _experts(xs, block_expert, nused, w_exp_gate[0], w_exp_up[0], w_exp_down[0])
    out = _final(ys, idx_t, rank_t, w_t, pstart, x1,
                 w_sh_gate[0].astype(BF16), w_sh_up[0].astype(BF16), w_sh_down[0].astype(BF16),
                 ln2_g[0].reshape(1, d).astype(F32), ln2_b[0].reshape(1, d).astype(F32),
                 alpha, _tile(t, 128))
    return out.reshape(batch, seq, d)
```

```python
import functools

import jax
import jax.numpy as jnp
from jax import lax
from jax.experimental import pallas as pl
from jax.experimental.pallas import tpu as pltpu

F32 = jnp.float32
BF16 = jnp.bfloat16

LANES = 128
SUBLANES = 8
VMEM_LIMIT = 48 * 1024 * 1024

HG_HEADS = 8
HG_CHUNK = 32
SB_HEADS = 8
N_GROUPS = 8
TOPK_GROUPS = 4
TOP_K = 8
ROUTED_SCALE = 2.5
MOE_BLOCK = 256
LN_EPS = 1e-5
RMS_EPS = 1e-6
STICK_DEAD = 110.0

NT_DIMS = (((1,), (1,)), ((), ()))
TN_DIMS = (((0,), (0,)), ((), ()))


def _dot(a, b):
    return jnp.dot(a, b, preferred_element_type=F32)


def _sigmoid(x):
    return 1.0 / (1.0 + jnp.exp(-x))


def _split3(x):
    hi = x.astype(BF16)
    r1 = x - hi.astype(F32)
    mid = r1.astype(BF16)
    lo = (r1 - mid.astype(F32)).astype(BF16)
    return hi, mid, lo


def _layer_norm(r, g, b):
    mu = jnp.mean(r, axis=-1, keepdims=True)
    d = r - mu
    var = jnp.mean(d * d, axis=-1, keepdims=True)
    return d * lax.rsqrt(var + LN_EPS) * g + b


def _mm_kernel(x_ref, w_ref, o_ref):
    o_ref[...] = _dot(x_ref[...].astype(BF16), w_ref[...]).astype(o_ref.dtype)


def _matmul(x, w, out_dtype, tm, tn):
    m, k = x.shape
    n = w.shape[1]
    return pl.pallas_call(
        _mm_kernel,
        grid=(m // tm, n // tn),
        in_specs=[pl.BlockSpec((tm, k), lambda i, j: (i, 0)),
                  pl.BlockSpec((k, tn), lambda i, j: (0, j))],
        out_specs=pl.BlockSpec((tm, tn), lambda i, j: (i, j)),
        out_shape=jax.ShapeDtypeStruct((m, n), out_dtype),
        compiler_params=pltpu.CompilerParams(
            dimension_semantics=("parallel", "arbitrary"), vmem_limit_bytes=VMEM_LIMIT),
        name="in_proj",
    )(x, w)


def _hgrn_kernel(lb_ref, g_ref, q_ref, f_ref, v_ref, og_ref, o_ref, st_ref, *, ts, chunk, hp):
    @pl.when(pl.program_id(2) == 0)
    def _():
        st_ref[...] = jnp.zeros_like(st_ref)

    shift = chunk.bit_length() - 1
    row = lax.broadcasted_iota(jnp.int32, (ts, ts), 0)
    col = lax.broadcasted_iota(jnp.int32, (ts, ts), 1)
    same = (row >> shift) == (col >> shift)
    causal = jnp.logical_and(same, col <= row)
    tri = jnp.where(causal, 1.0, 0.0).astype(BF16)
    ones = jnp.where(same, 1.0, 0.0).astype(BF16)
    d = LANES

    for hh in range(hp):
        cols = slice(hh * d, (hh + 1) * d)
        lb = lb_ref[0, :, cols]
        hf = f_ref[:, cols]
        log_f = jnp.log(lb + (1.0 - lb) * _sigmoid(hf))
        k_in = (1.0 - lb) * _sigmoid(-hf)

        parts = jnp.concatenate(_split3(log_f), axis=1)
        cs = _dot(tri, parts)
        tot = _dot(ones, parts)
        b = cs[:, :d] + cs[:, d:2 * d] + cs[:, 2 * d:]
        b_end = tot[:, :d] + tot[:, d:2 * d] + tot[:, 2 * d:]

        q = q_ref[:, cols]
        q_start = (q * jnp.exp(b)).astype(BF16)
        q_end = (q * jnp.exp(b - b_end)).astype(BF16)
        k_end = (k_in * jnp.exp(b_end - b)).astype(BF16)
        v = v_ref[:, cols]

        scores = lax.dot_general(q_end, k_end, NT_DIMS, preferred_element_type=F32)
        scores = jnp.where(causal, scores, 0.0)
        o_intra = _dot(scores.astype(BF16), v)

        dec = jnp.exp(b_end)
        st = st_ref[hh]
        outs = []
        for c in range(ts // chunk):
            lo = c * chunk
            outs.append(lax.dot_general(q_start[lo:lo + chunk], st.astype(BF16), NT_DIMS,
                                        preferred_element_type=F32))
            kv = lax.dot_general(v[lo:lo + chunk], k_end[lo:lo + chunk], TN_DIMS,
                                 preferred_element_type=F32)
            st = dec[lo:lo + 1, :] * st + kv
        st_ref[hh] = st
        o = o_intra + jnp.concatenate(outs, axis=0)

        o = o * lax.rsqrt(jnp.mean(o * o, axis=-1, keepdims=True) + RMS_EPS)
        o = o * g_ref[0, :, cols]
        hg = og_ref[:, cols]
        o_ref[:, cols] = (o * (hg * _sigmoid(hg))).astype(o_ref.dtype)


def _hgrn(p32, pbf, lb, g, batch, seq, ts, hp):
    t = batch * seq
    h = HG_HEADS
    ng = h // hp
    w = hp * LANES
    ns = seq // ts
    row = lambda off: (lambda b, hg, s: (b * ns + s, off + hg))
    par = lambda b, hg, s: (hg, 0, 0)
    return pl.pallas_call(
        functools.partial(_hgrn_kernel, ts=ts, chunk=HG_CHUNK, hp=hp),
        grid=(batch, ng, ns),
        in_specs=[pl.BlockSpec((1, 1, w), par),
                  pl.BlockSpec((1, 1, w), par),
                  pl.BlockSpec((ts, w), row(0)),
                  pl.BlockSpec((ts, w), row(ng)),
                  pl.BlockSpec((ts, w), row(0)),
                  pl.BlockSpec((ts, w), row(2 * ng))],
        out_specs=pl.BlockSpec((ts, w), row(0)),
        out_shape=jax.ShapeDtypeStruct((t, h * LANES), BF16),
        scratch_shapes=[pltpu.VMEM((hp, LANES, LANES), F32)],
        compiler_params=pltpu.CompilerParams(
            dimension_semantics=("parallel", "parallel", "arbitrary"), vmem_limit_bytes=VMEM_LIMIT),
        name="hgrn2",
    )(lb.reshape(ng, 1, w), g.reshape(ng, 1, w), p32, p32, pbf, p32)


def _sb_kernel(q_ref, k_ref, v_ref, u_ref, o_ref, *, tq, scale, hp):
    i = pl.program_id(2)
    u = u_ref[...]
    rep = tq // LANES
    d = LANES

    def block(hh, j, carry, acc, masked, live=None):
        cols = slice(hh * d, (hh + 1) * d)
        start = pl.multiple_of(j * tq, tq)
        q = q_ref[:, cols]
        kj = k_ref[pl.ds(start, tq), cols]
        vj = v_ref[pl.ds(start, tq), cols]
        z = lax.dot_general(q, kj, NT_DIMS, preferred_element_type=F32) * scale
        sp = jnp.maximum(z, 0.0) + jnp.log(1.0 + jnp.exp(-jnp.abs(z)))
        if masked:
            row = lax.broadcasted_iota(jnp.int32, (tq, tq), 0)
            col = lax.broadcasted_iota(jnp.int32, (tq, tq), 1)
            before = col < row
            drop = jnp.where(before, sp, 0.0)
        elif live is not None:
            drop = sp * live
        else:
            drop = sp
        hi = drop.astype(BF16)
        lo = (drop - hi.astype(F32)).astype(BF16)
        later = _dot(hi, u) + _dot(lo, u)
        stick = later + jnp.concatenate([carry] * rep, axis=1)
        w = jnp.exp((z - sp) - stick)
        if masked:
            w = jnp.where(before, w, 0.0)
        elif live is not None:
            w = w * live
        acc = acc + _dot(w.astype(BF16), vj)
        carry = carry + jnp.broadcast_to(later[:, 0:1] + drop[:, 0:1], (tq, LANES))
        return carry, acc

    def some_row_alive(carries):
        m = jnp.min(carries[0])
        for c in carries[1:]:
            m = jnp.minimum(m, jnp.min(c))
        return (m < STICK_DEAD).astype(jnp.int32)

    zero = jnp.zeros((tq, LANES), F32)
    live = (i > 0).astype(F32)
    prev = jnp.maximum(i - 1, 0)
    carries, accs = [], []
    for hh in range(hp):
        carry, acc = block(hh, i, zero, zero, True)
        carry, acc = block(hh, prev, carry, acc, False, live=live)
        carries.append(carry)
        accs.append(acc)

    def cond(state):
        return jnp.logical_and(state[0] >= 0, state[1] > 0)

    def body(state):
        j, _, carries, accs = state
        out = [block(hh, j, carries[hh], accs[hh], False) for hh in range(hp)]
        carries = tuple(o[0] for o in out)
        accs = tuple(o[1] for o in out)
        return j - 1, some_row_alive(carries), carries, accs

    _, _, carries, accs = lax.while_loop(
        cond, body, (i - 2, some_row_alive(carries), tuple(carries), tuple(accs)))
    for hh in range(hp):
        o_ref[:, hh * d:(hh + 1) * d] = accs[hh].astype(o_ref.dtype)


def _sb_attention(pbf, batch, seq, tq, hp):
    t = batch * seq
    h = SB_HEADS
    ng = h // hp
    w = hp * LANES
    nq = seq // tq
    r = jnp.arange(tq)
    u = (r[:, None] > r[None, :]).astype(BF16)
    return pl.pallas_call(
        functools.partial(_sb_kernel, tq=tq, scale=float(LANES) ** -0.5, hp=hp),
        grid=(batch, ng, nq),
        in_specs=[pl.BlockSpec((tq, w), lambda b, hg, i: (b * nq + i, ng + hg)),
                  pl.BlockSpec((seq, w), lambda b, hg, i: (b, 2 * ng + hg)),
                  pl.BlockSpec((seq, w), lambda b, hg, i: (b, 3 * ng + hg)),
                  pl.BlockSpec((tq, tq), lambda b, hg, i: (0, 0))],
        out_specs=pl.BlockSpec((tq, w), lambda b, hg, i: (b * nq + i, hg)),
        out_shape=jax.ShapeDtypeStruct((t, h * LANES), BF16),
        compiler_params=pltpu.CompilerParams(
            dimension_semantics=("parallel", "parallel", "arbitrary"), vmem_limit_bytes=VMEM_LIMIT),
        name="stick_breaking",
    )(pbf, pbf, pbf, u)


def _merge_kernel(ohg_ref, osb_ref, ghg_ref, gsb_ref, x_ref, wbh_ref, wbs_ref, wo_ref,
                  lg_ref, lb_ref, x1_ref, x1s_ref, *, alpha, tm, nchunk):
    y_hg = _dot(ohg_ref[...], wbh_ref[...])
    y_sb = _dot(osb_ref[...], wbs_ref[...])
    merged = _sigmoid(ghg_ref[...]) * y_hg + _sigmoid(gsb_ref[...]) * y_sb
    hmix = _dot(merged.astype(BF16), wo_ref[...])
    x1 = _layer_norm(alpha * x_ref[...] + hmix, lg_ref[...], lb_ref[...])
    x1_ref[...] = x1
    for c in range(nchunk):
        x1s_ref[pl.ds(c, tm, stride=nchunk), :] = x1[:, c * LANES:(c + 1) * LANES]


def _merge(o_hg, o_sb, p32, x, wbh, wbs, wo, lg, lb, alpha, tm):
    t, d = x.shape
    nchunk = d // LANES
    row = lambda off: (lambda i: (i, off))
    full = lambda i: (0, 0)
    return pl.pallas_call(
        functools.partial(_merge_kernel, alpha=alpha, tm=tm, nchunk=nchunk),
        grid=(t // tm,),
        in_specs=[pl.BlockSpec((tm, d), row(0)), pl.BlockSpec((tm, d), row(0)),
                  pl.BlockSpec((tm, d), row(3)), pl.BlockSpec((tm, d), row(4)),
                  pl.BlockSpec((tm, d), row(0)),
                  pl.BlockSpec((d, d), full), pl.BlockSpec((d, d), full), pl.BlockSpec((d, d), full),
                  pl.BlockSpec((1, d), full), pl.BlockSpec((1, d), full)],
        out_specs=[pl.BlockSpec((tm, d), row(0)), pl.BlockSpec((tm * nchunk, LANES), row(0))],
        out_shape=[jax.ShapeDtypeStruct((t, d), F32), jax.ShapeDtypeStruct((t * nchunk, LANES), F32)],
        compiler_params=pltpu.CompilerParams(
            dimension_semantics=("parallel",), vmem_limit_bytes=VMEM_LIMIT),
        name="merge_ln1",
    )(o_hg, o_sb, p32, p32, x, wbh, wbs, wo, lg, lb)


def _route_kernel(x_ref, wh_ref, wl_ref, bias_ref, u_ref, ones_ref,
                  idx_ref, rank_ref, w_ref, cnt_ref, carry, *, tile, n_exp):
    @pl.when(pl.program_id(0) == 0)
    def _():
        carry[...] = jnp.zeros_like(carry)

    x = x_ref[...]
    xh = x.astype(BF16)
    xl = (x - xh.astype(F32)).astype(BF16)
    wh = wh_ref[...]
    nt = functools.partial(lax.dot_general, dimension_numbers=NT_DIMS, preferred_element_type=F32)
    logits = nt(wh, xh) + nt(wh, xl) + nt(wl_ref[...], xh)
    scores = _sigmoid(logits)
    biased = scores + bias_ref[...]

    neg_inf = -jnp.inf
    gsz = n_exp // N_GROUPS
    groups = [biased[g * gsz:(g + 1) * gsz] for g in range(N_GROUPS)]
    gscore = []
    for xg in groups:
        m1 = jnp.max(xg, axis=0, keepdims=True)
        n1 = jnp.sum(jnp.where(xg == m1, 1.0, 0.0), axis=0, keepdims=True)
        m2 = jnp.max(jnp.where(xg < m1, xg, neg_inf), axis=0, keepdims=True)
        gscore.append(m1 + jnp.where(n1 >= 2.0, m1, m2))
    masked = []
    for g in range(N_GROUPS):
        beaten_by = jnp.zeros_like(gscore[g])
        for h in range(N_GROUPS):
            if h < g:
                beaten_by = beaten_by + jnp.where(gscore[h] >= gscore[g], 1.0, 0.0)
            elif h > g:
                beaten_by = beaten_by + jnp.where(gscore[h] > gscore[g], 1.0, 0.0)
        keep = jnp.broadcast_to(beaten_by, (gsz, tile)) < float(TOPK_GROUPS)
        masked.append(jnp.where(keep, groups[g], jnp.finfo(F32).min))
    cur = jnp.concatenate(masked, axis=0)

    eid = lax.broadcasted_iota(jnp.int32, (n_exp, tile), 0).astype(F32)
    chosen = jnp.zeros((n_exp, tile), F32)
    idxs, ws = [], []
    for _ in range(TOP_K):
        m = jnp.max(cur, axis=0, keepdims=True)
        ik = jnp.min(jnp.where(cur == m, eid, float(n_exp)), axis=0, keepdims=True)
        sel = eid == ik
        idxs.append(ik)
        ws.append(jnp.sum(jnp.where(sel, scores, 0.0), axis=0, keepdims=True))
        chosen = chosen + jnp.where(sel, 1.0, 0.0)
        cur = jnp.where(sel, neg_inf, cur)

    chosen_b = chosen.astype(BF16)
    run = carry[...]
    rank = _dot(chosen_b, u_ref[...]) + jnp.concatenate([run] * (tile // LANES), axis=1)
    ranks = [jnp.sum(jnp.where(eid == ik, rank, 0.0), axis=0, keepdims=True) for ik in idxs]
    run = run + _dot(chosen_b, ones_ref[...])
    carry[...] = run
    cnt_ref[...] = run

    wsum = ws[0]
    for wk in ws[1:]:
        wsum = wsum + wk
    idx_ref[...] = jnp.concatenate(idxs, axis=0).astype(jnp.int32)
    rank_ref[...] = jnp.concatenate(ranks, axis=0).astype(jnp.int32)
    w_ref[...] = jnp.concatenate([wk / wsum * ROUTED_SCALE for wk in ws], axis=0)


def _route(x1, w_router, router_bias, tile):
    t, d = x1.shape
    n_exp = w_router.shape[1]
    wt = w_router.astype(F32).T
    wh = wt.astype(BF16)
    wl = (wt - wh.astype(F32)).astype(BF16)
    r = jnp.arange(tile)
    u = (r[:, None] < r[None, :]).astype(BF16)
    ones = jnp.ones((tile, LANES), BF16)
    full = lambda i: (0, 0)
    tok = lambda i: (0, i)
    return pl.pallas_call(
        functools.partial(_route_kernel, tile=tile, n_exp=n_exp),
        grid=(t // tile,),
        in_specs=[pl.BlockSpec((tile, d), lambda i: (i, 0)),
                  pl.BlockSpec((n_exp, d), full), pl.BlockSpec((n_exp, d), full),
                  pl.BlockSpec((n_exp, 1), full),
                  pl.BlockSpec((tile, tile), full), pl.BlockSpec((tile, LANES), full)],
        out_specs=[pl.BlockSpec((TOP_K, tile), tok), pl.BlockSpec((TOP_K, tile), tok),
                   pl.BlockSpec((TOP_K, tile), tok), pl.BlockSpec((n_exp, LANES), full)],
        out_shape=[jax.ShapeDtypeStruct((TOP_K, t), jnp.int32), jax.ShapeDtypeStruct((TOP_K, t), jnp.int32),
                   jax.ShapeDtypeStruct((TOP_K, t), F32), jax.ShapeDtypeStruct((n_exp, LANES), F32)],
        scratch_shapes=[pltpu.VMEM((n_exp, LANES), F32)],
        compiler_params=pltpu.CompilerParams(
            dimension_semantics=("arbitrary",), vmem_limit_bytes=VMEM_LIMIT),
        name="router_topk",
    )(x1, wh, wl, router_bias.astype(F32).reshape(n_exp, 1), u, ones)


def _dest_kernel(idx_ref, rank_ref, pstart_ref, dst_ref, *, tile, n_exp, nchunk):
    eid = lax.broadcasted_iota(jnp.int32, (n_exp, tile), 0)
    pstart = jnp.broadcast_to(pstart_ref[...], (n_exp, tile))
    rows = []
    for k in range(TOP_K):
        hit = eid == idx_ref[k:k + 1, :]
        base = jnp.sum(jnp.where(hit, pstart, 0.0), axis=0, keepdims=True)
        rows.append((base.astype(jnp.int32) + rank_ref[k:k + 1, :]) * nchunk)
    dst_ref[...] = jnp.concatenate(rows, axis=0)


def _dest_rows(idx_t, rank_t, pstart, nchunk, tile):
    t = idx_t.shape[1]
    n_exp = pstart.shape[0]
    tok = lambda i: (0, i)
    return pl.pallas_call(
        functools.partial(_dest_kernel, tile=tile, n_exp=n_exp, nchunk=nchunk),
        grid=(t // tile,),
        in_specs=[pl.BlockSpec((TOP_K, tile), tok), pl.BlockSpec((TOP_K, tile), tok),
                  pl.BlockSpec((n_exp, 1), lambda i: (0, 0))],
        out_specs=pl.BlockSpec((TOP_K, tile), tok),
        out_shape=jax.ShapeDtypeStruct((TOP_K, t), jnp.int32),
        compiler_params=pltpu.CompilerParams(
            dimension_semantics=("parallel",), vmem_limit_bytes=VMEM_LIMIT),
        name="dest_rows",
    )(idx_t, rank_t, pstart.astype(F32).reshape(n_exp, 1))


def _dispatch_kernel(pstart_ref, pend_ref, dst_hbm, x_hbm, xs_hbm,
                     dst_s, xbuf, zbuf, isem, lsem, dsem, zsem,
                     *, td, nchunk, mb, n_exp, nsteps, n_blocks):
    i = pl.program_id(0)
    rows = mb * nchunk
    trows = td * nchunk

    def idx_copy(step, slot):
        return pltpu.make_async_copy(dst_hbm.at[:, pl.ds(step * td, td)], dst_s.at[slot], isem.at[slot])

    def tile_load(step, slot):
        start = pl.multiple_of(step * trows, trows)
        return pltpu.make_async_copy(x_hbm.at[pl.ds(start, trows), :], xbuf.at[slot], lsem.at[slot])

    def wait_rows(slot):
        for _ in range(TOP_K):
            pltpu.make_async_copy(xbuf.at[slot], xbuf.at[slot], dsem.at[slot]).wait()

    @pl.when(i == 0)
    def _():
        idx_copy(0, 0).start()
        tile_load(0, 0).start()
        zbuf[...] = jnp.zeros_like(zbuf)

        def zero_copy(e):
            start = pl.multiple_of((pend_ref[e] - mb) * nchunk, nchunk)
            return pltpu.make_async_copy(zbuf, xs_hbm.at[pl.ds(start, rows), :], zsem)

        def zstart(e, _):
            @pl.when(pend_ref[e] > pstart_ref[e])
            def _():
                zero_copy(e).start()
            return 0

        def zwait(e, _):
            @pl.when(pend_ref[e] > pstart_ref[e])
            def _():
                zero_copy(e).wait()
            return 0

        lax.fori_loop(0, n_exp, zstart, 0)
        lax.fori_loop(0, n_exp, zwait, 0)

        def tail_copy(b):
            start = pl.multiple_of(b * rows, rows)
            return pltpu.make_async_copy(zbuf, xs_hbm.at[pl.ds(start, rows), :], zsem)

        def tstart(b, _):
            tail_copy(b).start()
            return 0

        def twait(b, _):
            tail_copy(b).wait()
            return 0

        first_unused = pend_ref[n_exp - 1] // mb
        lax.fori_loop(first_unused, n_blocks, tstart, 0)
        lax.fori_loop(first_unused, n_blocks, twait, 0)

    for cur in range(3):
        nxt = (cur + 1) % 3

        @pl.when(i % 3 == cur)
        def _():
            idx_copy(i, cur).wait()

            @pl.when(i >= 2)
            def _():
                wait_rows(nxt)

            @pl.when(i + 1 < nsteps)
            def _():
                idx_copy(i + 1, nxt).start()
                tile_load(i + 1, nxt).start()

            tile_load(i, cur).wait()

            def body(r, _):
                src = pl.multiple_of(r * nchunk, nchunk)
                for k in range(TOP_K):
                    dst = pl.multiple_of(dst_s[cur, k, r], nchunk)
                    pltpu.make_async_copy(xbuf.at[cur, pl.ds(src, nchunk), :],
                                          xs_hbm.at[pl.ds(dst, nchunk), :], dsem.at[cur]).start()
                return 0

            lax.fori_loop(0, td, body, 0, unroll=2)

            @pl.when(i == nsteps - 1)
            def _():
                @pl.when(i >= 1)
                def _():
                    wait_rows((cur + 2) % 3)

                wait_rows(cur)


def _dispatch(x2, dst_t, pstart, pend, p_rows, td):
    n_exp = pstart.shape[0]
    t = dst_t.shape[1]
    nchunk = x2.shape[0] // t
    nsteps = t // td
    grid_spec = pltpu.PrefetchScalarGridSpec(
        num_scalar_prefetch=2,
        grid=(nsteps,),
        in_specs=[pl.BlockSpec(memory_space=pl.ANY)] * 2,
        out_specs=pl.BlockSpec(memory_space=pl.ANY),
        scratch_shapes=[pltpu.SMEM((3, TOP_K, td), jnp.int32),
                        pltpu.VMEM((3, td * nchunk, LANES), F32),
                        pltpu.VMEM((MOE_BLOCK * nchunk, LANES), F32),
                        pltpu.SemaphoreType.DMA((3,)),
                        pltpu.SemaphoreType.DMA((3,)),
                        pltpu.SemaphoreType.DMA((3,)),
                        pltpu.SemaphoreType.DMA],
    )
    return pl.pallas_call(
        functools.partial(_dispatch_kernel, td=td, nchunk=nchunk, mb=MOE_BLOCK, n_exp=n_exp, nsteps=nsteps,
                          n_blocks=p_rows // MOE_BLOCK),
        grid_spec=grid_spec,
        out_shape=jax.ShapeDtypeStruct((p_rows * nchunk, LANES), F32),
        compiler_params=pltpu.CompilerParams(
            dimension_semantics=("arbitrary",), vmem_limit_bytes=VMEM_LIMIT),
        name="moe_dispatch",
    )(pstart, pend, dst_t, x2)


def _experts_kernel(bexp_ref, nused_ref, xs_ref, wg_ref, wu_ref, wd_ref, ys_ref, wg_b, wu_b, wd_b,
                    *, mb, nchunk):
    i = pl.program_id(0)

    @pl.when(i < nused_ref[0])
    def _():
        new_expert = jnp.logical_or(i == 0, bexp_ref[i] != bexp_ref[jnp.maximum(i - 1, 0)])

        @pl.when(new_expert)
        def _():
            wg_b[...] = wg_ref[0].astype(BF16)
            wu_b[...] = wu_ref[0].astype(BF16)
            wd_b[...] = wd_ref[0].astype(BF16)

        xs = [xs_ref[pl.ds(c, mb, stride=nchunk), :] for c in range(nchunk)]
        xb = jnp.concatenate(xs, axis=1).astype(BF16)
        gate = _dot(xb, wg_b[...])
        up = _dot(xb, wu_b[...])
        hid = (gate * _sigmoid(gate)) * up
        y = _dot(hid.astype(BF16), wd_b[...])
        for c in range(nchunk):
            ys_ref[pl.ds(c, mb, stride=nchunk), :] = y[:, c * LANES:(c + 1) * LANES]

    @pl.when(i >= nused_ref[0])
    def _():
        ys_ref[...] = jnp.zeros_like(ys_ref)


def _experts(xs, block_expert, nused, w_gate, w_up, w_down):
    e, d, de = w_gate.shape
    nchunk = d // LANES
    mb = MOE_BLOCK
    rows = mb * nchunk
    n_blocks = xs.shape[0] // rows
    blk = lambda i, be, nu: (jnp.minimum(i, nu[0] - 1), 0)
    wsel = lambda i, be, nu: (be[i], 0, 0)
    grid_spec = pltpu.PrefetchScalarGridSpec(
        num_scalar_prefetch=2,
        grid=(n_blocks,),
        in_specs=[pl.BlockSpec((rows, LANES), blk),
                  pl.BlockSpec((1, d, de), wsel), pl.BlockSpec((1, d, de), wsel),
                  pl.BlockSpec((1, de, d), wsel)],
        out_specs=pl.BlockSpec((rows, LANES), lambda i, be, nu: (i, 0)),
        scratch_shapes=[pltpu.VMEM((d, de), BF16), pltpu.VMEM((d, de), BF16), pltpu.VMEM((de, d), BF16)],
    )
    return pl.pallas_call(
        functools.partial(_experts_kernel, mb=mb, nchunk=nchunk),
        grid_spec=grid_spec,
        out_shape=jax.ShapeDtypeStruct(xs.shape, F32),
        compiler_params=pltpu.CompilerParams(
            dimension_semantics=("arbitrary",), vmem_limit_bytes=VMEM_LIMIT),
        name="routed_experts",
    )(block_expert, nused, xs, w_gate, w_up, w_down)


def _final_kernel(dst_hbm, w_hbm, ys_hbm, x1_ref, wsg_ref, wsu_ref, wsd_ref,
                  lg_ref, lb_ref, o_ref, ybuf, rbuf, dst_s, w_s, isem, gsem,
                  *, tm, nchunk, alpha, nsteps):
    i = pl.program_id(0)
    tok_rows = TOP_K * nchunk

    def idx_copies(step, slot):
        cols = pl.ds(step * tm, tm)
        return (pltpu.make_async_copy(dst_hbm.at[:, cols], dst_s.at[slot], isem.at[slot]),
                pltpu.make_async_copy(w_hbm.at[:, cols], w_s.at[slot], isem.at[slot]))

    def issue_gathers(slot):
        def body(r, _):
            for k in range(TOP_K):
                src = pl.multiple_of(dst_s[slot, k, r], nchunk)
                pltpu.make_async_copy(ys_hbm.at[pl.ds(src, nchunk), :],
                                      ybuf.at[slot, pl.ds(r * tok_rows + k * nchunk, nchunk), :],
                                      gsem.at[slot]).start()
            return 0

        lax.fori_loop(0, tm, body, 0, unroll=2)

    def combine(slot):
        def body(r, _):
            acc = ybuf[slot, pl.ds(r * tok_rows, nchunk), :] * w_s[slot, 0, r]
            for k in range(1, TOP_K):
                acc = acc + ybuf[slot, pl.ds(r * tok_rows + k * nchunk, nchunk), :] * w_s[slot, k, r]
            rbuf[pl.ds(r * nchunk, nchunk), :] = acc
            return 0

        lax.fori_loop(0, tm, body, 0, unroll=2)

    @pl.when(i == 0)
    def _():
        for cp in idx_copies(0, 0):
            cp.start()
        for cp in idx_copies(0, 0):
            cp.wait()
        issue_gathers(0)

        @pl.when(nsteps > 1)
        def _():
            for cp in idx_copies(1, 1):
                cp.start()

    for cur in range(2):
        nxt = 1 - cur

        @pl.when(i % 2 == cur)
        def _():
            @pl.when(i + 1 < nsteps)
            def _():
                for cp in idx_copies(i + 1, nxt):
                    cp.wait()
                issue_gathers(nxt)

            pltpu.make_async_copy(ybuf.at[cur], ybuf.at[cur], gsem.at[cur]).wait()
            combine(cur)

            @pl.when(i + 2 < nsteps)
            def _():
                for cp in idx_copies(i + 2, cur):
                    cp.start()

    routed = jnp.concatenate([rbuf[pl.ds(c, tm, stride=nchunk), :] for c in range(nchunk)], axis=1)

    x1 = x1_ref[...]
    xb = x1.astype(BF16)
    gate = _dot(xb, wsg_ref[...])
    up = _dot(xb, wsu_ref[...])
    shared = _dot(((gate * _sigmoid(gate)) * up).astype(BF16), wsd_ref[...])
    o_ref[...] = _layer_norm(alpha * x1 + (routed + shared), lg_ref[...], lb_ref[...])


def _final(ys, dst_t, w_t, x1, wsg, wsu, wsd, lg, lb, alpha, tm):
    t, d = x1.shape
    nchunk = d // LANES
    ds = wsg.shape[1]
    nsteps = t // tm
    full = lambda i: (0, 0)
    return pl.pallas_call(
        functools.partial(_final_kernel, tm=tm, nchunk=nchunk, alpha=alpha, nsteps=nsteps),
        grid=(nsteps,),
        in_specs=[pl.BlockSpec(memory_space=pl.ANY)] * 3 + [
            pl.BlockSpec((tm, d), lambda i: (i, 0)),
            pl.BlockSpec((d, ds), full), pl.BlockSpec((d, ds), full), pl.BlockSpec((ds, d), full),
            pl.BlockSpec((1, d), full), pl.BlockSpec((1, d), full)],
        out_specs=pl.BlockSpec((tm, d), lambda i: (i, 0)),
        out_shape=jax.ShapeDtypeStruct((t, d), F32),
        scratch_shapes=[pltpu.VMEM((2, tm * TOP_K * nchunk, LANES), F32),
                        pltpu.VMEM((tm * nchunk, LANES), F32),
                        pltpu.SMEM((2, TOP_K, tm), jnp.int32),
                        pltpu.SMEM((2, TOP_K, tm), F32),
                        pltpu.SemaphoreType.DMA((2,)),
                        pltpu.SemaphoreType.DMA((2,))],
        compiler_params=pltpu.CompilerParams(
            dimension_semantics=("arbitrary",), vmem_limit_bytes=VMEM_LIMIT),
        name="combine_shared_ln2",
    )(dst_t, w_t, ys, x1, wsg, wsu, wsd, lg, lb)


def _block_tables(counts, n_rows):
    mb = MOE_BLOCK
    n_exp = counts.shape[0]
    padded = (counts + mb - 1) // mb * mb
    pad_end = jnp.cumsum(padded)
    pad_start = pad_end - padded
    n_blocks = n_rows // mb
    first_row = jnp.arange(n_blocks, dtype=jnp.int32) * mb
    block_expert = jnp.minimum(
        jnp.sum((pad_end[None, :] <= first_row[:, None]).astype(jnp.int32), axis=1), n_exp - 1)
    nused = (pad_end[-1:] // mb).astype(jnp.int32)
    return pad_start.astype(jnp.int32), pad_end.astype(jnp.int32), block_expert, nused


def _tile(n, pref):
    while n % pref:
        pref //= 2
    return pref


def kernel(x, w_in, lower_bounds, hg_norm_g, w_branch_hg, w_branch_sb, w_out, ln1_g, ln1_b,
           w_router, router_bias, w_exp_gate, w_exp_up, w_exp_down,
           w_sh_gate, w_sh_up, w_sh_down, ln2_g, ln2_b):
    depth = w_in.shape[0]
    assert depth == 1, "single-layer block only"
    batch, seq, d = x.shape
    t = batch * seq
    n_exp = w_router.shape[-1]
    alpha = (2.0 * depth) ** 0.25
    nchunk = d // LANES

    lb = jnp.cumsum(jax.nn.softmax(lower_bounds.astype(F32), axis=0), axis=0)[0]
    xf = x.reshape(t, d)

    w = w_in[0]
    c = d
    w32 = jnp.concatenate([w[:, 0:2 * c], w[:, 3 * c:4 * c], w[:, 7 * c:9 * c]], axis=1).astype(BF16)
    wbf = jnp.concatenate([w[:, 2 * c:3 * c], w[:, 4 * c:7 * c]], axis=1).astype(BF16)
    tm = _tile(t, 1024)
    p32 = _matmul(xf, w32, F32, tm, 512)
    pbf = _matmul(xf, wbf, BF16, tm, 512)

    o_hg = _hgrn(p32, pbf, lb, hg_norm_g[0].astype(F32), batch, seq, _tile(seq, 256), 4)
    o_sb = _sb_attention(pbf, batch, seq, _tile(seq, 256), 2)

    x1, x1s = _merge(o_hg, o_sb, p32, xf,
                     w_branch_hg[0].astype(BF16), w_branch_sb[0].astype(BF16), w_out[0].astype(BF16),
                     ln1_g[0].reshape(1, d).astype(F32), ln1_b[0].reshape(1, d).astype(F32),
                     alpha, _tile(t, 512))

    idx_t, rank_t, w_t, cnt = _route(x1, w_router[0], router_bias[0], _tile(t, 256))
    p_rows = t * TOP_K + n_exp * MOE_BLOCK
    pstart, pend, block_expert, nused = _block_tables(cnt[:, 0].astype(jnp.int32), p_rows)
    dst_t = _dest_rows(idx_t, rank_t, pstart, nchunk, _tile(t, 256))
    xs = _dispatch(x1s, dst_t, pstart, pend, p_rows, _tile(t, 128))
    ys = _experts(xs, block_expert, nused, w_exp_gate[0], w_exp_up[0], w_exp_down[0])
    out = _final(ys, dst_t, w_t, x1,
                 w_sh_gate[0].astype(BF16), w_sh_up[0].astype(BF16), w_sh_down[0].astype(BF16),
                 ln2_g[0].reshape(1, d).astype(F32), ln2_b[0].reshape(1, d).astype(F32),
                 alpha, _tile(t, 128))
    return out.reshape(batch, seq, d)
```

```python
import functools

import jax
import jax.numpy as jnp
from jax import lax
from jax.experimental import pallas as pl
from jax.experimental.pallas import tpu as pltpu

F32 = jnp.float32
BF16 = jnp.bfloat16

LANES = 128
SUBLANES = 8
VMEM_LIMIT = 48 * 1024 * 1024

HG_HEADS = 8
HG_CHUNK = 32
SB_HEADS = 8
N_GROUPS = 8
TOPK_GROUPS = 4
TOP_K = 8
ROUTED_SCALE = 2.5
MOE_BLOCK = 256
LN_EPS = 1e-5
RMS_EPS = 1e-6
STICK_DEAD = 110.0

NT_DIMS = (((1,), (1,)), ((), ()))
TN_DIMS = (((0,), (0,)), ((), ()))


def _dot(a, b):
    return jnp.dot(a, b, preferred_element_type=F32)


def _sigmoid(x):
    return 1.0 / (1.0 + jnp.exp(-x))


def _split3(x):
    hi = x.astype(BF16)
    r1 = x - hi.astype(F32)
    mid = r1.astype(BF16)
    lo = (r1 - mid.astype(F32)).astype(BF16)
    return hi, mid, lo


def _layer_norm(r, g, b):
    mu = jnp.mean(r, axis=-1, keepdims=True)
    d = r - mu
    var = jnp.mean(d * d, axis=-1, keepdims=True)
    return d * lax.rsqrt(var + LN_EPS) * g + b


def _mm_kernel(x_ref, w_ref, o_ref):
    o_ref[...] = _dot(x_ref[...].astype(BF16), w_ref[...]).astype(o_ref.dtype)


def _matmul(x, w, out_dtype, tm, tn):
    m, k = x.shape
    n = w.shape[1]
    return pl.pallas_call(
        _mm_kernel,
        grid=(m // tm, n // tn),
        in_specs=[pl.BlockSpec((tm, k), lambda i, j: (i, 0)),
                  pl.BlockSpec((k, tn), lambda i, j: (0, j))],
        out_specs=pl.BlockSpec((tm, tn), lambda i, j: (i, j)),
        out_shape=jax.ShapeDtypeStruct((m, n), out_dtype),
        compiler_params=pltpu.CompilerParams(
            dimension_semantics=("parallel", "arbitrary"), vmem_limit_bytes=VMEM_LIMIT),
        name="in_proj",
    )(x, w)


def _hgrn_kernel(lb_ref, g_ref, q_ref, f_ref, v_ref, og_ref, o_ref, st_ref, *, ts, chunk, hp):
    @pl.when(pl.program_id(2) == 0)
    def _():
        st_ref[...] = jnp.zeros_like(st_ref)

    shift = chunk.bit_length() - 1
    row = lax.broadcasted_iota(jnp.int32, (ts, ts), 0)
    col = lax.broadcasted_iota(jnp.int32, (ts, ts), 1)
    same = (row >> shift) == (col >> shift)
    causal = jnp.logical_and(same, col <= row)
    tri = jnp.where(causal, 1.0, 0.0).astype(BF16)
    ones = jnp.where(same, 1.0, 0.0).astype(BF16)
    d = LANES

    for hh in range(hp):
        cols = slice(hh * d, (hh + 1) * d)
        lb = lb_ref[0, :, cols]
        hf = f_ref[:, cols]
        log_f = jnp.log(lb + (1.0 - lb) * _sigmoid(hf))
        k_in = (1.0 - lb) * _sigmoid(-hf)

        parts = jnp.concatenate(_split3(log_f), axis=1)
        cs = _dot(tri, parts)
        tot = _dot(ones, parts)
        b = cs[:, :d] + cs[:, d:2 * d] + cs[:, 2 * d:]
        b_end = tot[:, :d] + tot[:, d:2 * d] + tot[:, 2 * d:]

        q = q_ref[:, cols]
        q_start = (q * jnp.exp(b)).astype(BF16)
        q_end = (q * jnp.exp(b - b_end)).astype(BF16)
        k_end = (k_in * jnp.exp(b_end - b)).astype(BF16)
        v = v_ref[:, cols]

        scores = lax.dot_general(q_end, k_end, NT_DIMS, preferred_element_type=F32)
        scores = jnp.where(causal, scores, 0.0)
        o_intra = _dot(scores.astype(BF16), v)

        dec = jnp.exp(b_end)
        st = st_ref[hh]
        outs = []
        for c in range(ts // chunk):
            lo = c * chunk
            outs.append(lax.dot_general(q_start[lo:lo + chunk], st.astype(BF16), NT_DIMS,
                                        preferred_element_type=F32))
            kv = lax.dot_general(v[lo:lo + chunk], k_end[lo:lo + chunk], TN_DIMS,
                                 preferred_element_type=F32)
            st = dec[lo:lo + 1, :] * st + kv
        st_ref[hh] = st
        o = o_intra + jnp.concatenate(outs, axis=0)

        o = o * lax.rsqrt(jnp.mean(o * o, axis=-1, keepdims=True) + RMS_EPS)
        o = o * g_ref[0, :, cols]
        hg = og_ref[:, cols]
        o_ref[:, cols] = (o * (hg * _sigmoid(hg))).astype(o_ref.dtype)


def _hgrn(p32, pbf, lb, g, batch, seq, ts, hp):
    t = batch * seq
    h = HG_HEADS
    ng = h // hp
    w = hp * LANES
    ns = seq // ts
    row = lambda off: (lambda b, hg, s: (b * ns + s, off + hg))
    par = lambda b, hg, s: (hg, 0, 0)
    return pl.pallas_call(
        functools.partial(_hgrn_kernel, ts=ts, chunk=HG_CHUNK, hp=hp),
        grid=(batch, ng, ns),
        in_specs=[pl.BlockSpec((1, 1, w), par),
                  pl.BlockSpec((1, 1, w), par),
                  pl.BlockSpec((ts, w), row(0)),
                  pl.BlockSpec((ts, w), row(ng)),
                  pl.BlockSpec((ts, w), row(0)),
                  pl.BlockSpec((ts, w), row(2 * ng))],
        out_specs=pl.BlockSpec((ts, w), row(0)),
        out_shape=jax.ShapeDtypeStruct((t, h * LANES), BF16),
        scratch_shapes=[pltpu.VMEM((hp, LANES, LANES), F32)],
        compiler_params=pltpu.CompilerParams(
            dimension_semantics=("parallel", "parallel", "arbitrary"), vmem_limit_bytes=VMEM_LIMIT),
        name="hgrn2",
    )(lb.reshape(ng, 1, w), g.reshape(ng, 1, w), p32, p32, pbf, p32)


def _sb_kernel(q_ref, k_ref, v_ref, u_ref, o_ref, *, tq, scale, hp):
    i = pl.program_id(2)
    u = u_ref[...]
    rep = tq // LANES
    d = LANES

    def block(hh, j, carry, acc, masked, live=None):
        cols = slice(hh * d, (hh + 1) * d)
        start = pl.multiple_of(j * tq, tq)
        q = q_ref[:, cols]
        kj = k_ref[pl.ds(start, tq), cols]
        vj = v_ref[pl.ds(start, tq), cols]
        z = lax.dot_general(q, kj, NT_DIMS, preferred_element_type=F32) * scale
        sp = jnp.maximum(z, 0.0) + jnp.log(1.0 + jnp.exp(-jnp.abs(z)))
        if masked:
            row = lax.broadcasted_iota(jnp.int32, (tq, tq), 0)
            col = lax.broadcasted_iota(jnp.int32, (tq, tq), 1)
            before = col < row
            drop = jnp.where(before, sp, 0.0)
        elif live is not None:
            drop = sp * live
        else:
            drop = sp
        hi = drop.astype(BF16)
        lo = (drop - hi.astype(F32)).astype(BF16)
        later = _dot(hi, u) + _dot(lo, u)
        stick = later + jnp.concatenate([carry] * rep, axis=1)
        w = jnp.exp((z - sp) - stick)
        if masked:
            w = jnp.where(before, w, 0.0)
        elif live is not None:
            w = w * live
        acc = acc + _dot(w.astype(BF16), vj)
        carry = carry + jnp.broadcast_to(later[:, 0:1] + drop[:, 0:1], (tq, LANES))
        return carry, acc

    def some_row_alive(carries):
        m = jnp.min(carries[0])
        for c in carries[1:]:
            m = jnp.minimum(m, jnp.min(c))
        return (m < STICK_DEAD).astype(jnp.int32)

    zero = jnp.zeros((tq, LANES), F32)
    live = (i > 0).astype(F32)
    prev = jnp.maximum(i - 1, 0)
    carries, accs = [], []
    for hh in range(hp):
        carry, acc = block(hh, i, zero, zero, True)
        carry, acc = block(hh, prev, carry, acc, False, live=live)
        carries.append(carry)
        accs.append(acc)

    def cond(state):
        return jnp.logical_and(state[0] >= 0, state[1] > 0)

    def body(state):
        j, _, carries, accs = state
        out = [block(hh, j, carries[hh], accs[hh], False) for hh in range(hp)]
        carries = tuple(o[0] for o in out)
        accs = tuple(o[1] for o in out)
        return j - 1, some_row_alive(carries), carries, accs

    _, _, carries, accs = lax.while_loop(
        cond, body, (i - 2, some_row_alive(carries), tuple(carries), tuple(accs)))
    for hh in range(hp):
        o_ref[:, hh * d:(hh + 1) * d] = accs[hh].astype(o_ref.dtype)


def _sb_attention(pbf, batch, seq, tq, hp):
    t = batch * seq
    h = SB_HEADS
    ng = h // hp
    w = hp * LANES
    nq = seq // tq
    r = jnp.arange(tq)
    u = (r[:, None] > r[None, :]).astype(BF16)
    return pl.pallas_call(
        functools.partial(_sb_kernel, tq=tq, scale=float(LANES) ** -0.5, hp=hp),
        grid=(batch, ng, nq),
        in_specs=[pl.BlockSpec((tq, w), lambda b, hg, i: (b * nq + i, ng + hg)),
                  pl.BlockSpec((seq, w), lambda b, hg, i: (b, 2 * ng + hg)),
                  pl.BlockSpec((seq, w), lambda b, hg, i: (b, 3 * ng + hg)),
                  pl.BlockSpec((tq, tq), lambda b, hg, i: (0, 0))],
        out_specs=pl.BlockSpec((tq, w), lambda b, hg, i: (b * nq + i, hg)),
        out_shape=jax.ShapeDtypeStruct((t, h * LANES), BF16),
        compiler_params=pltpu.CompilerParams(
            dimension_semantics=("parallel", "parallel", "arbitrary"), vmem_limit_bytes=VMEM_LIMIT),
        name="stick_breaking",
    )(pbf, pbf, pbf, u)


def _merge_kernel(ohg_ref, osb_ref, ghg_ref, gsb_ref, x_ref, wbh_ref, wbs_ref, wo_ref,
                  lg_ref, lb_ref, x1_ref, x1s_ref, *, alpha, tm, nchunk):
    y_hg = _dot(ohg_ref[...], wbh_ref[...])
    y_sb = _dot(osb_ref[...], wbs_ref[...])
    merged = _sigmoid(ghg_ref[...]) * y_hg + _sigmoid(gsb_ref[...]) * y_sb
    hmix = _dot(merged.astype(BF16), wo_ref[...])
    x1 = _layer_norm(alpha * x_ref[...] + hmix, lg_ref[...], lb_ref[...])
    x1_ref[...] = x1
    for c in range(nchunk):
        x1s_ref[pl.ds(c, tm, stride=nchunk), :] = x1[:, c * LANES:(c + 1) * LANES]


def _merge(o_hg, o_sb, p32, x, wbh, wbs, wo, lg, lb, alpha, tm):
    t, d = x.shape
    nchunk = d // LANES
    row = lambda off: (lambda i: (i, off))
    full = lambda i: (0, 0)
    return pl.pallas_call(
        functools.partial(_merge_kernel, alpha=alpha, tm=tm, nchunk=nchunk),
        grid=(t // tm,),
        in_specs=[pl.BlockSpec((tm, d), row(0)), pl.BlockSpec((tm, d), row(0)),
                  pl.BlockSpec((tm, d), row(3)), pl.BlockSpec((tm, d), row(4)),
                  pl.BlockSpec((tm, d), row(0)),
                  pl.BlockSpec((d, d), full), pl.BlockSpec((d, d), full), pl.BlockSpec((d, d), full),
                  pl.BlockSpec((1, d), full), pl.BlockSpec((1, d), full)],
        out_specs=[pl.BlockSpec((tm, d), row(0)), pl.BlockSpec((tm * nchunk, LANES), row(0))],
        out_shape=[jax.ShapeDtypeStruct((t, d), F32), jax.ShapeDtypeStruct((t * nchunk, LANES), F32)],
        compiler_params=pltpu.CompilerParams(
            dimension_semantics=("parallel",), vmem_limit_bytes=VMEM_LIMIT),
        name="merge_ln1",
    )(o_hg, o_sb, p32, p32, x, wbh, wbs, wo, lg, lb)


def _route_kernel(x_ref, wh_ref, wl_ref, bias_ref, u_ref, ones_ref,
                  idx_ref, rank_ref, w_ref, cnt_ref, carry, *, tile, n_exp):
    @pl.when(pl.program_id(0) == 0)
    def _():
        carry[...] = jnp.zeros_like(carry)

    x = x_ref[...]
    xh = x.astype(BF16)
    xl = (x - xh.astype(F32)).astype(BF16)
    wh = wh_ref[...]
    nt = functools.partial(lax.dot_general, dimension_numbers=NT_DIMS, preferred_element_type=F32)
    logits = nt(wh, xh) + nt(wh, xl) + nt(wl_ref[...], xh)
    scores = _sigmoid(logits)
    biased = scores + bias_ref[...]

    neg_inf = -jnp.inf
    gsz = n_exp // N_GROUPS
    groups = [biased[g * gsz:(g + 1) * gsz] for g in range(N_GROUPS)]
    gscore = []
    for xg in groups:
        m1 = jnp.max(xg, axis=0, keepdims=True)
        n1 = jnp.sum(jnp.where(xg == m1, 1.0, 0.0), axis=0, keepdims=True)
        m2 = jnp.max(jnp.where(xg < m1, xg, neg_inf), axis=0, keepdims=True)
        gscore.append(m1 + jnp.where(n1 >= 2.0, m1, m2))
    masked = []
    for g in range(N_GROUPS):
        beaten_by = jnp.zeros_like(gscore[g])
        for h in range(N_GROUPS):
            if h < g:
                beaten_by = beaten_by + jnp.where(gscore[h] >= gscore[g], 1.0, 0.0)
            elif h > g:
                beaten_by = beaten_by + jnp.where(gscore[h] > gscore[g], 1.0, 0.0)
        keep = jnp.broadcast_to(beaten_by, (gsz, tile)) < float(TOPK_GROUPS)
        masked.append(jnp.where(keep, groups[g], jnp.finfo(F32).min))
    cur = jnp.concatenate(masked, axis=0)

    eid = lax.broadcasted_iota(jnp.int32, (n_exp, tile), 0).astype(F32)
    chosen = jnp.zeros((n_exp, tile), F32)
    idxs, ws = [], []
    for _ in range(TOP_K):
        m = jnp.max(cur, axis=0, keepdims=True)
        ik = jnp.min(jnp.where(cur == m, eid, float(n_exp)), axis=0, keepdims=True)
        sel = eid == ik
        idxs.append(ik)
        ws.append(jnp.sum(jnp.where(sel, scores, 0.0), axis=0, keepdims=True))
        chosen = chosen + jnp.where(sel, 1.0, 0.0)
        cur = jnp.where(sel, neg_inf, cur)

    chosen_b = chosen.astype(BF16)
    run = carry[...]
    rank = _dot(chosen_b, u_ref[...]) + jnp.concatenate([run] * (tile // LANES), axis=1)
    ranks = [jnp.sum(jnp.where(eid == ik, rank, 0.0), axis=0, keepdims=True) for ik in idxs]
    run = run + _dot(chosen_b, ones_ref[...])
    carry[...] = run
    cnt_ref[...] = run

    wsum = ws[0]
    for wk in ws[1:]:
        wsum = wsum + wk
    idx_ref[...] = jnp.concatenate(idxs, axis=0).astype(jnp.int32)
    rank_ref[...] = jnp.concatenate(ranks, axis=0).astype(jnp.int32)
    w_ref[...] = jnp.concatenate([wk / wsum * ROUTED_SCALE for wk in ws], axis=0)


def _route(x1, w_router, router_bias, tile):
    t, d = x1.shape
    n_exp = w_router.shape[1]
    wt = w_router.astype(F32).T
    wh = wt.astype(BF16)
    wl = (wt - wh.astype(F32)).astype(BF16)
    r = jnp.arange(tile)
    u = (r[:, None] < r[None, :]).astype(BF16)
    ones = jnp.ones((tile, LANES), BF16)
    full = lambda i: (0, 0)
    tok = lambda i: (0, i)
    return pl.pallas_call(
        functools.partial(_route_kernel, tile=tile, n_exp=n_exp),
        grid=(t // tile,),
        in_specs=[pl.BlockSpec((tile, d), lambda i: (i, 0)),
                  pl.BlockSpec((n_exp, d), full), pl.BlockSpec((n_exp, d), full),
                  pl.BlockSpec((n_exp, 1), full),
                  pl.BlockSpec((tile, tile), full), pl.BlockSpec((tile, LANES), full)],
        out_specs=[pl.BlockSpec((TOP_K, tile), tok), pl.BlockSpec((TOP_K, tile), tok),
                   pl.BlockSpec((TOP_K, tile), tok), pl.BlockSpec((n_exp, LANES), full)],
        out_shape=[jax.ShapeDtypeStruct((TOP_K, t), jnp.int32), jax.ShapeDtypeStruct((TOP_K, t), jnp.int32),
                   jax.ShapeDtypeStruct((TOP_K, t), F32), jax.ShapeDtypeStruct((n_exp, LANES), F32)],
        scratch_shapes=[pltpu.VMEM((n_exp, LANES), F32)],
        compiler_params=pltpu.CompilerParams(
            dimension_semantics=("arbitrary",), vmem_limit_bytes=VMEM_LIMIT),
        name="router_topk",
    )(x1, wh, wl, router_bias.astype(F32).reshape(n_exp, 1), u, ones)


def _dest_kernel(idx_ref, rank_ref, pstart_ref, dst_ref, *, tile, n_exp, nchunk):
    eid = lax.broadcasted_iota(jnp.int32, (n_exp, tile), 0)
    pstart = jnp.broadcast_to(pstart_ref[...], (n_exp, tile))
    rows = []
    for k in range(TOP_K):
        hit = eid == idx_ref[k:k + 1, :]
        base = jnp.sum(jnp.where(hit, pstart, 0.0), axis=0, keepdims=True)
        rows.append((base.astype(jnp.int32) + rank_ref[k:k + 1, :]) * nchunk)
    dst_ref[...] = jnp.concatenate(rows, axis=0)


def _dest_rows(idx_t, rank_t, pstart, nchunk, tile):
    t = idx_t.shape[1]
    n_exp = pstart.shape[0]
    tok = lambda i: (0, i)
    return pl.pallas_call(
        functools.partial(_dest_kernel, tile=tile, n_exp=n_exp, nchunk=nchunk),
        grid=(t // tile,),
        in_specs=[pl.BlockSpec((TOP_K, tile), tok), pl.BlockSpec((TOP_K, tile), tok),
                  pl.BlockSpec((n_exp, 1), lambda i: (0, 0))],
        out_specs=pl.BlockSpec((TOP_K, tile), tok),
        out_shape=jax.ShapeDtypeStruct((TOP_K, t), jnp.int32),
        compiler_params=pltpu.CompilerParams(
            dimension_semantics=("parallel",), vmem_limit_bytes=VMEM_LIMIT),
        name="dest_rows",
    )(idx_t, rank_t, pstart.astype(F32).reshape(n_exp, 1))


def _dispatch_kernel(pstart_ref, pend_ref, dst_hbm, x_hbm, xs_hbm,
                     dst_s, xbuf, zbuf, isem, lsem, dsem, zsem,
                     *, td, nchunk, mb, n_exp, nsteps, n_blocks):
    i = pl.program_id(0)
    rows = mb * nchunk
    trows = td * nchunk

    def idx_copy(step, slot):
        return pltpu.make_async_copy(dst_hbm.at[:, pl.ds(step * td, td)], dst_s.at[slot], isem.at[slot])

    def tile_load(step, slot):
        start = pl.multiple_of(step * trows, trows)
        return pltpu.make_async_copy(x_hbm.at[pl.ds(start, trows), :], xbuf.at[slot], lsem.at[slot])

    def wait_rows(slot):
        for _ in range(TOP_K):
            pltpu.make_async_copy(xbuf.at[slot], xbuf.at[slot], dsem.at[slot]).wait()

    @pl.when(i == 0)
    def _():
        idx_copy(0, 0).start()
        tile_load(0, 0).start()
        zbuf[...] = jnp.zeros_like(zbuf)

        def zero_copy(e):
            start = pl.multiple_of((pend_ref[e] - mb) * nchunk, nchunk)
            return pltpu.make_async_copy(zbuf, xs_hbm.at[pl.ds(start, rows), :], zsem)

        def zstart(e, _):
            @pl.when(pend_ref[e] > pstart_ref[e])
            def _():
                zero_copy(e).start()
            return 0

        def zwait(e, _):
            @pl.when(pend_ref[e] > pstart_ref[e])
            def _():
                zero_copy(e).wait()
            return 0

        lax.fori_loop(0, n_exp, zstart, 0)
        lax.fori_loop(0, n_exp, zwait, 0)

        def tail_copy(b):
            start = pl.multiple_of(b * rows, rows)
            return pltpu.make_async_copy(zbuf, xs_hbm.at[pl.ds(start, rows), :], zsem)

        def tstart(b, _):
            tail_copy(b).start()
            return 0

        def twait(b, _):
            tail_copy(b).wait()
            return 0

        first_unused = pend_ref[n_exp - 1] // mb
        lax.fori_loop(first_unused, n_blocks, tstart, 0)
        lax.fori_loop(first_unused, n_blocks, twait, 0)

    for cur in range(3):
        nxt = (cur + 1) % 3

        @pl.when(i % 3 == cur)
        def _():
            idx_copy(i, cur).wait()

            @pl.when(i >= 2)
            def _():
                wait_rows(nxt)

            @pl.when(i + 1 < nsteps)
            def _():
                idx_copy(i + 1, nxt).start()
                tile_load(i + 1, nxt).start()

            tile_load(i, cur).wait()

            def body(r, _):
                src = pl.multiple_of(r * nchunk, nchunk)
                for k in range(TOP_K):
                    dst = pl.multiple_of(dst_s[cur, k, r], nchunk)
                    pltpu.make_async_copy(xbuf.at[cur, pl.ds(src, nchunk), :],
                                          xs_hbm.at[pl.ds(dst, nchunk), :], dsem.at[cur]).start(priority=k % 2)
                return 0

            lax.fori_loop(0, td, body, 0, unroll=2)

            @pl.when(i == nsteps - 1)
            def _():
                @pl.when(i >= 1)
                def _():
                    wait_rows((cur + 2) % 3)

                wait_rows(cur)


def _dispatch(x2, dst_t, pstart, pend, p_rows, td):
    n_exp = pstart.shape[0]
    t = dst_t.shape[1]
    nchunk = x2.shape[0] // t
    nsteps = t // td
    grid_spec = pltpu.PrefetchScalarGridSpec(
        num_scalar_prefetch=2,
        grid=(nsteps,),
        in_specs=[pl.BlockSpec(memory_space=pl.ANY)] * 2,
        out_specs=pl.BlockSpec(memory_space=pl.ANY),
        scratch_shapes=[pltpu.SMEM((3, TOP_K, td), jnp.int32),
                        pltpu.VMEM((3, td * nchunk, LANES), F32),
                        pltpu.VMEM((MOE_BLOCK * nchunk, LANES), F32),
                        pltpu.SemaphoreType.DMA((3,)),
                        pltpu.SemaphoreType.DMA((3,)),
                        pltpu.SemaphoreType.DMA((3,)),
                        pltpu.SemaphoreType.DMA],
    )
    return pl.pallas_call(
        functools.partial(_dispatch_kernel, td=td, nchunk=nchunk, mb=MOE_BLOCK, n_exp=n_exp, nsteps=nsteps,
                          n_blocks=p_rows // MOE_BLOCK),
        grid_spec=grid_spec,
        out_shape=jax.ShapeDtypeStruct((p_rows * nchunk, LANES), F32),
        compiler_params=pltpu.CompilerParams(
            dimension_semantics=("arbitrary",), vmem_limit_bytes=VMEM_LIMIT),
        name="moe_dispatch",
    )(pstart, pend, dst_t, x2)


def _experts_kernel(bexp_ref, nused_ref, xs_ref, wg_ref, wu_ref, wd_ref, ys_ref, wg_b, wu_b, wd_b,
                    *, mb, nchunk):
    i = pl.program_id(0)

    @pl.when(i < nused_ref[0])
    def _():
        new_expert = jnp.logical_or(i == 0, bexp_ref[i] != bexp_ref[jnp.maximum(i - 1, 0)])

        @pl.when(new_expert)
        def _():
            wg_b[...] = wg_ref[0].astype(BF16)
            wu_b[...] = wu_ref[0].astype(BF16)
            wd_b[...] = wd_ref[0].astype(BF16)

        xs = [xs_ref[pl.ds(c, mb, stride=nchunk), :] for c in range(nchunk)]
        xb = jnp.concatenate(xs, axis=1).astype(BF16)
        gate = _dot(xb, wg_b[...])
        up = _dot(xb, wu_b[...])
        hid = (gate * _sigmoid(gate)) * up
        y = _dot(hid.astype(BF16), wd_b[...])
        for c in range(nchunk):
            ys_ref[pl.ds(c, mb, stride=nchunk), :] = y[:, c * LANES:(c + 1) * LANES]

    @pl.when(i >= nused_ref[0])
    def _():
        ys_ref[...] = jnp.zeros_like(ys_ref)


def _experts(xs, block_expert, nused, w_gate, w_up, w_down):
    e, d, de = w_gate.shape
    nchunk = d // LANES
    mb = MOE_BLOCK
    rows = mb * nchunk
    n_blocks = xs.shape[0] // rows
    blk = lambda i, be, nu: (jnp.minimum(i, nu[0] - 1), 0)
    wsel = lambda i, be, nu: (be[i], 0, 0)
    grid_spec = pltpu.PrefetchScalarGridSpec(
        num_scalar_prefetch=2,
        grid=(n_blocks,),
        in_specs=[pl.BlockSpec((rows, LANES), blk),
                  pl.BlockSpec((1, d, de), wsel), pl.BlockSpec((1, d, de), wsel),
                  pl.BlockSpec((1, de, d), wsel)],
        out_specs=pl.BlockSpec((rows, LANES), lambda i, be, nu: (i, 0)),
        scratch_shapes=[pltpu.VMEM((d, de), BF16), pltpu.VMEM((d, de), BF16), pltpu.VMEM((de, d), BF16)],
    )
    return pl.pallas_call(
        functools.partial(_experts_kernel, mb=mb, nchunk=nchunk),
        grid_spec=grid_spec,
        out_shape=jax.ShapeDtypeStruct(xs.shape, F32),
        compiler_params=pltpu.CompilerParams(
            dimension_semantics=("arbitrary",), vmem_limit_bytes=VMEM_LIMIT),
        name="routed_experts",
    )(block_expert, nused, xs, w_gate, w_up, w_down)


def _final_kernel(dst_hbm, w_hbm, ys_hbm, x1_ref, wsg_ref, wsu_ref, wsd_ref,
                  lg_ref, lb_ref, o_ref, ybuf, rbuf, dst_s, w_s, isem, gsem,
                  *, tm, nchunk, alpha, nsteps):
    i = pl.program_id(0)
    tok_rows = TOP_K * nchunk

    def idx_copies(step, slot):
        cols = pl.ds(step * tm, tm)
        return (pltpu.make_async_copy(dst_hbm.at[:, cols], dst_s.at[slot], isem.at[slot]),
                pltpu.make_async_copy(w_hbm.at[:, cols], w_s.at[slot], isem.at[slot]))

    def issue_gathers(slot):
        def body(r, _):
            for k in range(TOP_K):
                src = pl.multiple_of(dst_s[slot, k, r], nchunk)
                pltpu.make_async_copy(ys_hbm.at[pl.ds(src, nchunk), :],
                                      ybuf.at[slot, pl.ds(r * tok_rows + k * nchunk, nchunk), :],
                                      gsem.at[slot]).start(priority=k % 2)
            return 0

        lax.fori_loop(0, tm, body, 0, unroll=2)

    def combine(slot):
        def body(r, _):
            acc = ybuf[slot, pl.ds(r * tok_rows, nchunk), :] * w_s[slot, 0, r]
            for k in range(1, TOP_K):
                acc = acc + ybuf[slot, pl.ds(r * tok_rows + k * nchunk, nchunk), :] * w_s[slot, k, r]
            rbuf[pl.ds(r * nchunk, nchunk), :] = acc
            return 0

        lax.fori_loop(0, tm, body, 0, unroll=2)

    @pl.when(i == 0)
    def _():
        for cp in idx_copies(0, 0):
            cp.start()
        for cp in idx_copies(0, 0):
            cp.wait()
        issue_gathers(0)

        @pl.when(nsteps > 1)
        def _():
            for cp in idx_copies(1, 1):
                cp.start()

    for cur in range(2):
        nxt = 1 - cur

        @pl.when(i % 2 == cur)
        def _():
            @pl.when(i + 1 < nsteps)
            def _():
                for cp in idx_copies(i + 1, nxt):
                    cp.wait()
                issue_gathers(nxt)

            pltpu.make_async_copy(ybuf.at[cur], ybuf.at[cur], gsem.at[cur]).wait()
            combine(cur)

            @pl.when(i + 2 < nsteps)
            def _():
                for cp in idx_copies(i + 2, cur):
                    cp.start()

    routed = jnp.concatenate([rbuf[pl.ds(c, tm, stride=nchunk), :] for c in range(nchunk)], axis=1)

    x1 = x1_ref[...]
    xb = x1.astype(BF16)
    gate = _dot(xb, wsg_ref[...])
    up = _dot(xb, wsu_ref[...])
    shared = _dot(((gate * _sigmoid(gate)) * up).astype(BF16), wsd_ref[...])
    o_ref[...] = _layer_norm(alpha * x1 + (routed + shared), lg_ref[...], lb_ref[...])


def _final(ys, dst_t, w_t, x1, wsg, wsu, wsd, lg, lb, alpha, tm):
    t, d = x1.shape
    nchunk = d // LANES
    ds = wsg.shape[1]
    nsteps = t // tm
    full = lambda i: (0, 0)
    return pl.pallas_call(
        functools.partial(_final_kernel, tm=tm, nchunk=nchunk, alpha=alpha, nsteps=nsteps),
        grid=(nsteps,),
        in_specs=[pl.BlockSpec(memory_space=pl.ANY)] * 3 + [
            pl.BlockSpec((tm, d), lambda i: (i, 0)),
            pl.BlockSpec((d, ds), full), pl.BlockSpec((d, ds), full), pl.BlockSpec((ds, d), full),
            pl.BlockSpec((1, d), full), pl.BlockSpec((1, d), full)],
        out_specs=pl.BlockSpec((tm, d), lambda i: (i, 0)),
        out_shape=jax.ShapeDtypeStruct((t, d), F32),
        scratch_shapes=[pltpu.VMEM((2, tm * TOP_K * nchunk, LANES), F32),
                        pltpu.VMEM((tm * nchunk, LANES), F32),
                        pltpu.SMEM((2, TOP_K, tm), jnp.int32),
                        pltpu.SMEM((2, TOP_K, tm), F32),
                        pltpu.SemaphoreType.DMA((2,)),
                        pltpu.SemaphoreType.DMA((2,))],
        compiler_params=pltpu.CompilerParams(
            dimension_semantics=("arbitrary",), vmem_limit_bytes=VMEM_LIMIT),
        name="combine_shared_ln2",
    )(dst_t, w_t, ys, x1, wsg, wsu, wsd, lg, lb)


def _block_tables(counts, n_rows):
    mb = MOE_BLOCK
    n_exp = counts.shape[0]
    padded = (counts + mb - 1) // mb * mb
    pad_end = jnp.cumsum(padded)
    pad_start = pad_end - padded
    n_blocks = n_rows // mb
    first_row = jnp.arange(n_blocks, dtype=jnp.int32) * mb
    block_expert = jnp.minimum(
        jnp.sum((pad_end[None, :] <= first_row[:, None]).astype(jnp.int32), axis=1), n_exp - 1)
    nused = (pad_end[-1:] // mb).astype(jnp.int32)
    return pad_start.astype(jnp.int32), pad_end.astype(jnp.int32), block_expert, nused


def _tile(n, pref):
    while n % pref:
        pref //= 2
    return pref


def kernel(x, w_in, lower_bounds, hg_norm_g, w_branch_hg, w_branch_sb, w_out, ln1_g, ln1_b,
           w_router, router_bias, w_exp_gate, w_exp_up, w_exp_down,
           w_sh_gate, w_sh_up, w_sh_down, ln2_g, ln2_b):
    depth = w_in.shape[0]
    assert depth == 1, "single-layer block only"
    batch, seq, d = x.shape
    t = batch * seq
    n_exp = w_router.shape[-1]
    alpha = (2.0 * depth) ** 0.25
    nchunk = d // LANES

    lb = jnp.cumsum(jax.nn.softmax(lower_bounds.astype(F32), axis=0), axis=0)[0]
    xf = x.reshape(t, d)

    w = w_in[0]
    c = d
    w32 = jnp.concatenate([w[:, 0:2 * c], w[:, 3 * c:4 * c], w[:, 7 * c:9 * c]], axis=1).astype(BF16)
    wbf = jnp.concatenate([w[:, 2 * c:3 * c], w[:, 4 * c:7 * c]], axis=1).astype(BF16)
    tm = _tile(t, 1024)
    p32 = _matmul(xf, w32, F32, tm, 512)
    pbf = _matmul(xf, wbf, BF16, tm, 512)

    o_hg = _hgrn(p32, pbf, lb, hg_norm_g[0].astype(F32), batch, seq, _tile(seq, 256), 4)
    o_sb = _sb_attention(pbf, batch, seq, _tile(seq, 256), 2)

    x1, x1s = _merge(o_hg, o_sb, p32, xf,
                     w_branch_hg[0].astype(BF16), w_branch_sb[0].astype(BF16), w_out[0].astype(BF16),
                     ln1_g[0].reshape(1, d).astype(F32), ln1_b[0].reshape(1, d).astype(F32),
                     alpha, _tile(t, 512))

    idx_t, rank_t, w_t, cnt = _route(x1, w_router[0], router_bias[0], _tile(t, 256))
    p_rows = t * TOP_K + n_exp * MOE_BLOCK
    pstart, pend, block_expert, nused = _block_tables(cnt[:, 0].astype(jnp.int32), p_rows)
    dst_t = _dest_rows(idx_t, rank_t, pstart, nchunk, _tile(t, 256))
    xs = _dispatch(x1s, dst_t, pstart, pend, p_rows, _tile(t, 128))
    ys = _experts(xs, block_expert, nused, w_exp_gate[0], w_exp_up[0], w_exp_down[0])
    out = _final(ys, dst_t, w_t, x1,
                 w_sh_gate[0].astype(BF16), w_sh_up[0].astype(BF16), w_sh_down[0].astype(BF16),
                 ln2_g[0].reshape(1, d).astype(F32), ln2_b[0].reshape(1, d).astype(F32),
                 alpha, _tile(t, 128))
    return out.reshape(batch, seq, d)
```

```python
import functools

import jax
import jax.numpy as jnp
from jax import lax
from jax.experimental import pallas as pl
from jax.experimental.pallas import tpu as pltpu

F32 = jnp.float32
BF16 = jnp.bfloat16

LANES = 128
SUBLANES = 8
VMEM_LIMIT = 48 * 1024 * 1024

HG_HEADS = 8
HG_CHUNK = 32
SB_HEADS = 8
N_GROUPS = 8
TOPK_GROUPS = 4
TOP_K = 8
ROUTED_SCALE = 2.5
MOE_BLOCK = 256
LN_EPS = 1e-5
RMS_EPS = 1e-6
STICK_DEAD = 110.0

NT_DIMS = (((1,), (1,)), ((), ()))
TN_DIMS = (((0,), (0,)), ((), ()))


def _dot(a, b):
    return jnp.dot(a, b, preferred_element_type=F32)


def _sigmoid(x):
    return 1.0 / (1.0 + jnp.exp(-x))


def _split2(x):
    hi = x.astype(BF16)
    lo = (x - hi.astype(F32)).astype(BF16)
    return hi, lo


def _layer_norm(r, g, b):
    mu = jnp.mean(r, axis=-1, keepdims=True)
    d = r - mu
    var = jnp.mean(d * d, axis=-1, keepdims=True)
    return d * lax.rsqrt(var + LN_EPS) * g + b


def _mm_kernel(x_ref, w_ref, o_ref):
    o_ref[...] = _dot(x_ref[...].astype(BF16), w_ref[...]).astype(o_ref.dtype)


def _matmul(x, w, out_dtype, tm, tn):
    m, k = x.shape
    n = w.shape[1]
    return pl.pallas_call(
        _mm_kernel,
        grid=(m // tm, n // tn),
        in_specs=[pl.BlockSpec((tm, k), lambda i, j: (i, 0)),
                  pl.BlockSpec((k, tn), lambda i, j: (0, j))],
        out_specs=pl.BlockSpec((tm, tn), lambda i, j: (i, j)),
        out_shape=jax.ShapeDtypeStruct((m, n), out_dtype),
        compiler_params=pltpu.CompilerParams(
            dimension_semantics=("parallel", "arbitrary"), vmem_limit_bytes=VMEM_LIMIT),
        name="in_proj",
    )(x, w)


def _hgrn_kernel(x_ref, w_ref, lb_ref, g_ref, o_ref, st_ref, *, ts, chunk, heads):
    @pl.when(pl.program_id(1) == 0)
    def _():
        st_ref[...] = jnp.zeros_like(st_ref)

    shift = chunk.bit_length() - 1
    row = lax.broadcasted_iota(jnp.int32, (ts, ts), 0)
    col = lax.broadcasted_iota(jnp.int32, (ts, ts), 1)
    same = (row >> shift) == (col >> shift)
    causal = jnp.logical_and(same, col <= row)
    tri = jnp.where(causal, 1.0, 0.0).astype(BF16)
    ones = jnp.where(same, 1.0, 0.0).astype(BF16)
    d = LANES
    hd = heads * d

    xb = x_ref[...].astype(BF16)
    hq_all = _dot(xb, w_ref[:, 0:hd])
    hf_all = _dot(xb, w_ref[:, hd:2 * hd])
    hi_all = _dot(xb, w_ref[:, 2 * hd:3 * hd]).astype(BF16)
    hg_all = _dot(xb, w_ref[:, 3 * hd:4 * hd])

    for hh in range(heads):
        cols = slice(hh * d, (hh + 1) * d)
        lb = lb_ref[:, cols]
        hf = hf_all[:, cols]
        log_f = jnp.log(lb + (1.0 - lb) * _sigmoid(hf))
        k_in = (1.0 - lb) * _sigmoid(-hf)

        parts = jnp.concatenate(_split2(log_f), axis=1)
        cs = _dot(tri, parts)
        tot = _dot(ones, parts)
        b = cs[:, :d] + cs[:, d:]
        b_end = tot[:, :d] + tot[:, d:]

        q = hq_all[:, cols]
        q_start = (q * jnp.exp(b)).astype(BF16)
        q_end = (q * jnp.exp(b - b_end)).astype(BF16)
        k_end = (k_in * jnp.exp(b_end - b)).astype(BF16)
        v = hi_all[:, cols]

        scores = lax.dot_general(q_end, k_end, NT_DIMS, preferred_element_type=F32)
        scores = jnp.where(causal, scores, 0.0)
        o_intra = _dot(scores.astype(BF16), v)

        dec = jnp.exp(b_end)
        st = st_ref[hh]
        outs = []
        for c in range(ts // chunk):
            lo = c * chunk
            outs.append(lax.dot_general(q_start[lo:lo + chunk], st.astype(BF16), NT_DIMS,
                                        preferred_element_type=F32))
            kv = lax.dot_general(v[lo:lo + chunk], k_end[lo:lo + chunk], TN_DIMS,
                                 preferred_element_type=F32)
            st = dec[lo:lo + 1, :] * st + kv
        st_ref[hh] = st
        o = o_intra + jnp.concatenate(outs, axis=0)

        o = o * lax.rsqrt(jnp.mean(o * o, axis=-1, keepdims=True) + RMS_EPS)
        o = o * g_ref[:, cols]
        hg = hg_all[:, cols]
        o_ref[:, cols] = (o * (hg * _sigmoid(hg))).astype(o_ref.dtype)


def _hgrn(x, w_hg, lb, g, batch, seq, ts):
    t, dm = x.shape
    h = HG_HEADS
    hd = h * LANES
    ns = seq // ts
    full = lambda b, s: (0, 0)
    return pl.pallas_call(
        functools.partial(_hgrn_kernel, ts=ts, chunk=HG_CHUNK, heads=h),
        grid=(batch, ns),
        in_specs=[pl.BlockSpec((ts, dm), lambda b, s: (b * ns + s, 0)),
                  pl.BlockSpec((dm, 4 * hd), full),
                  pl.BlockSpec((1, hd), full),
                  pl.BlockSpec((1, hd), full)],
        out_specs=pl.BlockSpec((ts, hd), lambda b, s: (b * ns + s, 0)),
        out_shape=jax.ShapeDtypeStruct((t, hd), BF16),
        scratch_shapes=[pltpu.VMEM((h, LANES, LANES), F32)],
        compiler_params=pltpu.CompilerParams(
            dimension_semantics=("parallel", "arbitrary"), vmem_limit_bytes=VMEM_LIMIT),
        name="hgrn2",
    )(x, w_hg, lb.reshape(1, hd), g.reshape(1, hd))


def _sb_kernel(q_ref, k_ref, v_ref, u_ref, o_ref, *, tq, scale, hp):
    i = pl.program_id(2)
    u = u_ref[...]
    rep = tq // LANES
    d = LANES

    def block(hh, j, carry, acc, masked, live=None):
        cols = slice(hh * d, (hh + 1) * d)
        start = pl.multiple_of(j * tq, tq)
        q = q_ref[:, cols]
        kj = k_ref[pl.ds(start, tq), cols]
        vj = v_ref[pl.ds(start, tq), cols]
        z = lax.dot_general(q, kj, NT_DIMS, preferred_element_type=F32) * scale
        sp = jnp.maximum(z, 0.0) + jnp.log(1.0 + jnp.exp(-jnp.abs(z)))
        if masked:
            row = lax.broadcasted_iota(jnp.int32, (tq, tq), 0)
            col = lax.broadcasted_iota(jnp.int32, (tq, tq), 1)
            before = col < row
            drop = jnp.where(before, sp, 0.0)
        elif live is not None:
            drop = sp * live
        else:
            drop = sp
        hi = drop.astype(BF16)
        lo = (drop - hi.astype(F32)).astype(BF16)
        later = _dot(hi, u) + _dot(lo, u)
        stick = later + jnp.concatenate([carry] * rep, axis=1)
        w = jnp.exp((z - sp) - stick)
        if masked:
            w = jnp.where(before, w, 0.0)
        elif live is not None:
            w = w * live
        acc = acc + _dot(w.astype(BF16), vj)
        carry = carry + jnp.broadcast_to(later[:, 0:1] + drop[:, 0:1], (tq, LANES))
        return carry, acc

    def some_row_alive(carries):
        m = jnp.min(carries[0])
        for c in carries[1:]:
            m = jnp.minimum(m, jnp.min(c))
        return (m < STICK_DEAD).astype(jnp.int32)

    zero = jnp.zeros((tq, LANES), F32)
    live = (i > 0).astype(F32)
    prev = jnp.maximum(i - 1, 0)
    carries, accs = [], []
    for hh in range(hp):
        carry, acc = block(hh, i, zero, zero, True)
        carry, acc = block(hh, prev, carry, acc, False, live=live)
        carries.append(carry)
        accs.append(acc)

    def cond(state):
        return jnp.logical_and(state[0] >= 0, state[1] > 0)

    def body(state):
        j, _, carries, accs = state
        out = [block(hh, j, carries[hh], accs[hh], False) for hh in range(hp)]
        carries = tuple(o[0] for o in out)
        accs = tuple(o[1] for o in out)
        return j - 1, some_row_alive(carries), carries, accs

    _, _, carries, accs = lax.while_loop(
        cond, body, (i - 2, some_row_alive(carries), tuple(carries), tuple(accs)))
    for hh in range(hp):
        o_ref[:, hh * d:(hh + 1) * d] = accs[hh].astype(o_ref.dtype)


def _sb_attention(pbf, batch, seq, tq, hp):
    t = batch * seq
    h = SB_HEADS
    ng = h // hp
    w = hp * LANES
    nq = seq // tq
    r = jnp.arange(tq)
    u = (r[:, None] > r[None, :]).astype(BF16)
    return pl.pallas_call(
        functools.partial(_sb_kernel, tq=tq, scale=float(LANES) ** -0.5, hp=hp),
        grid=(batch, ng, nq),
        in_specs=[pl.BlockSpec((tq, w), lambda b, hg, i: (b * nq + i, hg)),
                  pl.BlockSpec((seq, w), lambda b, hg, i: (b, ng + hg)),
                  pl.BlockSpec((seq, w), lambda b, hg, i: (b, 2 * ng + hg)),
                  pl.BlockSpec((tq, tq), lambda b, hg, i: (0, 0))],
        out_specs=pl.BlockSpec((tq, w), lambda b, hg, i: (b * nq + i, hg)),
        out_shape=jax.ShapeDtypeStruct((t, h * LANES), BF16),
        compiler_params=pltpu.CompilerParams(
            dimension_semantics=("parallel", "parallel", "arbitrary"), vmem_limit_bytes=VMEM_LIMIT),
        name="stick_breaking",
    )(pbf, pbf, pbf, u)


def _merge_kernel(ohg_ref, osb_ref, x_ref, wgate_ref, wbh_ref, wbs_ref, wo_ref,
                  lg_ref, lb_ref, x1_ref, x1s_ref, *, alpha, tm, nchunk):
    x = x_ref[...]
    xb = x.astype(BF16)
    d = x.shape[1]
    g_hg = _dot(xb, wgate_ref[:, 0:d])
    g_sb = _dot(xb, wgate_ref[:, d:2 * d])
    y_hg = _dot(ohg_ref[...], wbh_ref[...])
    y_sb = _dot(osb_ref[...], wbs_ref[...])
    merged = _sigmoid(g_hg) * y_hg + _sigmoid(g_sb) * y_sb
    hmix = _dot(merged.astype(BF16), wo_ref[...])
    x1 = _layer_norm(alpha * x + hmix, lg_ref[...], lb_ref[...])
    x1_ref[...] = x1
    for c in range(nchunk):
        x1s_ref[pl.ds(c, tm, stride=nchunk), :] = x1[:, c * LANES:(c + 1) * LANES]


def _merge(o_hg, o_sb, x, wgate, wbh, wbs, wo, lg, lb, alpha, tm):
    t, d = x.shape
    nchunk = d // LANES
    row = lambda off: (lambda i: (i, off))
    full = lambda i: (0, 0)
    return pl.pallas_call(
        functools.partial(_merge_kernel, alpha=alpha, tm=tm, nchunk=nchunk),
        grid=(t // tm,),
        in_specs=[pl.BlockSpec((tm, d), row(0)), pl.BlockSpec((tm, d), row(0)),
                  pl.BlockSpec((tm, d), row(0)),
                  pl.BlockSpec((d, 2 * d), full),
                  pl.BlockSpec((d, d), full), pl.BlockSpec((d, d), full), pl.BlockSpec((d, d), full),
                  pl.BlockSpec((1, d), full), pl.BlockSpec((1, d), full)],
        out_specs=[pl.BlockSpec((tm, d), row(0)), pl.BlockSpec((tm * nchunk, LANES), row(0))],
        out_shape=[jax.ShapeDtypeStruct((t, d), F32), jax.ShapeDtypeStruct((t * nchunk, LANES), F32)],
        compiler_params=pltpu.CompilerParams(
            dimension_semantics=("parallel",), vmem_limit_bytes=VMEM_LIMIT),
        name="merge_ln1",
    )(o_hg, o_sb, x, wgate, wbh, wbs, wo, lg, lb)


def _route_kernel(x_ref, wh_ref, wl_ref, bias_ref, u_ref, ones_ref,
                  idx_ref, rank_ref, w_ref, cnt_ref, carry, *, tile, n_exp):
    @pl.when(pl.program_id(0) == 0)
    def _():
        carry[...] = jnp.zeros_like(carry)

    x = x_ref[...]
    xh = x.astype(BF16)
    xl = (x - xh.astype(F32)).astype(BF16)
    wh = wh_ref[...]
    nt = functools.partial(lax.dot_general, dimension_numbers=NT_DIMS, preferred_element_type=F32)
    logits = nt(wh, xh) + nt(wh, xl) + nt(wl_ref[...], xh)
    scores = _sigmoid(logits)
    biased = scores + bias_ref[...]

    neg_inf = -jnp.inf
    gsz = n_exp // N_GROUPS
    groups = [biased[g * gsz:(g + 1) * gsz] for g in range(N_GROUPS)]
    gscore = []
    for xg in groups:
        m1 = jnp.max(xg, axis=0, keepdims=True)
        n1 = jnp.sum(jnp.where(xg == m1, 1.0, 0.0), axis=0, keepdims=True)
        m2 = jnp.max(jnp.where(xg < m1, xg, neg_inf), axis=0, keepdims=True)
        gscore.append(m1 + jnp.where(n1 >= 2.0, m1, m2))
    masked = []
    for g in range(N_GROUPS):
        beaten_by = jnp.zeros_like(gscore[g])
        for h in range(N_GROUPS):
            if h < g:
                beaten_by = beaten_by + jnp.where(gscore[h] >= gscore[g], 1.0, 0.0)
            elif h > g:
                beaten_by = beaten_by + jnp.where(gscore[h] > gscore[g], 1.0, 0.0)
        keep = jnp.broadcast_to(beaten_by, (gsz, tile)) < float(TOPK_GROUPS)
        masked.append(jnp.where(keep, groups[g], jnp.finfo(F32).min))
    cur = jnp.concatenate(masked, axis=0)

    eid = lax.broadcasted_iota(jnp.int32, (n_exp, tile), 0).astype(F32)
    chosen = jnp.zeros((n_exp, tile), F32)
    idxs, ws = [], []
    for _ in range(TOP_K):
        m = jnp.max(cur, axis=0, keepdims=True)
        ik = jnp.min(jnp.where(cur == m, eid, float(n_exp)), axis=0, keepdims=True)
        sel = eid == ik
        idxs.append(ik)
        ws.append(jnp.sum(jnp.where(sel, scores, 0.0), axis=0, keepdims=True))
        chosen = chosen + jnp.where(sel, 1.0, 0.0)
        cur = jnp.where(sel, neg_inf, cur)

    chosen_b = chosen.astype(BF16)
    run = carry[...]
    rank = _dot(chosen_b, u_ref[...]) + jnp.concatenate([run] * (tile // LANES), axis=1)
    ranks = [jnp.sum(jnp.where(eid == ik, rank, 0.0), axis=0, keepdims=True) for ik in idxs]
    run = run + _dot(chosen_b, ones_ref[...])
    carry[...] = run
    cnt_ref[...] = run

    wsum = ws[0]
    for wk in ws[1:]:
        wsum = wsum + wk
    idx_ref[...] = jnp.concatenate(idxs, axis=0).astype(jnp.int32)
    rank_ref[...] = jnp.concatenate(ranks, axis=0).astype(jnp.int32)
    w_ref[...] = jnp.concatenate([wk / wsum * ROUTED_SCALE for wk in ws], axis=0)


def _route(x1, w_router, router_bias, tile):
    t, d = x1.shape
    n_exp = w_router.shape[1]
    wt = w_router.astype(F32).T
    wh = wt.astype(BF16)
    wl = (wt - wh.astype(F32)).astype(BF16)
    r = jnp.arange(tile)
    u = (r[:, None] < r[None, :]).astype(BF16)
    ones = jnp.ones((tile, LANES), BF16)
    full = lambda i: (0, 0)
    tok = lambda i: (0, i)
    return pl.pallas_call(
        functools.partial(_route_kernel, tile=tile, n_exp=n_exp),
        grid=(t // tile,),
        in_specs=[pl.BlockSpec((tile, d), lambda i: (i, 0)),
                  pl.BlockSpec((n_exp, d), full), pl.BlockSpec((n_exp, d), full),
                  pl.BlockSpec((n_exp, 1), full),
                  pl.BlockSpec((tile, tile), full), pl.BlockSpec((tile, LANES), full)],
        out_specs=[pl.BlockSpec((TOP_K, tile), tok), pl.BlockSpec((TOP_K, tile), tok),
                   pl.BlockSpec((TOP_K, tile), tok), pl.BlockSpec((n_exp, LANES), full)],
        out_shape=[jax.ShapeDtypeStruct((TOP_K, t), jnp.int32), jax.ShapeDtypeStruct((TOP_K, t), jnp.int32),
                   jax.ShapeDtypeStruct((TOP_K, t), F32), jax.ShapeDtypeStruct((n_exp, LANES), F32)],
        scratch_shapes=[pltpu.VMEM((n_exp, LANES), F32)],
        compiler_params=pltpu.CompilerParams(
            dimension_semantics=("arbitrary",), vmem_limit_bytes=VMEM_LIMIT),
        name="router_topk",
    )(x1, wh, wl, router_bias.astype(F32).reshape(n_exp, 1), u, ones)


def _dest_kernel(idx_ref, rank_ref, pstart_ref, dst_ref, *, tile, n_exp, nchunk):
    eid = lax.broadcasted_iota(jnp.int32, (n_exp, tile), 0)
    pstart = jnp.broadcast_to(pstart_ref[...], (n_exp, tile))
    rows = []
    for k in range(TOP_K):
        hit = eid == idx_ref[k:k + 1, :]
        base = jnp.sum(jnp.where(hit, pstart, 0.0), axis=0, keepdims=True)
        rows.append((base.astype(jnp.int32) + rank_ref[k:k + 1, :]) * nchunk)
    dst_ref[...] = jnp.concatenate(rows, axis=0)


def _dest_rows(idx_t, rank_t, pstart, nchunk, tile):
    t = idx_t.shape[1]
    n_exp = pstart.shape[0]
    tok = lambda i: (0, i)
    return pl.pallas_call(
        functools.partial(_dest_kernel, tile=tile, n_exp=n_exp, nchunk=nchunk),
        grid=(t // tile,),
        in_specs=[pl.BlockSpec((TOP_K, tile), tok), pl.BlockSpec((TOP_K, tile), tok),
                  pl.BlockSpec((n_exp, 1), lambda i: (0, 0))],
        out_specs=pl.BlockSpec((TOP_K, tile), tok),
        out_shape=jax.ShapeDtypeStruct((TOP_K, t), jnp.int32),
        compiler_params=pltpu.CompilerParams(
            dimension_semantics=("parallel",), vmem_limit_bytes=VMEM_LIMIT),
        name="dest_rows",
    )(idx_t, rank_t, pstart.astype(F32).reshape(n_exp, 1))


def _dispatch_kernel(pstart_ref, pend_ref, dst_hbm, x_hbm, xs_hbm,
                     dst_s, xbuf, zbuf, isem, lsem, dsem, zsem,
                     *, td, nchunk, mb, n_exp, nsteps, n_blocks):
    i = pl.program_id(0)
    rows = mb * nchunk
    trows = td * nchunk

    def idx_copy(step, slot):
        return pltpu.make_async_copy(dst_hbm.at[:, pl.ds(step * td, td)], dst_s.at[slot], isem.at[slot])

    def tile_load(step, slot):
        start = pl.multiple_of(step * trows, trows)
        return pltpu.make_async_copy(x_hbm.at[pl.ds(start, trows), :], xbuf.at[slot], lsem.at[slot])

    def wait_rows(slot):
        for _ in range(TOP_K):
            pltpu.make_async_copy(xbuf.at[slot], xbuf.at[slot], dsem.at[slot]).wait()

    @pl.when(i == 0)
    def _():
        idx_copy(0, 0).start()
        tile_load(0, 0).start()
        zbuf[...] = jnp.zeros_like(zbuf)

        def zero_copy(e):
            start = pl.multiple_of((pend_ref[e] - mb) * nchunk, nchunk)
            return pltpu.make_async_copy(zbuf, xs_hbm.at[pl.ds(start, rows), :], zsem)

        def zstart(e, _):
            @pl.when(pend_ref[e] > pstart_ref[e])
            def _():
                zero_copy(e).start()
            return 0

        def zwait(e, _):
            @pl.when(pend_ref[e] > pstart_ref[e])
            def _():
                zero_copy(e).wait()
            return 0

        lax.fori_loop(0, n_exp, zstart, 0)
        lax.fori_loop(0, n_exp, zwait, 0)

        def tail_copy(b):
            start = pl.multiple_of(b * rows, rows)
            return pltpu.make_async_copy(zbuf, xs_hbm.at[pl.ds(start, rows), :], zsem)

        def tstart(b, _):
            tail_copy(b).start()
            return 0

        def twait(b, _):
            tail_copy(b).wait()
            return 0

        first_unused = pend_ref[n_exp - 1] // mb
        lax.fori_loop(first_unused, n_blocks, tstart, 0)
        lax.fori_loop(first_unused, n_blocks, twait, 0)

    for cur in range(3):
        nxt = (cur + 1) % 3

        @pl.when(i % 3 == cur)
        def _():
            idx_copy(i, cur).wait()

            @pl.when(i >= 2)
            def _():
                wait_rows(nxt)

            @pl.when(i + 1 < nsteps)
            def _():
                idx_copy(i + 1, nxt).start()
                tile_load(i + 1, nxt).start()

            tile_load(i, cur).wait()

            def body(r, _):
                src = pl.multiple_of(r * nchunk, nchunk)
                for k in range(TOP_K):
                    dst = pl.multiple_of(dst_s[cur, k, r], nchunk)
                    pltpu.make_async_copy(xbuf.at[cur, pl.ds(src, nchunk), :],
                                          xs_hbm.at[pl.ds(dst, nchunk), :], dsem.at[cur]).start(priority=k % 2)
                return 0

            lax.fori_loop(0, td, body, 0, unroll=2)

            @pl.when(i == nsteps - 1)
            def _():
                @pl.when(i >= 1)
                def _():
                    wait_rows((cur + 2) % 3)

                wait_rows(cur)


def _dispatch(x2, dst_t, pstart, pend, p_rows, td):
    n_exp = pstart.shape[0]
    t = dst_t.shape[1]
    nchunk = x2.shape[0] // t
    nsteps = t // td
    grid_spec = pltpu.PrefetchScalarGridSpec(
        num_scalar_prefetch=2,
        grid=(nsteps,),
        in_specs=[pl.BlockSpec(memory_space=pl.ANY)] * 2,
        out_specs=pl.BlockSpec(memory_space=pl.ANY),
        scratch_shapes=[pltpu.SMEM((3, TOP_K, td), jnp.int32),
                        pltpu.VMEM((3, td * nchunk, LANES), F32),
                        pltpu.VMEM((MOE_BLOCK * nchunk, LANES), F32),
                        pltpu.SemaphoreType.DMA((3,)),
                        pltpu.SemaphoreType.DMA((3,)),
                        pltpu.SemaphoreType.DMA((3,)),
                        pltpu.SemaphoreType.DMA],
    )
    return pl.pallas_call(
        functools.partial(_dispatch_kernel, td=td, nchunk=nchunk, mb=MOE_BLOCK, n_exp=n_exp, nsteps=nsteps,
                          n_blocks=p_rows // MOE_BLOCK),
        grid_spec=grid_spec,
        out_shape=jax.ShapeDtypeStruct((p_rows * nchunk, LANES), F32),
        compiler_params=pltpu.CompilerParams(
            dimension_semantics=("arbitrary",), vmem_limit_bytes=VMEM_LIMIT),
        name="moe_dispatch",
    )(pstart, pend, dst_t, x2)


def _experts_kernel(bexp_ref, nused_ref, xs_ref, wg_ref, wu_ref, wd_ref, ys_ref, wg_b, wu_b, wd_b,
                    *, mb, nchunk):
    i = pl.program_id(0)

    @pl.when(i < nused_ref[0])
    def _():
        new_expert = jnp.logical_or(i == 0, bexp_ref[i] != bexp_ref[jnp.maximum(i - 1, 0)])

        @pl.when(new_expert)
        def _():
            wg_b[...] = wg_ref[0].astype(BF16)
            wu_b[...] = wu_ref[0].astype(BF16)
            wd_b[...] = wd_ref[0].astype(BF16)

        xs = [xs_ref[pl.ds(c, mb, stride=nchunk), :] for c in range(nchunk)]
        xb = jnp.concatenate(xs, axis=1).astype(BF16)
        gate = _dot(xb, wg_b[...])
        up = _dot(xb, wu_b[...])
        hid = (gate * _sigmoid(gate)) * up
        y = _dot(hid.astype(BF16), wd_b[...])
        for c in range(nchunk):
            ys_ref[pl.ds(c, mb, stride=nchunk), :] = y[:, c * LANES:(c + 1) * LANES]

    @pl.when(i >= nused_ref[0])
    def _():
        ys_ref[...] = jnp.zeros_like(ys_ref)


def _experts(xs, block_expert, nused, w_gate, w_up, w_down):
    e, d, de = w_gate.shape
    nchunk = d // LANES
    mb = MOE_BLOCK
    rows = mb * nchunk
    n_blocks = xs.shape[0] // rows
    blk = lambda i, be, nu: (jnp.minimum(i, nu[0] - 1), 0)
    wsel = lambda i, be, nu: (be[i], 0, 0)
    grid_spec = pltpu.PrefetchScalarGridSpec(
        num_scalar_prefetch=2,
        grid=(n_blocks,),
        in_specs=[pl.BlockSpec((rows, LANES), blk),
                  pl.BlockSpec((1, d, de), wsel), pl.BlockSpec((1, d, de), wsel),
                  pl.BlockSpec((1, de, d), wsel)],
        out_specs=pl.BlockSpec((rows, LANES), lambda i, be, nu: (i, 0)),
        scratch_shapes=[pltpu.VMEM((d, de), BF16), pltpu.VMEM((d, de), BF16), pltpu.VMEM((de, d), BF16)],
    )
    return pl.pallas_call(
        functools.partial(_experts_kernel, mb=mb, nchunk=nchunk),
        grid_spec=grid_spec,
        out_shape=jax.ShapeDtypeStruct(xs.shape, F32),
        compiler_params=pltpu.CompilerParams(
            dimension_semantics=("arbitrary",), vmem_limit_bytes=VMEM_LIMIT),
        name="routed_experts",
    )(block_expert, nused, xs, w_gate, w_up, w_down)


def _final_kernel(dst_hbm, w_hbm, ys_hbm, x1_ref, wsg_ref, wsu_ref, wsd_ref,
                  lg_ref, lb_ref, o_ref, ybuf, rbuf, dst_s, w_s, isem, gsem,
                  *, tm, nchunk, alpha, nsteps):
    i = pl.program_id(0)
    tok_rows = TOP_K * nchunk

    def idx_copies(step, slot):
        cols = pl.ds(step * tm, tm)
        return (pltpu.make_async_copy(dst_hbm.at[:, cols], dst_s.at[slot], isem.at[slot]),
                pltpu.make_async_copy(w_hbm.at[:, cols], w_s.at[slot], isem.at[slot]))

    def issue_gathers(slot):
        def body(r, _):
            for k in range(TOP_K):
                src = pl.multiple_of(dst_s[slot, k, r], nchunk)
                pltpu.make_async_copy(ys_hbm.at[pl.ds(src, nchunk), :],
                                      ybuf.at[slot, pl.ds(r * tok_rows + k * nchunk, nchunk), :],
                                      gsem.at[slot]).start(priority=k % 2)
            return 0

        lax.fori_loop(0, tm, body, 0, unroll=2)

    def combine(slot):
        def body(r, _):
            acc = ybuf[slot, pl.ds(r * tok_rows, nchunk), :] * w_s[slot, 0, r]
            for k in range(1, TOP_K):
                acc = acc + ybuf[slot, pl.ds(r * tok_rows + k * nchunk, nchunk), :] * w_s[slot, k, r]
            rbuf[pl.ds(r * nchunk, nchunk), :] = acc
            return 0

        lax.fori_loop(0, tm, body, 0, unroll=2)

    @pl.when(i == 0)
    def _():
        for cp in idx_copies(0, 0):
            cp.start()
        for cp in idx_copies(0, 0):
            cp.wait()
        issue_gathers(0)

        @pl.when(nsteps > 1)
        def _():
            for cp in idx_copies(1, 1):
                cp.start()

    for cur in range(2):
        nxt = 1 - cur

        @pl.when(i % 2 == cur)
        def _():
            @pl.when(i + 1 < nsteps)
            def _():
                for cp in idx_copies(i + 1, nxt):
                    cp.wait()
                issue_gathers(nxt)

            pltpu.make_async_copy(ybuf.at[cur], ybuf.at[cur], gsem.at[cur]).wait()
            combine(cur)

            @pl.when(i + 2 < nsteps)
            def _():
                for cp in idx_copies(i + 2, cur):
                    cp.start()

    routed = jnp.concatenate([rbuf[pl.ds(c, tm, stride=nchunk), :] for c in range(nchunk)], axis=1)

    x1 = x1_ref[...]
    xb = x1.astype(BF16)
    gate = _dot(xb, wsg_ref[...])
    up = _dot(xb, wsu_ref[...])
    shared = _dot(((gate * _sigmoid(gate)) * up).astype(BF16), wsd_ref[...])
    o_ref[...] = _layer_norm(alpha * x1 + (routed + shared), lg_ref[...], lb_ref[...])


def _final(ys, dst_t, w_t, x1, wsg, wsu, wsd, lg, lb, alpha, tm):
    t, d = x1.shape
    nchunk = d // LANES
    ds = wsg.shape[1]
    nsteps = t // tm
    full = lambda i: (0, 0)
    return pl.pallas_call(
        functools.partial(_final_kernel, tm=tm, nchunk=nchunk, alpha=alpha, nsteps=nsteps),
        grid=(nsteps,),
        in_specs=[pl.BlockSpec(memory_space=pl.ANY)] * 3 + [
            pl.BlockSpec((tm, d), lambda i: (i, 0)),
            pl.BlockSpec((d, ds), full), pl.BlockSpec((d, ds), full), pl.BlockSpec((ds, d), full),
            pl.BlockSpec((1, d), full), pl.BlockSpec((1, d), full)],
        out_specs=pl.BlockSpec((tm, d), lambda i: (i, 0)),
        out_shape=jax.ShapeDtypeStruct((t, d), F32),
        scratch_shapes=[pltpu.VMEM((2, tm * TOP_K * nchunk, LANES), F32),
                        pltpu.VMEM((tm * nchunk, LANES), F32),
                        pltpu.SMEM((2, TOP_K, tm), jnp.int32),
                        pltpu.SMEM((2, TOP_K, tm), F32),
                        pltpu.SemaphoreType.DMA((2,)),
                        pltpu.SemaphoreType.DMA((2,))],
        compiler_params=pltpu.CompilerParams(
            dimension_semantics=("arbitrary",), vmem_limit_bytes=VMEM_LIMIT),
        name="combine_shared_ln2",
    )(dst_t, w_t, ys, x1, wsg, wsu, wsd, lg, lb)


def _block_tables(counts, n_rows):
    mb = MOE_BLOCK
    n_exp = counts.shape[0]
    padded = (counts + mb - 1) // mb * mb
    pad_end = jnp.cumsum(padded)
    pad_start = pad_end - padded
    n_blocks = n_rows // mb
    first_row = jnp.arange(n_blocks, dtype=jnp.int32) * mb
    block_expert = jnp.minimum(
        jnp.sum((pad_end[None, :] <= first_row[:, None]).astype(jnp.int32), axis=1), n_exp - 1)
    nused = (pad_end[-1:] // mb).astype(jnp.int32)
    return pad_start.astype(jnp.int32), pad_end.astype(jnp.int32), block_expert, nused


def _tile(n, pref):
    while n % pref:
        pref //= 2
    return pref


def kernel(x, w_in, lower_bounds, hg_norm_g, w_branch_hg, w_branch_sb, w_out, ln1_g, ln1_b,
           w_router, router_bias, w_exp_gate, w_exp_up, w_exp_down,
           w_sh_gate, w_sh_up, w_sh_down, ln2_g, ln2_b):
    depth = w_in.shape[0]
    assert depth == 1, "single-layer block only"
    batch, seq, d = x.shape
    t = batch * seq
    n_exp = w_router.shape[-1]
    alpha = (2.0 * depth) ** 0.25
    nchunk = d // LANES

    lb = jnp.cumsum(jax.nn.softmax(lower_bounds.astype(F32), axis=0), axis=0)[0]
    xf = x.reshape(t, d)

    w = w_in[0].astype(BF16)
    c = d
    o_hg = _hgrn(xf, w[:, 0:4 * c], lb, hg_norm_g[0].astype(F32), batch, seq, _tile(seq, 256))
    pbf = _matmul(xf, w[:, 4 * c:7 * c], BF16, _tile(t, 1024), 512)
    o_sb = _sb_attention(pbf, batch, seq, _tile(seq, 256), 2)

    x1, x1s = _merge(o_hg, o_sb, xf, w[:, 7 * c:9 * c],
                     w_branch_hg[0].astype(BF16), w_branch_sb[0].astype(BF16), w_out[0].astype(BF16),
                     ln1_g[0].reshape(1, d).astype(F32), ln1_b[0].reshape(1, d).astype(F32),
                     alpha, _tile(t, 512))

    idx_t, rank_t, w_t, cnt = _route(x1, w_router[0], router_bias[0], _tile(t, 256))
    p_rows = t * TOP_K + n_exp * MOE_BLOCK
    pstart, pend, block_expert, nused = _block_tables(cnt[:, 0].astype(jnp.int32), p_rows)
    dst_t = _dest_rows(idx_t, rank_t, pstart, nchunk, _tile(t, 256))
    xs = _dispatch(x1s, dst_t, pstart, pend, p_rows, _tile(t, 128))
    ys = _experts(xs, block_expert, nused, w_exp_gate[0], w_exp_up[0], w_exp_down[0])
    out = _final(ys, dst_t, w_t, x1,
                 w_sh_gate[0].astype(BF16), w_sh_up[0].astype(BF16), w_sh_down[0].astype(BF16),
                 ln2_g[0].reshape(1, d).astype(F32), ln2_b[0].reshape(1, d).astype(F32),
                 alpha, _tile(t, 128))
    return out.reshape(batch, seq, d)
```

```python
import functools

import jax
import jax.numpy as jnp
from jax import lax
from jax.experimental import pallas as pl
from jax.experimental.pallas import tpu as pltpu

F32 = jnp.float32
BF16 = jnp.bfloat16

LANES = 128
SUBLANES = 8
VMEM_LIMIT = 48 * 1024 * 1024

HG_HEADS = 8
HG_CHUNK = 32
SB_HEADS = 8
N_GROUPS = 8
TOPK_GROUPS = 4
TOP_K = 8
ROUTED_SCALE = 2.5
MOE_BLOCK = 256
LN_EPS = 1e-5
RMS_EPS = 1e-6
STICK_DEAD = 110.0

NT_DIMS = (((1,), (1,)), ((), ()))
TN_DIMS = (((0,), (0,)), ((), ()))


def _dot(a, b):
    return jnp.dot(a, b, preferred_element_type=F32)


def _sigmoid(x):
    return 1.0 / (1.0 + jnp.exp(-x))


def _split2(x):
    hi = x.astype(BF16)
    lo = (x - hi.astype(F32)).astype(BF16)
    return hi, lo


def _layer_norm(r, g, b):
    mu = jnp.mean(r, axis=-1, keepdims=True)
    d = r - mu
    var = jnp.mean(d * d, axis=-1, keepdims=True)
    return d * lax.rsqrt(var + LN_EPS) * g + b


def _mm_kernel(x_ref, w_ref, o_ref):
    o_ref[...] = _dot(x_ref[...].astype(BF16), w_ref[...]).astype(o_ref.dtype)


def _matmul(x, w, out_dtype, tm, tn):
    m, k = x.shape
    n = w.shape[1]
    return pl.pallas_call(
        _mm_kernel,
        grid=(m // tm, n // tn),
        in_specs=[pl.BlockSpec((tm, k), lambda i, j: (i, 0)),
                  pl.BlockSpec((k, tn), lambda i, j: (0, j))],
        out_specs=pl.BlockSpec((tm, tn), lambda i, j: (i, j)),
        out_shape=jax.ShapeDtypeStruct((m, n), out_dtype),
        compiler_params=pltpu.CompilerParams(
            dimension_semantics=("parallel", "arbitrary"), vmem_limit_bytes=VMEM_LIMIT),
        name="in_proj",
    )(x, w)


def _hgrn_kernel(x_ref, w_ref, lb_ref, g_ref, o_ref, st_ref, *, ts, chunk, heads):
    @pl.when(pl.program_id(1) == 0)
    def _():
        st_ref[...] = jnp.zeros_like(st_ref)

    shift = chunk.bit_length() - 1
    row = lax.broadcasted_iota(jnp.int32, (ts, ts), 0)
    col = lax.broadcasted_iota(jnp.int32, (ts, ts), 1)
    same = (row >> shift) == (col >> shift)
    causal = jnp.logical_and(same, col <= row)
    tri = jnp.where(causal, 1.0, 0.0).astype(BF16)
    ones = jnp.where(same, 1.0, 0.0).astype(BF16)
    d = LANES
    hd = heads * d

    xb = x_ref[...].astype(BF16)
    hq_all = _dot(xb, w_ref[:, 0:hd])
    hf_all = _dot(xb, w_ref[:, hd:2 * hd])
    hi_all = _dot(xb, w_ref[:, 2 * hd:3 * hd]).astype(BF16)
    hg_all = _dot(xb, w_ref[:, 3 * hd:4 * hd])

    for hh in range(heads):
        cols = slice(hh * d, (hh + 1) * d)
        lb = lb_ref[:, cols]
        hf = hf_all[:, cols]
        log_f = jnp.log(lb + (1.0 - lb) * _sigmoid(hf))
        k_in = (1.0 - lb) * _sigmoid(-hf)

        parts = jnp.concatenate(_split2(log_f), axis=1)
        cs = _dot(tri, parts)
        tot = _dot(ones, parts)
        b = cs[:, :d] + cs[:, d:]
        b_end = tot[:, :d] + tot[:, d:]

        q = hq_all[:, cols]
        q_start = (q * jnp.exp(b)).astype(BF16)
        q_end = (q * jnp.exp(b - b_end)).astype(BF16)
        k_end = (k_in * jnp.exp(b_end - b)).astype(BF16)
        v = hi_all[:, cols]

        scores = lax.dot_general(q_end, k_end, NT_DIMS, preferred_element_type=F32)
        scores = jnp.where(causal, scores, 0.0)
        o_intra = _dot(scores.astype(BF16), v)

        dec = jnp.exp(b_end)
        st = st_ref[hh]
        outs = []
        for c in range(ts // chunk):
            lo = c * chunk
            outs.append(lax.dot_general(q_start[lo:lo + chunk], st.astype(BF16), NT_DIMS,
                                        preferred_element_type=F32))
            kv = lax.dot_general(v[lo:lo + chunk], k_end[lo:lo + chunk], TN_DIMS,
                                 preferred_element_type=F32)
            st = dec[lo:lo + 1, :] * st + kv
        st_ref[hh] = st
        o = o_intra + jnp.concatenate(outs, axis=0)

        o = o * lax.rsqrt(jnp.mean(o * o, axis=-1, keepdims=True) + RMS_EPS)
        o = o * g_ref[:, cols]
        hg = hg_all[:, cols]
        o_ref[:, cols] = (o * (hg * _sigmoid(hg))).astype(o_ref.dtype)


def _hgrn(x, w_hg, lb, g, batch, seq, ts):
    t, dm = x.shape
    h = HG_HEADS
    hd = h * LANES
    ns = seq // ts
    full = lambda b, s: (0, 0)
    return pl.pallas_call(
        functools.partial(_hgrn_kernel, ts=ts, chunk=HG_CHUNK, heads=h),
        grid=(batch, ns),
        in_specs=[pl.BlockSpec((ts, dm), lambda b, s: (b * ns + s, 0)),
                  pl.BlockSpec((dm, 4 * hd), full),
                  pl.BlockSpec((1, hd), full),
                  pl.BlockSpec((1, hd), full)],
        out_specs=pl.BlockSpec((ts, hd), lambda b, s: (b * ns + s, 0)),
        out_shape=jax.ShapeDtypeStruct((t, hd), BF16),
        scratch_shapes=[pltpu.VMEM((h, LANES, LANES), F32)],
        compiler_params=pltpu.CompilerParams(
            dimension_semantics=("parallel", "arbitrary"), vmem_limit_bytes=VMEM_LIMIT),
        name="hgrn2",
    )(x, w_hg, lb.reshape(1, hd), g.reshape(1, hd))


def _sb_kernel(q_ref, k_ref, v_ref, u_ref, o_ref, *, tq, scale, hp):
    i = pl.program_id(2)
    u = u_ref[...]
    rep = tq // LANES
    d = LANES

    def block(hh, j, carry, acc, masked, live=None):
        cols = slice(hh * d, (hh + 1) * d)
        start = pl.multiple_of(j * tq, tq)
        q = q_ref[:, cols]
        kj = k_ref[pl.ds(start, tq), cols]
        vj = v_ref[pl.ds(start, tq), cols]
        z = lax.dot_general(q, kj, NT_DIMS, preferred_element_type=F32) * scale
        sp = jnp.maximum(z, 0.0) + jnp.log(1.0 + jnp.exp(-jnp.abs(z)))
        if masked:
            row = lax.broadcasted_iota(jnp.int32, (tq, tq), 0)
            col = lax.broadcasted_iota(jnp.int32, (tq, tq), 1)
            before = col < row
            drop = jnp.where(before, sp, 0.0)
        elif live is not None:
            drop = sp * live
        else:
            drop = sp
        hi = drop.astype(BF16)
        lo = (drop - hi.astype(F32)).astype(BF16)
        later = _dot(hi, u) + _dot(lo, u)
        stick = later + jnp.concatenate([carry] * rep, axis=1)
        w = jnp.exp((z - sp) - stick)
        if masked:
            w = jnp.where(before, w, 0.0)
        elif live is not None:
            w = w * live
        acc = acc + _dot(w.astype(BF16), vj)
        carry = carry + jnp.broadcast_to(later[:, 0:1] + drop[:, 0:1], (tq, LANES))
        return carry, acc

    def some_row_alive(carries):
        m = jnp.min(carries[0])
        for c in carries[1:]:
            m = jnp.minimum(m, jnp.min(c))
        return (m < STICK_DEAD).astype(jnp.int32)

    zero = jnp.zeros((tq, LANES), F32)
    live = (i > 0).astype(F32)
    prev = jnp.maximum(i - 1, 0)
    carries, accs = [], []
    for hh in range(hp):
        carry, acc = block(hh, i, zero, zero, True)
        carry, acc = block(hh, prev, carry, acc, False, live=live)
        carries.append(carry)
        accs.append(acc)

    def cond(state):
        return jnp.logical_and(state[0] >= 0, state[1] > 0)

    def body(state):
        j, _, carries, accs = state
        out = [block(hh, j, carries[hh], accs[hh], False) for hh in range(hp)]
        carries = tuple(o[0] for o in out)
        accs = tuple(o[1] for o in out)
        return j - 1, some_row_alive(carries), carries, accs

    _, _, carries, accs = lax.while_loop(
        cond, body, (i - 2, some_row_alive(carries), tuple(carries), tuple(accs)))
    for hh in range(hp):
        o_ref[:, hh * d:(hh + 1) * d] = accs[hh].astype(o_ref.dtype)


def _sb_attention(pbf, batch, seq, tq, hp):
    t = batch * seq
    h = SB_HEADS
    ng = h // hp
    w = hp * LANES
    nq = seq // tq
    r = jnp.arange(tq)
    u = (r[:, None] > r[None, :]).astype(BF16)
    return pl.pallas_call(
        functools.partial(_sb_kernel, tq=tq, scale=float(LANES) ** -0.5, hp=hp),
        grid=(batch, ng, nq),
        in_specs=[pl.BlockSpec((tq, w), lambda b, hg, i: (b * nq + i, hg)),
                  pl.BlockSpec((seq, w), lambda b, hg, i: (b, ng + hg)),
                  pl.BlockSpec((seq, w), lambda b, hg, i: (b, 2 * ng + hg)),
                  pl.BlockSpec((tq, tq), lambda b, hg, i: (0, 0))],
        out_specs=pl.BlockSpec((tq, w), lambda b, hg, i: (b * nq + i, hg)),
        out_shape=jax.ShapeDtypeStruct((t, h * LANES), BF16),
        compiler_params=pltpu.CompilerParams(
            dimension_semantics=("parallel", "parallel", "arbitrary"), vmem_limit_bytes=VMEM_LIMIT),
        name="stick_breaking",
    )(pbf, pbf, pbf, u)


def _merge_kernel(ohg_ref, osb_ref, x_ref, wgate_ref, wbh_ref, wbs_ref, wo_ref,
                  lg_ref, lb_ref, x1_ref, x1s_ref, *, alpha, tm, nchunk):
    x = x_ref[...]
    xb = x.astype(BF16)
    d = x.shape[1]
    g_hg = _dot(xb, wgate_ref[:, 0:d])
    g_sb = _dot(xb, wgate_ref[:, d:2 * d])
    y_hg = _dot(ohg_ref[...], wbh_ref[...])
    y_sb = _dot(osb_ref[...], wbs_ref[...])
    merged = _sigmoid(g_hg) * y_hg + _sigmoid(g_sb) * y_sb
    hmix = _dot(merged.astype(BF16), wo_ref[...])
    x1 = _layer_norm(alpha * x + hmix, lg_ref[...], lb_ref[...])
    x1_ref[...] = x1
    half = nchunk // 2
    for c in range(half):
        lo = x1[:, c * LANES:(c + 1) * LANES].astype(BF16).astype(F32)
        hi = x1[:, (c + half) * LANES:(c + half + 1) * LANES].astype(BF16).astype(F32)
        word = pltpu.bitcast(hi, jnp.uint32) | (pltpu.bitcast(lo, jnp.uint32) >> 16)
        x1s_ref[pl.ds(c, tm, stride=half), :] = word


def _merge(o_hg, o_sb, x, wgate, wbh, wbs, wo, lg, lb, alpha, tm):
    t, d = x.shape
    nchunk = d // LANES
    row = lambda off: (lambda i: (i, off))
    full = lambda i: (0, 0)
    return pl.pallas_call(
        functools.partial(_merge_kernel, alpha=alpha, tm=tm, nchunk=nchunk),
        grid=(t // tm,),
        in_specs=[pl.BlockSpec((tm, d), row(0)), pl.BlockSpec((tm, d), row(0)),
                  pl.BlockSpec((tm, d), row(0)),
                  pl.BlockSpec((d, 2 * d), full),
                  pl.BlockSpec((d, d), full), pl.BlockSpec((d, d), full), pl.BlockSpec((d, d), full),
                  pl.BlockSpec((1, d), full), pl.BlockSpec((1, d), full)],
        out_specs=[pl.BlockSpec((tm, d), row(0)), pl.BlockSpec((tm * nchunk // 2, LANES), row(0))],
        out_shape=[jax.ShapeDtypeStruct((t, d), F32),
                   jax.ShapeDtypeStruct((t * nchunk // 2, LANES), jnp.uint32)],
        compiler_params=pltpu.CompilerParams(
            dimension_semantics=("parallel",), vmem_limit_bytes=VMEM_LIMIT),
        name="merge_ln1",
    )(o_hg, o_sb, x, wgate, wbh, wbs, wo, lg, lb)


def _route_kernel(x_ref, wh_ref, wl_ref, bias_ref, u_ref, ones_ref,
                  idx_ref, rank_ref, w_ref, cnt_ref, carry, *, tile, n_exp):
    @pl.when(pl.program_id(0) == 0)
    def _():
        carry[...] = jnp.zeros_like(carry)

    x = x_ref[...]
    xh = x.astype(BF16)
    xl = (x - xh.astype(F32)).astype(BF16)
    wh = wh_ref[...]
    nt = functools.partial(lax.dot_general, dimension_numbers=NT_DIMS, preferred_element_type=F32)
    logits = nt(wh, xh) + nt(wh, xl) + nt(wl_ref[...], xh)
    scores = _sigmoid(logits)
    biased = scores + bias_ref[...]

    neg_inf = -jnp.inf
    gsz = n_exp // N_GROUPS
    groups = [biased[g * gsz:(g + 1) * gsz] for g in range(N_GROUPS)]
    gscore = []
    for xg in groups:
        m1 = jnp.max(xg, axis=0, keepdims=True)
        n1 = jnp.sum(jnp.where(xg == m1, 1.0, 0.0), axis=0, keepdims=True)
        m2 = jnp.max(jnp.where(xg < m1, xg, neg_inf), axis=0, keepdims=True)
        gscore.append(m1 + jnp.where(n1 >= 2.0, m1, m2))
    masked = []
    for g in range(N_GROUPS):
        beaten_by = jnp.zeros_like(gscore[g])
        for h in range(N_GROUPS):
            if h < g:
                beaten_by = beaten_by + jnp.where(gscore[h] >= gscore[g], 1.0, 0.0)
            elif h > g:
                beaten_by = beaten_by + jnp.where(gscore[h] > gscore[g], 1.0, 0.0)
        keep = jnp.broadcast_to(beaten_by, (gsz, tile)) < float(TOPK_GROUPS)
        masked.append(jnp.where(keep, groups[g], jnp.finfo(F32).min))
    cur = jnp.concatenate(masked, axis=0)

    eid = lax.broadcasted_iota(jnp.int32, (n_exp, tile), 0).astype(F32)
    chosen = jnp.zeros((n_exp, tile), F32)
    idxs, ws = [], []
    for _ in range(TOP_K):
        m = jnp.max(cur, axis=0, keepdims=True)
        ik = jnp.min(jnp.where(cur == m, eid, float(n_exp)), axis=0, keepdims=True)
        sel = eid == ik
        idxs.append(ik)
        ws.append(jnp.sum(jnp.where(sel, scores, 0.0), axis=0, keepdims=True))
        chosen = chosen + jnp.where(sel, 1.0, 0.0)
        cur = jnp.where(sel, neg_inf, cur)

    chosen_b = chosen.astype(BF16)
    run = carry[...]
    rank = _dot(chosen_b, u_ref[...]) + jnp.concatenate([run] * (tile // LANES), axis=1)
    ranks = [jnp.sum(jnp.where(eid == ik, rank, 0.0), axis=0, keepdims=True) for ik in idxs]
    run = run + _dot(chosen_b, ones_ref[...])
    carry[...] = run
    cnt_ref[...] = run

    wsum = ws[0]
    for wk in ws[1:]:
        wsum = wsum + wk
    idx_ref[...] = jnp.concatenate(idxs, axis=0).astype(jnp.int32)
    rank_ref[...] = jnp.concatenate(ranks, axis=0).astype(jnp.int32)
    w_ref[...] = jnp.concatenate([wk / wsum * ROUTED_SCALE for wk in ws], axis=0)


def _route(x1, w_router, router_bias, tile):
    t, d = x1.shape
    n_exp = w_router.shape[1]
    wt = w_router.astype(F32).T
    wh = wt.astype(BF16)
    wl = (wt - wh.astype(F32)).astype(BF16)
    r = jnp.arange(tile)
    u = (r[:, None] < r[None, :]).astype(BF16)
    ones = jnp.ones((tile, LANES), BF16)
    full = lambda i: (0, 0)
    tok = lambda i: (0, i)
    return pl.pallas_call(
        functools.partial(_route_kernel, tile=tile, n_exp=n_exp),
        grid=(t // tile,),
        in_specs=[pl.BlockSpec((tile, d), lambda i: (i, 0)),
                  pl.BlockSpec((n_exp, d), full), pl.BlockSpec((n_exp, d), full),
                  pl.BlockSpec((n_exp, 1), full),
                  pl.BlockSpec((tile, tile), full), pl.BlockSpec((tile, LANES), full)],
        out_specs=[pl.BlockSpec((TOP_K, tile), tok), pl.BlockSpec((TOP_K, tile), tok),
                   pl.BlockSpec((TOP_K, tile), tok), pl.BlockSpec((n_exp, LANES), full)],
        out_shape=[jax.ShapeDtypeStruct((TOP_K, t), jnp.int32), jax.ShapeDtypeStruct((TOP_K, t), jnp.int32),
                   jax.ShapeDtypeStruct((TOP_K, t), F32), jax.ShapeDtypeStruct((n_exp, LANES), F32)],
        scratch_shapes=[pltpu.VMEM((n_exp, LANES), F32)],
        compiler_params=pltpu.CompilerParams(
            dimension_semantics=("arbitrary",), vmem_limit_bytes=VMEM_LIMIT),
        name="router_topk",
    )(x1, wh, wl, router_bias.astype(F32).reshape(n_exp, 1), u, ones)


def _dest_kernel(idx_ref, rank_ref, pstart_ref, dstx_ref, dsty_ref, *, tile, n_exp, nchunk):
    eid = lax.broadcasted_iota(jnp.int32, (n_exp, tile), 0)
    pstart = jnp.broadcast_to(pstart_ref[...], (n_exp, tile))
    rows = []
    for k in range(TOP_K):
        hit = eid == idx_ref[k:k + 1, :]
        base = jnp.sum(jnp.where(hit, pstart, 0.0), axis=0, keepdims=True)
        rows.append(base.astype(jnp.int32) + rank_ref[k:k + 1, :])
    dst = jnp.concatenate(rows, axis=0)
    dstx_ref[...] = dst * (nchunk // 2)
    dsty_ref[...] = dst * nchunk


def _dest_rows(idx_t, rank_t, pstart, nchunk, tile):
    t = idx_t.shape[1]
    n_exp = pstart.shape[0]
    tok = lambda i: (0, i)
    return pl.pallas_call(
        functools.partial(_dest_kernel, tile=tile, n_exp=n_exp, nchunk=nchunk),
        grid=(t // tile,),
        in_specs=[pl.BlockSpec((TOP_K, tile), tok), pl.BlockSpec((TOP_K, tile), tok),
                  pl.BlockSpec((n_exp, 1), lambda i: (0, 0))],
        out_specs=[pl.BlockSpec((TOP_K, tile), tok)] * 2,
        out_shape=[jax.ShapeDtypeStruct((TOP_K, t), jnp.int32)] * 2,
        compiler_params=pltpu.CompilerParams(
            dimension_semantics=("parallel",), vmem_limit_bytes=VMEM_LIMIT),
        name="dest_rows",
    )(idx_t, rank_t, pstart.astype(F32).reshape(n_exp, 1))


def _dispatch_kernel(pstart_ref, pend_ref, dst_hbm, x_hbm, xs_hbm,
                     dst_s, xbuf, zbuf, isem, lsem, dsem, zsem,
                     *, td, nchunk, mb, n_exp, nsteps, n_blocks):
    i = pl.program_id(0)
    rows = mb * nchunk
    trows = td * nchunk

    def idx_copy(step, slot):
        return pltpu.make_async_copy(dst_hbm.at[:, pl.ds(step * td, td)], dst_s.at[slot], isem.at[slot])

    def tile_load(step, slot):
        start = pl.multiple_of(step * trows, trows)
        return pltpu.make_async_copy(x_hbm.at[pl.ds(start, trows), :], xbuf.at[slot], lsem.at[slot])

    def wait_rows(slot):
        for _ in range(TOP_K):
            pltpu.make_async_copy(xbuf.at[slot], xbuf.at[slot], dsem.at[slot]).wait()

    @pl.when(i == 0)
    def _():
        idx_copy(0, 0).start()
        tile_load(0, 0).start()
        zbuf[...] = jnp.zeros_like(zbuf)

        def zero_copy(e):
            start = pl.multiple_of((pend_ref[e] - mb) * nchunk, nchunk)
            return pltpu.make_async_copy(zbuf, xs_hbm.at[pl.ds(start, rows), :], zsem)

        def zstart(e, _):
            @pl.when(pend_ref[e] > pstart_ref[e])
            def _():
                zero_copy(e).start()
            return 0

        def zwait(e, _):
            @pl.when(pend_ref[e] > pstart_ref[e])
            def _():
                zero_copy(e).wait()
            return 0

        lax.fori_loop(0, n_exp, zstart, 0)
        lax.fori_loop(0, n_exp, zwait, 0)

        def tail_copy(b):
            start = pl.multiple_of(b * rows, rows)
            return pltpu.make_async_copy(zbuf, xs_hbm.at[pl.ds(start, rows), :], zsem)

        def tstart(b, _):
            tail_copy(b).start()
            return 0

        def twait(b, _):
            tail_copy(b).wait()
            return 0

        first_unused = pend_ref[n_exp - 1] // mb
        lax.fori_loop(first_unused, n_blocks, tstart, 0)
        lax.fori_loop(first_unused, n_blocks, twait, 0)

    for cur in range(3):
        nxt = (cur + 1) % 3

        @pl.when(i % 3 == cur)
        def _():
            idx_copy(i, cur).wait()

            @pl.when(i >= 2)
            def _():
                wait_rows(nxt)

            @pl.when(i + 1 < nsteps)
            def _():
                idx_copy(i + 1, nxt).start()
                tile_load(i + 1, nxt).start()

            tile_load(i, cur).wait()

            def body(r, _):
                src = pl.multiple_of(r * nchunk, nchunk)
                for k in range(TOP_K):
                    dst = pl.multiple_of(dst_s[cur, k, r], nchunk)
                    pltpu.make_async_copy(xbuf.at[cur, pl.ds(src, nchunk), :],
                                          xs_hbm.at[pl.ds(dst, nchunk), :], dsem.at[cur]).start(priority=k % 2)
                return 0

            lax.fori_loop(0, td, body, 0, unroll=2)

            @pl.when(i == nsteps - 1)
            def _():
                @pl.when(i >= 1)
                def _():
                    wait_rows((cur + 2) % 3)

                wait_rows(cur)


def _dispatch(x2, dst_t, pstart, pend, p_rows, td):
    n_exp = pstart.shape[0]
    t = dst_t.shape[1]
    nchunk = x2.shape[0] // t
    nsteps = t // td
    grid_spec = pltpu.PrefetchScalarGridSpec(
        num_scalar_prefetch=2,
        grid=(nsteps,),
        in_specs=[pl.BlockSpec(memory_space=pl.ANY)] * 2,
        out_specs=pl.BlockSpec(memory_space=pl.ANY),
        scratch_shapes=[pltpu.SMEM((3, TOP_K, td), jnp.int32),
                        pltpu.VMEM((3, td * nchunk, LANES), x2.dtype),
                        pltpu.VMEM((MOE_BLOCK * nchunk, LANES), x2.dtype),
                        pltpu.SemaphoreType.DMA((3,)),
                        pltpu.SemaphoreType.DMA((3,)),
                        pltpu.SemaphoreType.DMA((3,)),
                        pltpu.SemaphoreType.DMA],
    )
    return pl.pallas_call(
        functools.partial(_dispatch_kernel, td=td, nchunk=nchunk, mb=MOE_BLOCK, n_exp=n_exp, nsteps=nsteps,
                          n_blocks=p_rows // MOE_BLOCK),
        grid_spec=grid_spec,
        out_shape=jax.ShapeDtypeStruct((p_rows * nchunk, LANES), x2.dtype),
        compiler_params=pltpu.CompilerParams(
            dimension_semantics=("arbitrary",), vmem_limit_bytes=VMEM_LIMIT),
        name="moe_dispatch",
    )(pstart, pend, dst_t, x2)


def _experts_kernel(bexp_ref, nused_ref, xs_ref, wg_ref, wu_ref, wd_ref, ys_ref, wg_b, wu_b, wd_b,
                    *, mb, nchunk):
    i = pl.program_id(0)

    @pl.when(i < nused_ref[0])
    def _():
        new_expert = jnp.logical_or(i == 0, bexp_ref[i] != bexp_ref[jnp.maximum(i - 1, 0)])

        @pl.when(new_expert)
        def _():
            wg_b[...] = wg_ref[0].astype(BF16)
            wu_b[...] = wu_ref[0].astype(BF16)
            wd_b[...] = wd_ref[0].astype(BF16)

        half = nchunk // 2
        words = [xs_ref[pl.ds(c, mb, stride=half), :] for c in range(half)]
        lows = [pltpu.bitcast(wv << 16, F32) for wv in words]
        highs = [pltpu.bitcast(wv & jnp.uint32(0xFFFF0000), F32) for wv in words]
        xb = jnp.concatenate(lows + highs, axis=1).astype(BF16)
        gate = _dot(xb, wg_b[...])
        up = _dot(xb, wu_b[...])
        hid = (gate * _sigmoid(gate)) * up
        y = _dot(hid.astype(BF16), wd_b[...])
        for c in range(nchunk):
            ys_ref[pl.ds(c, mb, stride=nchunk), :] = y[:, c * LANES:(c + 1) * LANES]

    @pl.when(i >= nused_ref[0])
    def _():
        ys_ref[...] = jnp.zeros_like(ys_ref)


def _experts(xs, block_expert, nused, w_gate, w_up, w_down):
    e, d, de = w_gate.shape
    nchunk = d // LANES
    mb = MOE_BLOCK
    rows = mb * nchunk
    xrows = rows // 2
    n_blocks = xs.shape[0] // xrows
    blk = lambda i, be, nu: (jnp.minimum(i, nu[0] - 1), 0)
    wsel = lambda i, be, nu: (be[i], 0, 0)
    grid_spec = pltpu.PrefetchScalarGridSpec(
        num_scalar_prefetch=2,
        grid=(n_blocks,),
        in_specs=[pl.BlockSpec((xrows, LANES), blk),
                  pl.BlockSpec((1, d, de), wsel), pl.BlockSpec((1, d, de), wsel),
                  pl.BlockSpec((1, de, d), wsel)],
        out_specs=pl.BlockSpec((rows, LANES), lambda i, be, nu: (i, 0)),
        scratch_shapes=[pltpu.VMEM((d, de), BF16), pltpu.VMEM((d, de), BF16), pltpu.VMEM((de, d), BF16)],
    )
    return pl.pallas_call(
        functools.partial(_experts_kernel, mb=mb, nchunk=nchunk),
        grid_spec=grid_spec,
        out_shape=jax.ShapeDtypeStruct((n_blocks * rows, LANES), F32),
        compiler_params=pltpu.CompilerParams(
            dimension_semantics=("arbitrary",), vmem_limit_bytes=VMEM_LIMIT),
        name="routed_experts",
    )(block_expert, nused, xs, w_gate, w_up, w_down)


def _final_kernel(dst_hbm, w_hbm, ys_hbm, x1_ref, wsg_ref, wsu_ref, wsd_ref,
                  lg_ref, lb_ref, o_ref, ybuf, rbuf, dst_s, w_s, isem, gsem,
                  *, tm, nchunk, alpha, nsteps):
    i = pl.program_id(0)
    tok_rows = TOP_K * nchunk

    def idx_copies(step, slot):
        cols = pl.ds(step * tm, tm)
        return (pltpu.make_async_copy(dst_hbm.at[:, cols], dst_s.at[slot], isem.at[slot]),
                pltpu.make_async_copy(w_hbm.at[:, cols], w_s.at[slot], isem.at[slot]))

    def issue_gathers(slot):
        def body(r, _):
            for k in range(TOP_K):
                src = pl.multiple_of(dst_s[slot, k, r], nchunk)
                pltpu.make_async_copy(ys_hbm.at[pl.ds(src, nchunk), :],
                                      ybuf.at[slot, pl.ds(r * tok_rows + k * nchunk, nchunk), :],
                                      gsem.at[slot]).start(priority=k % 2)
            return 0

        lax.fori_loop(0, tm, body, 0, unroll=2)

    def combine(slot):
        def body(r, _):
            acc = ybuf[slot, pl.ds(r * tok_rows, nchunk), :] * w_s[slot, 0, r]
            for k in range(1, TOP_K):
                acc = acc + ybuf[slot, pl.ds(r * tok_rows + k * nchunk, nchunk), :] * w_s[slot, k, r]
            rbuf[pl.ds(r * nchunk, nchunk), :] = acc
            return 0

        lax.fori_loop(0, tm, body, 0, unroll=2)

    @pl.when(i == 0)
    def _():
        for cp in idx_copies(0, 0):
            cp.start()
        for cp in idx_copies(0, 0):
            cp.wait()
        issue_gathers(0)

        @pl.when(nsteps > 1)
        def _():
            for cp in idx_copies(1, 1):
                cp.start()

    for cur in range(2):
        nxt = 1 - cur

        @pl.when(i % 2 == cur)
        def _():
            @pl.when(i + 1 < nsteps)
            def _():
                for cp in idx_copies(i + 1, nxt):
                    cp.wait()
                issue_gathers(nxt)

            pltpu.make_async_copy(ybuf.at[cur], ybuf.at[cur], gsem.at[cur]).wait()
            combine(cur)

            @pl.when(i + 2 < nsteps)
            def _():
                for cp in idx_copies(i + 2, cur):
                    cp.start()

    routed = jnp.concatenate([rbuf[pl.ds(c, tm, stride=nchunk), :] for c in range(nchunk)], axis=1)

    x1 = x1_ref[...]
    xb = x1.astype(BF16)
    gate = _dot(xb, wsg_ref[...])
    up = _dot(xb, wsu_ref[...])
    shared = _dot(((gate * _sigmoid(gate)) * up).astype(BF16), wsd_ref[...])
    o_ref[...] = _layer_norm(alpha * x1 + (routed + shared), lg_ref[...], lb_ref[...])


def _final(ys, dst_t, w_t, x1, wsg, wsu, wsd, lg, lb, alpha, tm):
    t, d = x1.shape
    nchunk = d // LANES
    ds = wsg.shape[1]
    nsteps = t // tm
    full = lambda i: (0, 0)
    return pl.pallas_call(
        functools.partial(_final_kernel, tm=tm, nchunk=nchunk, alpha=alpha, nsteps=nsteps),
        grid=(nsteps,),
        in_specs=[pl.BlockSpec(memory_space=pl.ANY)] * 3 + [
            pl.BlockSpec((tm, d), lambda i: (i, 0)),
            pl.BlockSpec((d, ds), full), pl.BlockSpec((d, ds), full), pl.BlockSpec((ds, d), full),
            pl.BlockSpec((1, d), full), pl.BlockSpec((1, d), full)],
        out_specs=pl.BlockSpec((tm, d), lambda i: (i, 0)),
        out_shape=jax.ShapeDtypeStruct((t, d), F32),
        scratch_shapes=[pltpu.VMEM((2, tm * TOP_K * nchunk, LANES), F32),
                        pltpu.VMEM((tm * nchunk, LANES), F32),
                        pltpu.SMEM((2, TOP_K, tm), jnp.int32),
                        pltpu.SMEM((2, TOP_K, tm), F32),
                        pltpu.SemaphoreType.DMA((2,)),
                        pltpu.SemaphoreType.DMA((2,))],
        compiler_params=pltpu.CompilerParams(
            dimension_semantics=("arbitrary",), vmem_limit_bytes=VMEM_LIMIT),
        name="combine_shared_ln2",
    )(dst_t, w_t, ys, x1, wsg, wsu, wsd, lg, lb)


def _block_tables(counts, n_rows):
    mb = MOE_BLOCK
    n_exp = counts.shape[0]
    padded = (counts + mb - 1) // mb * mb
    pad_end = jnp.cumsum(padded)
    pad_start = pad_end - padded
    n_blocks = n_rows // mb
    first_row = jnp.arange(n_blocks, dtype=jnp.int32) * mb
    block_expert = jnp.minimum(
        jnp.sum((pad_end[None, :] <= first_row[:, None]).astype(jnp.int32), axis=1), n_exp - 1)
    nused = (pad_end[-1:] // mb).astype(jnp.int32)
    return pad_start.astype(jnp.int32), pad_end.astype(jnp.int32), block_expert, nused


def _tile(n, pref):
    while n % pref:
        pref //= 2
    return pref


def kernel(x, w_in, lower_bounds, hg_norm_g, w_branch_hg, w_branch_sb, w_out, ln1_g, ln1_b,
           w_router, router_bias, w_exp_gate, w_exp_up, w_exp_down,
           w_sh_gate, w_sh_up, w_sh_down, ln2_g, ln2_b):
    depth = w_in.shape[0]
    assert depth == 1, "single-layer block only"
    batch, seq, d = x.shape
    t = batch * seq
    n_exp = w_router.shape[-1]
    alpha = (2.0 * depth) ** 0.25
    nchunk = d // LANES

    lb = jnp.cumsum(jax.nn.softmax(lower_bounds.astype(F32), axis=0), axis=0)[0]
    xf = x.reshape(t, d)

    w = w_in[0].astype(BF16)
    c = d
    o_hg = _hgrn(xf, w[:, 0:4 * c], lb, hg_norm_g[0].astype(F32), batch, seq, _tile(seq, 256))
    pbf = _matmul(xf, w[:, 4 * c:7 * c], BF16, _tile(t, 1024), 512)
    o_sb = _sb_attention(pbf, batch, seq, _tile(seq, 256), 2)

    x1, x1s = _merge(o_hg, o_sb, xf, w[:, 7 * c:9 * c],
                     w_branch_hg[0].astype(BF16), w_branch_sb[0].astype(BF16), w_out[0].astype(BF16),
                     ln1_g[0].reshape(1, d).astype(F32), ln1_b[0].reshape(1, d).astype(F32),
                     alpha, _tile(t, 512))

    idx_t, rank_t, w_t, cnt = _route(x1, w_router[0], router_bias[0], _tile(t, 256))
    p_rows = t * TOP_K + n_exp * MOE_BLOCK
    pstart, pend, block_expert, nused = _block_tables(cnt[:, 0].astype(jnp.int32), p_rows)
    dstx_t, dsty_t = _dest_rows(idx_t, rank_t, pstart, nchunk, _tile(t, 256))
    xs = _dispatch(x1s, dstx_t, pstart, pend, p_rows, _tile(t, 128))
    ys = _experts(xs, block_expert, nused, w_exp_gate[0], w_exp_up[0], w_exp_down[0])
    out = _final(ys, dsty_t, w_t, x1,
                 w_sh_gate[0].astype(BF16), w_sh_up[0].astype(BF16), w_sh_down[0].astype(BF16),
                 ln2_g[0].reshape(1, d).astype(F32), ln2_b[0].reshape(1, d).astype(F32),
                 alpha, _tile(t, 128))
    return out.reshape(batch, seq, d)
```

```python
import functools

import jax
import jax.numpy as jnp
from jax import lax
from jax.experimental import pallas as pl
from jax.experimental.pallas import tpu as pltpu

F32 = jnp.float32
BF16 = jnp.bfloat16

LANES = 128
SUBLANES = 8
VMEM_LIMIT = 48 * 1024 * 1024

HG_HEADS = 8
HG_CHUNK = 32
SB_HEADS = 8
N_GROUPS = 8
TOPK_GROUPS = 4
TOP_K = 8
ROUTED_SCALE = 2.5
MOE_BLOCK = 512
MOE_SUB = 256
LN_EPS = 1e-5
RMS_EPS = 1e-6
STICK_DEAD = 110.0

NT_DIMS = (((1,), (1,)), ((), ()))
TN_DIMS = (((0,), (0,)), ((), ()))


def _dot(a, b):
    return jnp.dot(a, b, preferred_element_type=F32)


def _sigmoid(x):
    return 1.0 / (1.0 + jnp.exp(-x))


def _split2(x):
    hi = x.astype(BF16)
    lo = (x - hi.astype(F32)).astype(BF16)
    return hi, lo


def _layer_norm(r, g, b):
    mu = jnp.mean(r, axis=-1, keepdims=True)
    d = r - mu
    var = jnp.mean(d * d, axis=-1, keepdims=True)
    return d * lax.rsqrt(var + LN_EPS) * g + b


def _mm_kernel(x_ref, w_ref, o_ref):
    o_ref[...] = _dot(x_ref[...].astype(BF16), w_ref[...]).astype(o_ref.dtype)


def _matmul(x, w, out_dtype, tm, tn):
    m, k = x.shape
    n = w.shape[1]
    return pl.pallas_call(
        _mm_kernel,
        grid=(m // tm, n // tn),
        in_specs=[pl.BlockSpec((tm, k), lambda i, j: (i, 0)),
                  pl.BlockSpec((k, tn), lambda i, j: (0, j))],
        out_specs=pl.BlockSpec((tm, tn), lambda i, j: (i, j)),
        out_shape=jax.ShapeDtypeStruct((m, n), out_dtype),
        compiler_params=pltpu.CompilerParams(
            dimension_semantics=("parallel", "arbitrary"), vmem_limit_bytes=VMEM_LIMIT),
        name="in_proj",
    )(x, w)


def _hgrn_kernel(x_ref, w_ref, lb_ref, g_ref, o_ref, st_ref, *, ts, chunk, heads):
    @pl.when(pl.program_id(1) == 0)
    def _():
        st_ref[...] = jnp.zeros_like(st_ref)

    shift = chunk.bit_length() - 1
    row = lax.broadcasted_iota(jnp.int32, (ts, ts), 0)
    col = lax.broadcasted_iota(jnp.int32, (ts, ts), 1)
    same = (row >> shift) == (col >> shift)
    causal = jnp.logical_and(same, col <= row)
    tri = jnp.where(causal, 1.0, 0.0).astype(BF16)
    ones = jnp.where(same, 1.0, 0.0).astype(BF16)
    d = LANES
    hd = heads * d

    xb = x_ref[...].astype(BF16)
    hq_all = _dot(xb, w_ref[:, 0:hd])
    hf_all = _dot(xb, w_ref[:, hd:2 * hd])
    hi_all = _dot(xb, w_ref[:, 2 * hd:3 * hd]).astype(BF16)
    hg_all = _dot(xb, w_ref[:, 3 * hd:4 * hd])

    for hh in range(heads):
        cols = slice(hh * d, (hh + 1) * d)
        lb = lb_ref[:, cols]
        hf = hf_all[:, cols]
        log_f = jnp.log(lb + (1.0 - lb) * _sigmoid(hf))
        k_in = (1.0 - lb) * _sigmoid(-hf)

        parts = jnp.concatenate(_split2(log_f), axis=1)
        cs = _dot(tri, parts)
        tot = _dot(ones, parts)
        b = cs[:, :d] + cs[:, d:]
        b_end = tot[:, :d] + tot[:, d:]

        q = hq_all[:, cols]
        q_start = (q * jnp.exp(b)).astype(BF16)
        q_end = (q * jnp.exp(b - b_end)).astype(BF16)
        k_end = (k_in * jnp.exp(b_end - b)).astype(BF16)
        v = hi_all[:, cols]

        scores = lax.dot_general(q_end, k_end, NT_DIMS, preferred_element_type=F32)
        scores = jnp.where(causal, scores, 0.0)
        o_intra = _dot(scores.astype(BF16), v)

        dec = jnp.exp(b_end)
        st = st_ref[hh]
        outs = []
        for c in range(ts // chunk):
            lo = c * chunk
            outs.append(lax.dot_general(q_start[lo:lo + chunk], st.astype(BF16), NT_DIMS,
                                        preferred_element_type=F32))
            kv = lax.dot_general(v[lo:lo + chunk], k_end[lo:lo + chunk], TN_DIMS,
                                 preferred_element_type=F32)
            st = dec[lo:lo + 1, :] * st + kv
        st_ref[hh] = st
        o = o_intra + jnp.concatenate(outs, axis=0)

        o = o * lax.rsqrt(jnp.mean(o * o, axis=-1, keepdims=True) + RMS_EPS)
        o = o * g_ref[:, cols]
        hg = hg_all[:, cols]
        o_ref[:, cols] = (o * (hg * _sigmoid(hg))).astype(o_ref.dtype)


def _hgrn(x, w_hg, lb, g, batch, seq, ts):
    t, dm = x.shape
    h = HG_HEADS
    hd = h * LANES
    ns = seq // ts
    full = lambda b, s: (0, 0)
    return pl.pallas_call(
        functools.partial(_hgrn_kernel, ts=ts, chunk=HG_CHUNK, heads=h),
        grid=(batch, ns),
        in_specs=[pl.BlockSpec((ts, dm), lambda b, s: (b * ns + s, 0)),
                  pl.BlockSpec((dm, 4 * hd), full),
                  pl.BlockSpec((1, hd), full),
                  pl.BlockSpec((1, hd), full)],
        out_specs=pl.BlockSpec((ts, hd), lambda b, s: (b * ns + s, 0)),
        out_shape=jax.ShapeDtypeStruct((t, hd), BF16),
        scratch_shapes=[pltpu.VMEM((h, LANES, LANES), F32)],
        compiler_params=pltpu.CompilerParams(
            dimension_semantics=("parallel", "arbitrary"), vmem_limit_bytes=VMEM_LIMIT),
        name="hgrn2",
    )(x, w_hg, lb.reshape(1, hd), g.reshape(1, hd))


def _sb_kernel(q_ref, k_ref, v_ref, u_ref, o_ref, *, tq, scale, hp):
    i = pl.program_id(2)
    u = u_ref[...]
    rep = tq // LANES
    d = LANES

    def block(hh, j, carry, acc, masked, live=None):
        cols = slice(hh * d, (hh + 1) * d)
        start = pl.multiple_of(j * tq, tq)
        q = q_ref[:, cols]
        kj = k_ref[pl.ds(start, tq), cols]
        vj = v_ref[pl.ds(start, tq), cols]
        z = lax.dot_general(q, kj, NT_DIMS, preferred_element_type=F32) * scale
        sp = jnp.maximum(z, 0.0) + jnp.log(1.0 + jnp.exp(-jnp.abs(z)))
        if masked:
            row = lax.broadcasted_iota(jnp.int32, (tq, tq), 0)
            col = lax.broadcasted_iota(jnp.int32, (tq, tq), 1)
            before = col < row
            drop = jnp.where(before, sp, 0.0)
        elif live is not None:
            drop = sp * live
        else:
            drop = sp
        hi = drop.astype(BF16)
        lo = (drop - hi.astype(F32)).astype(BF16)
        later = _dot(hi, u) + _dot(lo, u)
        stick = later + jnp.concatenate([carry] * rep, axis=1)
        w = jnp.exp((z - sp) - stick)
        if masked:
            w = jnp.where(before, w, 0.0)
        elif live is not None:
            w = w * live
        acc = acc + _dot(w.astype(BF16), vj)
        carry = carry + jnp.broadcast_to(later[:, 0:1] + drop[:, 0:1], (tq, LANES))
        return carry, acc

    def some_row_alive(carries):
        m = jnp.min(carries[0])
        for c in carries[1:]:
            m = jnp.minimum(m, jnp.min(c))
        return (m < STICK_DEAD).astype(jnp.int32)

    zero = jnp.zeros((tq, LANES), F32)
    live = (i > 0).astype(F32)
    prev = jnp.maximum(i - 1, 0)
    carries, accs = [], []
    for hh in range(hp):
        carry, acc = block(hh, i, zero, zero, True)
        carry, acc = block(hh, prev, carry, acc, False, live=live)
        carries.append(carry)
        accs.append(acc)

    def cond(state):
        return jnp.logical_and(state[0] >= 0, state[1] > 0)

    def body(state):
        j, _, carries, accs = state
        out = [block(hh, j, carries[hh], accs[hh], False) for hh in range(hp)]
        carries = tuple(o[0] for o in out)
        accs = tuple(o[1] for o in out)
        return j - 1, some_row_alive(carries), carries, accs

    _, _, carries, accs = lax.while_loop(
        cond, body, (i - 2, some_row_alive(carries), tuple(carries), tuple(accs)))
    for hh in range(hp):
        o_ref[:, hh * d:(hh + 1) * d] = accs[hh].astype(o_ref.dtype)


def _sb_attention(pbf, batch, seq, tq, hp):
    t = batch * seq
    h = SB_HEADS
    ng = h // hp
    w = hp * LANES
    nq = seq // tq
    r = jnp.arange(tq)
    u = (r[:, None] > r[None, :]).astype(BF16)
    return pl.pallas_call(
        functools.partial(_sb_kernel, tq=tq, scale=float(LANES) ** -0.5, hp=hp),
        grid=(batch, ng, nq),
        in_specs=[pl.BlockSpec((tq, w), lambda b, hg, i: (b * nq + i, hg)),
                  pl.BlockSpec((seq, w), lambda b, hg, i: (b, ng + hg)),
                  pl.BlockSpec((seq, w), lambda b, hg, i: (b, 2 * ng + hg)),
                  pl.BlockSpec((tq, tq), lambda b, hg, i: (0, 0))],
        out_specs=pl.BlockSpec((tq, w), lambda b, hg, i: (b * nq + i, hg)),
        out_shape=jax.ShapeDtypeStruct((t, h * LANES), BF16),
        compiler_params=pltpu.CompilerParams(
            dimension_semantics=("parallel", "parallel", "arbitrary"), vmem_limit_bytes=VMEM_LIMIT),
        name="stick_breaking",
    )(pbf, pbf, pbf, u)


def _merge_kernel(ohg_ref, osb_ref, x_ref, wgate_ref, wbh_ref, wbs_ref, wo_ref,
                  lg_ref, lb_ref, x1_ref, x1s_ref, *, alpha, tm, nchunk):
    x = x_ref[...]
    xb = x.astype(BF16)
    d = x.shape[1]
    g_hg = _dot(xb, wgate_ref[:, 0:d])
    g_sb = _dot(xb, wgate_ref[:, d:2 * d])
    y_hg = _dot(ohg_ref[...], wbh_ref[...])
    y_sb = _dot(osb_ref[...], wbs_ref[...])
    merged = _sigmoid(g_hg) * y_hg + _sigmoid(g_sb) * y_sb
    hmix = _dot(merged.astype(BF16), wo_ref[...])
    x1 = _layer_norm(alpha * x + hmix, lg_ref[...], lb_ref[...])
    x1_ref[...] = x1
    half = nchunk // 2
    for c in range(half):
        lo = x1[:, c * LANES:(c + 1) * LANES].astype(BF16).astype(F32)
        hi = x1[:, (c + half) * LANES:(c + half + 1) * LANES].astype(BF16).astype(F32)
        word = pltpu.bitcast(hi, jnp.uint32) | (pltpu.bitcast(lo, jnp.uint32) >> 16)
        x1s_ref[pl.ds(c, tm, stride=half), :] = word


def _merge(o_hg, o_sb, x, wgate, wbh, wbs, wo, lg, lb, alpha, tm):
    t, d = x.shape
    nchunk = d // LANES
    row = lambda off: (lambda i: (i, off))
    full = lambda i: (0, 0)
    return pl.pallas_call(
        functools.partial(_merge_kernel, alpha=alpha, tm=tm, nchunk=nchunk),
        grid=(t // tm,),
        in_specs=[pl.BlockSpec((tm, d), row(0)), pl.BlockSpec((tm, d), row(0)),
                  pl.BlockSpec((tm, d), row(0)),
                  pl.BlockSpec((d, 2 * d), full),
                  pl.BlockSpec((d, d), full), pl.BlockSpec((d, d), full), pl.BlockSpec((d, d), full),
                  pl.BlockSpec((1, d), full), pl.BlockSpec((1, d), full)],
        out_specs=[pl.BlockSpec((tm, d), row(0)), pl.BlockSpec((tm * nchunk // 2, LANES), row(0))],
        out_shape=[jax.ShapeDtypeStruct((t, d), F32),
                   jax.ShapeDtypeStruct((t * nchunk // 2, LANES), jnp.uint32)],
        compiler_params=pltpu.CompilerParams(
            dimension_semantics=("parallel",), vmem_limit_bytes=VMEM_LIMIT),
        name="merge_ln1",
    )(o_hg, o_sb, x, wgate, wbh, wbs, wo, lg, lb)


def _route_kernel(x_ref, wh_ref, wl_ref, bias_ref, u_ref, ones_ref,
                  idx_ref, rank_ref, w_ref, cnt_ref, carry, *, tile, n_exp):
    @pl.when(pl.program_id(0) == 0)
    def _():
        carry[...] = jnp.zeros_like(carry)

    x = x_ref[...]
    xh = x.astype(BF16)
    xl = (x - xh.astype(F32)).astype(BF16)
    wh = wh_ref[...]
    nt = functools.partial(lax.dot_general, dimension_numbers=NT_DIMS, preferred_element_type=F32)
    logits = nt(wh, xh) + nt(wh, xl) + nt(wl_ref[...], xh)
    scores = _sigmoid(logits)
    biased = scores + bias_ref[...]

    neg_inf = -jnp.inf
    gsz = n_exp // N_GROUPS
    groups = [biased[g * gsz:(g + 1) * gsz] for g in range(N_GROUPS)]
    gscore = []
    for xg in groups:
        m1 = jnp.max(xg, axis=0, keepdims=True)
        n1 = jnp.sum(jnp.where(xg == m1, 1.0, 0.0), axis=0, keepdims=True)
        m2 = jnp.max(jnp.where(xg < m1, xg, neg_inf), axis=0, keepdims=True)
        gscore.append(m1 + jnp.where(n1 >= 2.0, m1, m2))
    masked = []
    for g in range(N_GROUPS):
        beaten_by = jnp.zeros_like(gscore[g])
        for h in range(N_GROUPS):
            if h < g:
                beaten_by = beaten_by + jnp.where(gscore[h] >= gscore[g], 1.0, 0.0)
            elif h > g:
                beaten_by = beaten_by + jnp.where(gscore[h] > gscore[g], 1.0, 0.0)
        keep = jnp.broadcast_to(beaten_by, (gsz, tile)) < float(TOPK_GROUPS)
        masked.append(jnp.where(keep, groups[g], jnp.finfo(F32).min))
    cur = jnp.concatenate(masked, axis=0)

    eid = lax.broadcasted_iota(jnp.int32, (n_exp, tile), 0).astype(F32)
    chosen = jnp.zeros((n_exp, tile), F32)
    idxs, ws = [], []
    for _ in range(TOP_K):
        m = jnp.max(cur, axis=0, keepdims=True)
        ik = jnp.min(jnp.where(cur == m, eid, float(n_exp)), axis=0, keepdims=True)
        sel = eid == ik
        idxs.append(ik)
        ws.append(jnp.sum(jnp.where(sel, scores, 0.0), axis=0, keepdims=True))
        chosen = chosen + jnp.where(sel, 1.0, 0.0)
        cur = jnp.where(sel, neg_inf, cur)

    chosen_b = chosen.astype(BF16)
    run = carry[...]
    rank = _dot(chosen_b, u_ref[...]) + jnp.concatenate([run] * (tile // LANES), axis=1)
    ranks = [jnp.sum(jnp.where(eid == ik, rank, 0.0), axis=0, keepdims=True) for ik in idxs]
    run = run + _dot(chosen_b, ones_ref[...])
    carry[...] = run
    cnt_ref[...] = run

    wsum = ws[0]
    for wk in ws[1:]:
        wsum = wsum + wk
    idx_ref[...] = jnp.concatenate(idxs, axis=0).astype(jnp.int32)
    rank_ref[...] = jnp.concatenate(ranks, axis=0).astype(jnp.int32)
    w_ref[...] = jnp.concatenate([wk / wsum * ROUTED_SCALE for wk in ws], axis=0)


def _route(x1, w_router, router_bias, tile):
    t, d = x1.shape
    n_exp = w_router.shape[1]
    wt = w_router.astype(F32).T
    wh = wt.astype(BF16)
    wl = (wt - wh.astype(F32)).astype(BF16)
    r = jnp.arange(tile)
    u = (r[:, None] < r[None, :]).astype(BF16)
    ones = jnp.ones((tile, LANES), BF16)
    full = lambda i: (0, 0)
    tok = lambda i: (0, i)
    return pl.pallas_call(
        functools.partial(_route_kernel, tile=tile, n_exp=n_exp),
        grid=(t // tile,),
        in_specs=[pl.BlockSpec((tile, d), lambda i: (i, 0)),
                  pl.BlockSpec((n_exp, d), full), pl.BlockSpec((n_exp, d), full),
                  pl.BlockSpec((n_exp, 1), full),
                  pl.BlockSpec((tile, tile), full), pl.BlockSpec((tile, LANES), full)],
        out_specs=[pl.BlockSpec((TOP_K, tile), tok), pl.BlockSpec((TOP_K, tile), tok),
                   pl.BlockSpec((TOP_K, tile), tok), pl.BlockSpec((n_exp, LANES), full)],
        out_shape=[jax.ShapeDtypeStruct((TOP_K, t), jnp.int32), jax.ShapeDtypeStruct((TOP_K, t), jnp.int32),
                   jax.ShapeDtypeStruct((TOP_K, t), F32), jax.ShapeDtypeStruct((n_exp, LANES), F32)],
        scratch_shapes=[pltpu.VMEM((n_exp, LANES), F32)],
        compiler_params=pltpu.CompilerParams(
            dimension_semantics=("arbitrary",), vmem_limit_bytes=VMEM_LIMIT),
        name="router_topk",
    )(x1, wh, wl, router_bias.astype(F32).reshape(n_exp, 1), u, ones)


def _dest_kernel(idx_ref, rank_ref, pstart_ref, dstx_ref, dsty_ref, *, tile, n_exp, nchunk):
    eid = lax.broadcasted_iota(jnp.int32, (n_exp, tile), 0)
    pstart = jnp.broadcast_to(pstart_ref[...], (n_exp, tile))
    rows = []
    for k in range(TOP_K):
        hit = eid == idx_ref[k:k + 1, :]
        base = jnp.sum(jnp.where(hit, pstart, 0.0), axis=0, keepdims=True)
        rows.append(base.astype(jnp.int32) + rank_ref[k:k + 1, :])
    dst = jnp.concatenate(rows, axis=0)
    dstx_ref[...] = dst * (nchunk // 2)
    dsty_ref[...] = dst * nchunk


def _dest_rows(idx_t, rank_t, pstart, nchunk, tile):
    t = idx_t.shape[1]
    n_exp = pstart.shape[0]
    tok = lambda i: (0, i)
    return pl.pallas_call(
        functools.partial(_dest_kernel, tile=tile, n_exp=n_exp, nchunk=nchunk),
        grid=(t // tile,),
        in_specs=[pl.BlockSpec((TOP_K, tile), tok), pl.BlockSpec((TOP_K, tile), tok),
                  pl.BlockSpec((n_exp, 1), lambda i: (0, 0))],
        out_specs=[pl.BlockSpec((TOP_K, tile), tok)] * 2,
        out_shape=[jax.ShapeDtypeStruct((TOP_K, t), jnp.int32)] * 2,
        compiler_params=pltpu.CompilerParams(
            dimension_semantics=("parallel",), vmem_limit_bytes=VMEM_LIMIT),
        name="dest_rows",
    )(idx_t, rank_t, pstart.astype(F32).reshape(n_exp, 1))


def _dispatch_kernel(pstart_ref, pend_ref, vend_ref, dst_hbm, x_hbm, xs_hbm,
                     dst_s, xbuf, zbuf, isem, lsem, dsem, zsem,
                     *, td, nchunk, mb, sub, n_exp, nsteps, n_blocks):
    i = pl.program_id(0)
    rows = sub * nchunk
    trows = td * nchunk

    def idx_copy(step, slot):
        return pltpu.make_async_copy(dst_hbm.at[:, pl.ds(step * td, td)], dst_s.at[slot], isem.at[slot])

    def tile_load(step, slot):
        start = pl.multiple_of(step * trows, trows)
        return pltpu.make_async_copy(x_hbm.at[pl.ds(start, trows), :], xbuf.at[slot], lsem.at[slot])

    def wait_rows(slot):
        for _ in range(TOP_K):
            pltpu.make_async_copy(xbuf.at[slot], xbuf.at[slot], dsem.at[slot]).wait()

    @pl.when(i == 0)
    def _():
        idx_copy(0, 0).start()
        tile_load(0, 0).start()
        zbuf[...] = jnp.zeros_like(zbuf)
        pieces = mb // sub

        def zero_copy(e, p):
            start = pl.multiple_of((pend_ref[e] - (p + 1) * sub) * nchunk, nchunk)
            return pltpu.make_async_copy(zbuf, xs_hbm.at[pl.ds(start, rows), :], zsem)

        def needs_zero(e, p):
            return pend_ref[e] - vend_ref[e] > p * sub

        def zstart(e, _):
            for p in range(pieces):
                @pl.when(needs_zero(e, p))
                def _():
                    zero_copy(e, p).start()
            return 0

        def zwait(e, _):
            for p in range(pieces):
                @pl.when(needs_zero(e, p))
                def _():
                    zero_copy(e, p).wait()
            return 0

        lax.fori_loop(0, n_exp, zstart, 0)
        lax.fori_loop(0, n_exp, zwait, 0)

        def tail_copy(b):
            start = pl.multiple_of(b * rows, rows)
            return pltpu.make_async_copy(zbuf, xs_hbm.at[pl.ds(start, rows), :], zsem)

        def tstart(b, _):
            tail_copy(b).start()
            return 0

        def twait(b, _):
            tail_copy(b).wait()
            return 0

        first_unused = pend_ref[n_exp - 1] // sub
        lax.fori_loop(first_unused, n_blocks * pieces, tstart, 0)
        lax.fori_loop(first_unused, n_blocks * pieces, twait, 0)

    for cur in range(3):
        nxt = (cur + 1) % 3

        @pl.when(i % 3 == cur)
        def _():
            idx_copy(i, cur).wait()

            @pl.when(i >= 2)
            def _():
                wait_rows(nxt)

            @pl.when(i + 1 < nsteps)
            def _():
                idx_copy(i + 1, nxt).start()
                tile_load(i + 1, nxt).start()

            tile_load(i, cur).wait()

            def body(r, _):
                src = pl.multiple_of(r * nchunk, nchunk)
                for k in range(TOP_K):
                    dst = pl.multiple_of(dst_s[cur, k, r], nchunk)
                    pltpu.make_async_copy(xbuf.at[cur, pl.ds(src, nchunk), :],
                                          xs_hbm.at[pl.ds(dst, nchunk), :], dsem.at[cur]).start(priority=k % 2)
                return 0

            lax.fori_loop(0, td, body, 0, unroll=2)

            @pl.when(i == nsteps - 1)
            def _():
                @pl.when(i >= 1)
                def _():
                    wait_rows((cur + 2) % 3)

                wait_rows(cur)


def _dispatch(x2, dst_t, pstart, pend, vend, p_rows, td):
    n_exp = pstart.shape[0]
    t = dst_t.shape[1]
    nchunk = x2.shape[0] // t
    nsteps = t // td
    grid_spec = pltpu.PrefetchScalarGridSpec(
        num_scalar_prefetch=3,
        grid=(nsteps,),
        in_specs=[pl.BlockSpec(memory_space=pl.ANY)] * 2,
        out_specs=pl.BlockSpec(memory_space=pl.ANY),
        scratch_shapes=[pltpu.SMEM((3, TOP_K, td), jnp.int32),
                        pltpu.VMEM((3, td * nchunk, LANES), x2.dtype),
                        pltpu.VMEM((MOE_SUB * nchunk, LANES), x2.dtype),
                        pltpu.SemaphoreType.DMA((3,)),
                        pltpu.SemaphoreType.DMA((3,)),
                        pltpu.SemaphoreType.DMA((3,)),
                        pltpu.SemaphoreType.DMA],
    )
    return pl.pallas_call(
        functools.partial(_dispatch_kernel, td=td, nchunk=nchunk, mb=MOE_BLOCK, sub=MOE_SUB, n_exp=n_exp,
                          nsteps=nsteps, n_blocks=p_rows // MOE_BLOCK),
        grid_spec=grid_spec,
        out_shape=jax.ShapeDtypeStruct((p_rows * nchunk, LANES), x2.dtype),
        compiler_params=pltpu.CompilerParams(
            dimension_semantics=("arbitrary",), vmem_limit_bytes=VMEM_LIMIT),
        name="moe_dispatch",
    )(pstart, pend, vend, dst_t, x2)


def _experts_kernel(bexp_ref, nused_ref, bvalid_ref, xs_ref, wg_ref, wu_ref, wd_ref, ys_ref,
                    wg_b, wu_b, wd_b, *, mb, sub, nchunk):
    i = pl.program_id(0)
    half = nchunk // 2
    used = i < nused_ref[0]

    @pl.when(used)
    def _():
        new_expert = jnp.logical_or(i == 0, bexp_ref[i] != bexp_ref[jnp.maximum(i - 1, 0)])

        @pl.when(new_expert)
        def _():
            wg_b[...] = wg_ref[0].astype(BF16)
            wu_b[...] = wu_ref[0].astype(BF16)
            wd_b[...] = wd_ref[0].astype(BF16)

    for p in range(mb // sub):
        live = jnp.logical_and(used, bvalid_ref[i] > p * sub)

        @pl.when(live)
        def _():
            x0 = p * sub * half
            words = [xs_ref[pl.ds(x0 + c, sub, stride=half), :] for c in range(half)]
            lows = [pltpu.bitcast(wv << 16, F32) for wv in words]
            highs = [pltpu.bitcast(wv & jnp.uint32(0xFFFF0000), F32) for wv in words]
            xb = jnp.concatenate(lows + highs, axis=1).astype(BF16)
            gate = _dot(xb, wg_b[...])
            up = _dot(xb, wu_b[...])
            hid = (gate * _sigmoid(gate)) * up
            y = _dot(hid.astype(BF16), wd_b[...])
            y0 = p * sub * nchunk
            for c in range(nchunk):
                ys_ref[pl.ds(y0 + c, sub, stride=nchunk), :] = y[:, c * LANES:(c + 1) * LANES]

        @pl.when(jnp.logical_not(live))
        def _():
            ys_ref[pl.ds(p * sub * nchunk, sub * nchunk), :] = jnp.zeros((sub * nchunk, LANES), F32)


def _experts(xs, block_expert, nused, block_valid, w_gate, w_up, w_down):
    e, d, de = w_gate.shape
    nchunk = d // LANES
    mb = MOE_BLOCK
    rows = mb * nchunk
    xrows = rows // 2
    n_blocks = xs.shape[0] // xrows
    blk = lambda i, be, nu, bv: (jnp.minimum(i, nu[0] - 1), 0)
    wsel = lambda i, be, nu, bv: (be[i], 0, 0)
    grid_spec = pltpu.PrefetchScalarGridSpec(
        num_scalar_prefetch=3,
        grid=(n_blocks,),
        in_specs=[pl.BlockSpec((xrows, LANES), blk),
                  pl.BlockSpec((1, d, de), wsel), pl.BlockSpec((1, d, de), wsel),
                  pl.BlockSpec((1, de, d), wsel)],
        out_specs=pl.BlockSpec((rows, LANES), lambda i, be, nu, bv: (i, 0)),
        scratch_shapes=[pltpu.VMEM((d, de), BF16), pltpu.VMEM((d, de), BF16), pltpu.VMEM((de, d), BF16)],
    )
    return pl.pallas_call(
        functools.partial(_experts_kernel, mb=mb, sub=MOE_SUB, nchunk=nchunk),
        grid_spec=grid_spec,
        out_shape=jax.ShapeDtypeStruct((n_blocks * rows, LANES), F32),
        compiler_params=pltpu.CompilerParams(
            dimension_semantics=("arbitrary",), vmem_limit_bytes=VMEM_LIMIT),
        name="routed_experts",
    )(block_expert, nused, block_valid, xs, w_gate, w_up, w_down)


def _final_kernel(dst_hbm, w_hbm, ys_hbm, x1_ref, wsg_ref, wsu_ref, wsd_ref,
                  lg_ref, lb_ref, o_ref, ybuf, rbuf, dst_s, w_s, isem, gsem,
                  *, tm, nchunk, alpha, nsteps):
    i = pl.program_id(0)
    tok_rows = TOP_K * nchunk

    def idx_copies(step, slot):
        cols = pl.ds(step * tm, tm)
        return (pltpu.make_async_copy(dst_hbm.at[:, cols], dst_s.at[slot], isem.at[slot]),
                pltpu.make_async_copy(w_hbm.at[:, cols], w_s.at[slot], isem.at[slot]))

    def issue_gathers(slot):
        def body(r, _):
            for k in range(TOP_K):
                src = pl.multiple_of(dst_s[slot, k, r], nchunk)
                pltpu.make_async_copy(ys_hbm.at[pl.ds(src, nchunk), :],
                                      ybuf.at[slot, pl.ds(r * tok_rows + k * nchunk, nchunk), :],
                                      gsem.at[slot]).start(priority=k % 2)
            return 0

        lax.fori_loop(0, tm, body, 0, unroll=2)

    def combine(slot):
        def body(r, _):
            acc = ybuf[slot, pl.ds(r * tok_rows, nchunk), :] * w_s[slot, 0, r]
            for k in range(1, TOP_K):
                acc = acc + ybuf[slot, pl.ds(r * tok_rows + k * nchunk, nchunk), :] * w_s[slot, k, r]
            rbuf[pl.ds(r * nchunk, nchunk), :] = acc
            return 0

        lax.fori_loop(0, tm, body, 0, unroll=2)

    @pl.when(i == 0)
    def _():
        for cp in idx_copies(0, 0):
            cp.start()
        for cp in idx_copies(0, 0):
            cp.wait()
        issue_gathers(0)

        @pl.when(nsteps > 1)
        def _():
            for cp in idx_copies(1, 1):
                cp.start()

    for cur in range(2):
        nxt = 1 - cur

        @pl.when(i % 2 == cur)
        def _():
            @pl.when(i + 1 < nsteps)
            def _():
                for cp in idx_copies(i + 1, nxt):
                    cp.wait()
                issue_gathers(nxt)

            pltpu.make_async_copy(ybuf.at[cur], ybuf.at[cur], gsem.at[cur]).wait()
            combine(cur)

            @pl.when(i + 2 < nsteps)
            def _():
                for cp in idx_copies(i + 2, cur):
                    cp.start()

    routed = jnp.concatenate([rbuf[pl.ds(c, tm, stride=nchunk), :] for c in range(nchunk)], axis=1)

    x1 = x1_ref[...]
    xb = x1.astype(BF16)
    gate = _dot(xb, wsg_ref[...])
    up = _dot(xb, wsu_ref[...])
    shared = _dot(((gate * _sigmoid(gate)) * up).astype(BF16), wsd_ref[...])
    o_ref[...] = _layer_norm(alpha * x1 + (routed + shared), lg_ref[...], lb_ref[...])


def _final(ys, dst_t, w_t, x1, wsg, wsu, wsd, lg, lb, alpha, tm):
    t, d = x1.shape
    nchunk = d // LANES
    ds = wsg.shape[1]
    nsteps = t // tm
    full = lambda i: (0, 0)
    return pl.pallas_call(
        functools.partial(_final_kernel, tm=tm, nchunk=nchunk, alpha=alpha, nsteps=nsteps),
        grid=(nsteps,),
        in_specs=[pl.BlockSpec(memory_space=pl.ANY)] * 3 + [
            pl.BlockSpec((tm, d), lambda i: (i, 0)),
            pl.BlockSpec((d, ds), full), pl.BlockSpec((d, ds), full), pl.BlockSpec((ds, d), full),
            pl.BlockSpec((1, d), full), pl.BlockSpec((1, d), full)],
        out_specs=pl.BlockSpec((tm, d), lambda i: (i, 0)),
        out_shape=jax.ShapeDtypeStruct((t, d), F32),
        scratch_shapes=[pltpu.VMEM((2, tm * TOP_K * nchunk, LANES), F32),
                        pltpu.VMEM((tm * nchunk, LANES), F32),
                        pltpu.SMEM((2, TOP_K, tm), jnp.int32),
                        pltpu.SMEM((2, TOP_K, tm), F32),
                        pltpu.SemaphoreType.DMA((2,)),
                        pltpu.SemaphoreType.DMA((2,))],
        compiler_params=pltpu.CompilerParams(
            dimension_semantics=("arbitrary",), vmem_limit_bytes=VMEM_LIMIT),
        name="combine_shared_ln2",
    )(dst_t, w_t, ys, x1, wsg, wsu, wsd, lg, lb)


def _block_tables(counts, n_rows):
    mb = MOE_BLOCK
    n_exp = counts.shape[0]
    padded = (counts + mb - 1) // mb * mb
    pad_end = jnp.cumsum(padded)
    pad_start = pad_end - padded
    valid_end = pad_start + counts
    n_blocks = n_rows // mb
    first_row = jnp.arange(n_blocks, dtype=jnp.int32) * mb
    block_expert = jnp.minimum(
        jnp.sum((pad_end[None, :] <= first_row[:, None]).astype(jnp.int32), axis=1), n_exp - 1)
    block_valid = jnp.clip(valid_end[block_expert] - first_row, 0, mb)
    nused = (pad_end[-1:] // mb).astype(jnp.int32)
    i32 = lambda a: a.astype(jnp.int32)
    return i32(pad_start), i32(pad_end), i32(valid_end), block_expert, i32(block_valid), nused


def _tile(n, pref):
    while n % pref:
        pref //= 2
    return pref


def kernel(x, w_in, lower_bounds, hg_norm_g, w_branch_hg, w_branch_sb, w_out, ln1_g, ln1_b,
           w_router, router_bias, w_exp_gate, w_exp_up, w_exp_down,
           w_sh_gate, w_sh_up, w_sh_down, ln2_g, ln2_b):
    depth = w_in.shape[0]
    assert depth == 1, "single-layer block only"
    batch, seq, d = x.shape
    t = batch * seq
    n_exp = w_router.shape[-1]
    alpha = (2.0 * depth) ** 0.25
    nchunk = d // LANES

    lb = jnp.cumsum(jax.nn.softmax(lower_bounds.astype(F32), axis=0), axis=0)[0]
    xf = x.reshape(t, d)

    w = w_in[0].astype(BF16)
    c = d
    o_hg = _hgrn(xf, w[:, 0:4 * c], lb, hg_norm_g[0].astype(F32), batch, seq, _tile(seq, 256))
    pbf = _matmul(xf, w[:, 4 * c:7 * c], BF16, _tile(t, 1024), 512)
    o_sb = _sb_attention(pbf, batch, seq, _tile(seq, 256), 2)

    x1, x1s = _merge(o_hg, o_sb, xf, w[:, 7 * c:9 * c],
                     w_branch_hg[0].astype(BF16), w_branch_sb[0].astype(BF16), w_out[0].astype(BF16),
                     ln1_g[0].reshape(1, d).astype(F32), ln1_b[0].reshape(1, d).astype(F32),
                     alpha, _tile(t, 512))

    idx_t, rank_t, w_t, cnt = _route(x1, w_router[0], router_bias[0], _tile(t, 256))
    p_rows = t * TOP_K + n_exp * MOE_BLOCK
    pstart, pend, vend, block_expert, block_valid, nused = _block_tables(cnt[:, 0].astype(jnp.int32), p_rows)
    dstx_t, dsty_t = _dest_rows(idx_t, rank_t, pstart, nchunk, _tile(t, 256))
    xs = _dispatch(x1s, dstx_t, pstart, pend, vend, p_rows, _tile(t, 128))
    ys = _experts(xs, block_expert, nused, block_valid, w_exp_gate[0], w_exp_up[0], w_exp_down[0])
    out = _final(ys, dsty_t, w_t, x1,
                 w_sh_gate[0].astype(BF16), w_sh_up[0].astype(BF16), w_sh_down[0].astype(BF16),
                 ln2_g[0].reshape(1, d).astype(F32), ln2_b[0].reshape(1, d).astype(F32),
                 alpha, _tile(t, 128))
    return out.reshape(batch, seq, d)
```

```python
import functools

import jax
import jax.numpy as jnp
from jax import lax
from jax.experimental import pallas as pl
from jax.experimental.pallas import tpu as pltpu

F32 = jnp.float32
BF16 = jnp.bfloat16

LANES = 128
SUBLANES = 8
VMEM_LIMIT = 48 * 1024 * 1024

HG_HEADS = 8
HG_CHUNK = 32
SB_HEADS = 8
N_GROUPS = 8
TOPK_GROUPS = 4
TOP_K = 8
ROUTED_SCALE = 2.5
MOE_BLOCK = 512
MOE_SUB = 512
LN_EPS = 1e-5
RMS_EPS = 1e-6
STICK_DEAD = 110.0

NT_DIMS = (((1,), (1,)), ((), ()))
TN_DIMS = (((0,), (0,)), ((), ()))


def _dot(a, b):
    return jnp.dot(a, b, preferred_element_type=F32)


def _sigmoid(x):
    return 1.0 / (1.0 + jnp.exp(-x))


def _split2(x):
    hi = x.astype(BF16)
    lo = (x - hi.astype(F32)).astype(BF16)
    return hi, lo


def _layer_norm(r, g, b):
    mu = jnp.mean(r, axis=-1, keepdims=True)
    d = r - mu
    var = jnp.mean(d * d, axis=-1, keepdims=True)
    return d * lax.rsqrt(var + LN_EPS) * g + b


def _mm_kernel(x_ref, w_ref, o_ref):
    o_ref[...] = _dot(x_ref[...].astype(BF16), w_ref[...]).astype(o_ref.dtype)


def _matmul(x, w, out_dtype, tm, tn):
    m, k = x.shape
    n = w.shape[1]
    return pl.pallas_call(
        _mm_kernel,
        grid=(m // tm, n // tn),
        in_specs=[pl.BlockSpec((tm, k), lambda i, j: (i, 0)),
                  pl.BlockSpec((k, tn), lambda i, j: (0, j))],
        out_specs=pl.BlockSpec((tm, tn), lambda i, j: (i, j)),
        out_shape=jax.ShapeDtypeStruct((m, n), out_dtype),
        compiler_params=pltpu.CompilerParams(
            dimension_semantics=("parallel", "arbitrary"), vmem_limit_bytes=VMEM_LIMIT),
        name="in_proj",
    )(x, w)


def _hgrn_kernel(x_ref, w_ref, lb_ref, g_ref, o_ref, st_ref, *, ts, chunk, heads):
    @pl.when(pl.program_id(1) == 0)
    def _():
        st_ref[...] = jnp.zeros_like(st_ref)

    shift = chunk.bit_length() - 1
    row = lax.broadcasted_iota(jnp.int32, (ts, ts), 0)
    col = lax.broadcasted_iota(jnp.int32, (ts, ts), 1)
    same = (row >> shift) == (col >> shift)
    causal = jnp.logical_and(same, col <= row)
    tri = jnp.where(causal, 1.0, 0.0).astype(BF16)
    ones = jnp.where(same, 1.0, 0.0).astype(BF16)
    d = LANES
    hd = heads * d

    xb = x_ref[...].astype(BF16)
    hq_all = _dot(xb, w_ref[:, 0:hd])
    hf_all = _dot(xb, w_ref[:, hd:2 * hd])
    hi_all = _dot(xb, w_ref[:, 2 * hd:3 * hd]).astype(BF16)
    hg_all = _dot(xb, w_ref[:, 3 * hd:4 * hd])

    for hh in range(heads):
        cols = slice(hh * d, (hh + 1) * d)
        lb = lb_ref[:, cols]
        hf = hf_all[:, cols]
        log_f = jnp.log(lb + (1.0 - lb) * _sigmoid(hf))
        k_in = (1.0 - lb) * _sigmoid(-hf)

        parts = jnp.concatenate(_split2(log_f), axis=1)
        cs = _dot(tri, parts)
        tot = _dot(ones, parts)
        b = cs[:, :d] + cs[:, d:]
        b_end = tot[:, :d] + tot[:, d:]

        q = hq_all[:, cols]
        q_start = (q * jnp.exp(b)).astype(BF16)
        q_end = (q * jnp.exp(b - b_end)).astype(BF16)
        k_end = (k_in * jnp.exp(b_end - b)).astype(BF16)
        v = hi_all[:, cols]

        scores = lax.dot_general(q_end, k_end, NT_DIMS, preferred_element_type=F32)
        scores = jnp.where(causal, scores, 0.0)
        o_intra = _dot(scores.astype(BF16), v)

        dec = jnp.exp(b_end)
        st = st_ref[hh]
        outs = []
        for c in range(ts // chunk):
            lo = c * chunk
            outs.append(lax.dot_general(q_start[lo:lo + chunk], st.astype(BF16), NT_DIMS,
                                        preferred_element_type=F32))
            kv = lax.dot_general(v[lo:lo + chunk], k_end[lo:lo + chunk], TN_DIMS,
                                 preferred_element_type=F32)
            st = dec[lo:lo + 1, :] * st + kv
        st_ref[hh] = st
        o = o_intra + jnp.concatenate(outs, axis=0)

        o = o * lax.rsqrt(jnp.mean(o * o, axis=-1, keepdims=True) + RMS_EPS)
        o = o * g_ref[:, cols]
        hg = hg_all[:, cols]
        o_ref[:, cols] = (o * (hg * _sigmoid(hg))).astype(o_ref.dtype)


def _hgrn(x, w_hg, lb, g, batch, seq, ts):
    t, dm = x.shape
    h = HG_HEADS
    hd = h * LANES
    ns = seq // ts
    full = lambda b, s: (0, 0)
    return pl.pallas_call(
        functools.partial(_hgrn_kernel, ts=ts, chunk=HG_CHUNK, heads=h),
        grid=(batch, ns),
        in_specs=[pl.BlockSpec((ts, dm), lambda b, s: (b * ns + s, 0)),
                  pl.BlockSpec((dm, 4 * hd), full),
                  pl.BlockSpec((1, hd), full),
                  pl.BlockSpec((1, hd), full)],
        out_specs=pl.BlockSpec((ts, hd), lambda b, s: (b * ns + s, 0)),
        out_shape=jax.ShapeDtypeStruct((t, hd), BF16),
        scratch_shapes=[pltpu.VMEM((h, LANES, LANES), F32)],
        compiler_params=pltpu.CompilerParams(
            dimension_semantics=("parallel", "arbitrary"), vmem_limit_bytes=VMEM_LIMIT),
        name="hgrn2",
    )(x, w_hg, lb.reshape(1, hd), g.reshape(1, hd))


def _sb_kernel(q_ref, k_ref, v_ref, u_ref, o_ref, *, tq, scale, hp):
    i = pl.program_id(2)
    u = u_ref[...]
    rep = tq // LANES
    d = LANES

    def block(hh, j, carry, acc, masked, live=None):
        cols = slice(hh * d, (hh + 1) * d)
        start = pl.multiple_of(j * tq, tq)
        q = q_ref[:, cols]
        kj = k_ref[pl.ds(start, tq), cols]
        vj = v_ref[pl.ds(start, tq), cols]
        z = lax.dot_general(q, kj, NT_DIMS, preferred_element_type=F32) * scale
        sp = jnp.maximum(z, 0.0) + jnp.log(1.0 + jnp.exp(-jnp.abs(z)))
        if masked:
            row = lax.broadcasted_iota(jnp.int32, (tq, tq), 0)
            col = lax.broadcasted_iota(jnp.int32, (tq, tq), 1)
            before = col < row
            drop = jnp.where(before, sp, 0.0)
        elif live is not None:
            drop = sp * live
        else:
            drop = sp
        hi = drop.astype(BF16)
        lo = (drop - hi.astype(F32)).astype(BF16)
        later = _dot(hi, u) + _dot(lo, u)
        stick = later + jnp.concatenate([carry] * rep, axis=1)
        w = jnp.exp((z - sp) - stick)
        if masked:
            w = jnp.where(before, w, 0.0)
        elif live is not None:
            w = w * live
        acc = acc + _dot(w.astype(BF16), vj)
        carry = carry + jnp.broadcast_to(later[:, 0:1] + drop[:, 0:1], (tq, LANES))
        return carry, acc

    def some_row_alive(carries):
        m = jnp.min(carries[0])
        for c in carries[1:]:
            m = jnp.minimum(m, jnp.min(c))
        return (m < STICK_DEAD).astype(jnp.int32)

    zero = jnp.zeros((tq, LANES), F32)
    live = (i > 0).astype(F32)
    prev = jnp.maximum(i - 1, 0)
    carries, accs = [], []
    for hh in range(hp):
        carry, acc = block(hh, i, zero, zero, True)
        carry, acc = block(hh, prev, carry, acc, False, live=live)
        carries.append(carry)
        accs.append(acc)

    def cond(state):
        return jnp.logical_and(state[0] >= 0, state[1] > 0)

    def body(state):
        j, _, carries, accs = state
        out = [block(hh, j, carries[hh], accs[hh], False) for hh in range(hp)]
        carries = tuple(o[0] for o in out)
        accs = tuple(o[1] for o in out)
        return j - 1, some_row_alive(carries), carries, accs

    _, _, carries, accs = lax.while_loop(
        cond, body, (i - 2, some_row_alive(carries), tuple(carries), tuple(accs)))
    for hh in range(hp):
        o_ref[:, hh * d:(hh + 1) * d] = accs[hh].astype(o_ref.dtype)


def _sb_attention(pbf, batch, seq, tq, hp):
    t = batch * seq
    h = SB_HEADS
    ng = h // hp
    w = hp * LANES
    nq = seq // tq
    r = jnp.arange(tq)
    u = (r[:, None] > r[None, :]).astype(BF16)
    return pl.pallas_call(
        functools.partial(_sb_kernel, tq=tq, scale=float(LANES) ** -0.5, hp=hp),
        grid=(batch, ng, nq),
        in_specs=[pl.BlockSpec((tq, w), lambda b, hg, i: (b * nq + i, hg)),
                  pl.BlockSpec((seq, w), lambda b, hg, i: (b, ng + hg)),
                  pl.BlockSpec((seq, w), lambda b, hg, i: (b, 2 * ng + hg)),
                  pl.BlockSpec((tq, tq), lambda b, hg, i: (0, 0))],
        out_specs=pl.BlockSpec((tq, w), lambda b, hg, i: (b * nq + i, hg)),
        out_shape=jax.ShapeDtypeStruct((t, h * LANES), BF16),
        compiler_params=pltpu.CompilerParams(
            dimension_semantics=("parallel", "parallel", "arbitrary"), vmem_limit_bytes=VMEM_LIMIT),
        name="stick_breaking",
    )(pbf, pbf, pbf, u)


def _merge_kernel(ohg_ref, osb_ref, x_ref, wgate_ref, wbh_ref, wbs_ref, wo_ref,
                  lg_ref, lb_ref, x1_ref, x1s_ref, *, alpha, tm, nchunk):
    x = x_ref[...]
    xb = x.astype(BF16)
    d = x.shape[1]
    g_hg = _dot(xb, wgate_ref[:, 0:d])
    g_sb = _dot(xb, wgate_ref[:, d:2 * d])
    y_hg = _dot(ohg_ref[...], wbh_ref[...])
    y_sb = _dot(osb_ref[...], wbs_ref[...])
    merged = _sigmoid(g_hg) * y_hg + _sigmoid(g_sb) * y_sb
    hmix = _dot(merged.astype(BF16), wo_ref[...])
    x1 = _layer_norm(alpha * x + hmix, lg_ref[...], lb_ref[...])
    x1_ref[...] = x1
    half = nchunk // 2
    for c in range(half):
        lo = x1[:, c * LANES:(c + 1) * LANES].astype(BF16).astype(F32)
        hi = x1[:, (c + half) * LANES:(c + half + 1) * LANES].astype(BF16).astype(F32)
        word = pltpu.bitcast(hi, jnp.uint32) | (pltpu.bitcast(lo, jnp.uint32) >> 16)
        x1s_ref[pl.ds(c, tm, stride=half), :] = word


def _merge(o_hg, o_sb, x, wgate, wbh, wbs, wo, lg, lb, alpha, tm):
    t, d = x.shape
    nchunk = d // LANES
    row = lambda off: (lambda i: (i, off))
    full = lambda i: (0, 0)
    return pl.pallas_call(
        functools.partial(_merge_kernel, alpha=alpha, tm=tm, nchunk=nchunk),
        grid=(t // tm,),
        in_specs=[pl.BlockSpec((tm, d), row(0)), pl.BlockSpec((tm, d), row(0)),
                  pl.BlockSpec((tm, d), row(0)),
                  pl.BlockSpec((d, 2 * d), full),
                  pl.BlockSpec((d, d), full), pl.BlockSpec((d, d), full), pl.BlockSpec((d, d), full),
                  pl.BlockSpec((1, d), full), pl.BlockSpec((1, d), full)],
        out_specs=[pl.BlockSpec((tm, d), row(0)), pl.BlockSpec((tm * nchunk // 2, LANES), row(0))],
        out_shape=[jax.ShapeDtypeStruct((t, d), F32),
                   jax.ShapeDtypeStruct((t * nchunk // 2, LANES), jnp.uint32)],
        compiler_params=pltpu.CompilerParams(
            dimension_semantics=("parallel",), vmem_limit_bytes=VMEM_LIMIT),
        name="merge_ln1",
    )(o_hg, o_sb, x, wgate, wbh, wbs, wo, lg, lb)


def _route_kernel(x_ref, wh_ref, wl_ref, bias_ref, u_ref, ones_ref,
                  idx_ref, rank_ref, w_ref, cnt_ref, carry, *, tile, n_exp):
    @pl.when(pl.program_id(0) == 0)
    def _():
        carry[...] = jnp.zeros_like(carry)

    x = x_ref[...]
    xh = x.astype(BF16)
    xl = (x - xh.astype(F32)).astype(BF16)
    wh = wh_ref[...]
    nt = functools.partial(lax.dot_general, dimension_numbers=NT_DIMS, preferred_element_type=F32)
    logits = nt(wh, xh) + nt(wh, xl) + nt(wl_ref[...], xh)
    scores = _sigmoid(logits)
    biased = scores + bias_ref[...]

    neg_inf = -jnp.inf
    gsz = n_exp // N_GROUPS
    groups = [biased[g * gsz:(g + 1) * gsz] for g in range(N_GROUPS)]
    gscore = []
    for xg in groups:
        m1 = jnp.max(xg, axis=0, keepdims=True)
        n1 = jnp.sum(jnp.where(xg == m1, 1.0, 0.0), axis=0, keepdims=True)
        m2 = jnp.max(jnp.where(xg < m1, xg, neg_inf), axis=0, keepdims=True)
        gscore.append(m1 + jnp.where(n1 >= 2.0, m1, m2))
    masked = []
    for g in range(N_GROUPS):
        beaten_by = jnp.zeros_like(gscore[g])
        for h in range(N_GROUPS):
            if h < g:
                beaten_by = beaten_by + jnp.where(gscore[h] >= gscore[g], 1.0, 0.0)
            elif h > g:
                beaten_by = beaten_by + jnp.where(gscore[h] > gscore[g], 1.0, 0.0)
        keep = jnp.broadcast_to(beaten_by, (gsz, tile)) < float(TOPK_GROUPS)
        masked.append(jnp.where(keep, groups[g], jnp.finfo(F32).min))
    cur = jnp.concatenate(masked, axis=0)

    eid = lax.broadcasted_iota(jnp.int32, (n_exp, tile), 0).astype(F32)
    chosen = jnp.zeros((n_exp, tile), F32)
    idxs, ws = [], []
    for _ in range(TOP_K):
        m = jnp.max(cur, axis=0, keepdims=True)
        ik = jnp.min(jnp.where(cur == m, eid, float(n_exp)), axis=0, keepdims=True)
        sel = eid == ik
        idxs.append(ik)
        ws.append(jnp.sum(jnp.where(sel, scores, 0.0), axis=0, keepdims=True))
        chosen = chosen + jnp.where(sel, 1.0, 0.0)
        cur = jnp.where(sel, neg_inf, cur)

    chosen_b = chosen.astype(BF16)
    run = carry[...]
    rank = _dot(chosen_b, u_ref[...]) + jnp.concatenate([run] * (tile // LANES), axis=1)
    ranks = [jnp.sum(jnp.where(eid == ik, rank, 0.0), axis=0, keepdims=True) for ik in idxs]
    run = run + _dot(chosen_b, ones_ref[...])
    carry[...] = run
    cnt_ref[...] = run

    wsum = ws[0]
    for wk in ws[1:]:
        wsum = wsum + wk
    idx_ref[...] = jnp.concatenate(idxs, axis=0).astype(jnp.int32)
    rank_ref[...] = jnp.concatenate(ranks, axis=0).astype(jnp.int32)
    w_ref[...] = jnp.concatenate([wk / wsum * ROUTED_SCALE for wk in ws], axis=0)


def _route(x1, w_router, router_bias, tile):
    t, d = x1.shape
    n_exp = w_router.shape[1]
    wt = w_router.astype(F32).T
    wh = wt.astype(BF16)
    wl = (wt - wh.astype(F32)).astype(BF16)
    r = jnp.arange(tile)
    u = (r[:, None] < r[None, :]).astype(BF16)
    ones = jnp.ones((tile, LANES), BF16)
    full = lambda i: (0, 0)
    tok = lambda i: (0, i)
    return pl.pallas_call(
        functools.partial(_route_kernel, tile=tile, n_exp=n_exp),
        grid=(t // tile,),
        in_specs=[pl.BlockSpec((tile, d), lambda i: (i, 0)),
                  pl.BlockSpec((n_exp, d), full), pl.BlockSpec((n_exp, d), full),
                  pl.BlockSpec((n_exp, 1), full),
                  pl.BlockSpec((tile, tile), full), pl.BlockSpec((tile, LANES), full)],
        out_specs=[pl.BlockSpec((TOP_K, tile), tok), pl.BlockSpec((TOP_K, tile), tok),
                   pl.BlockSpec((TOP_K, tile), tok), pl.BlockSpec((n_exp, LANES), full)],
        out_shape=[jax.ShapeDtypeStruct((TOP_K, t), jnp.int32), jax.ShapeDtypeStruct((TOP_K, t), jnp.int32),
                   jax.ShapeDtypeStruct((TOP_K, t), F32), jax.ShapeDtypeStruct((n_exp, LANES), F32)],
        scratch_shapes=[pltpu.VMEM((n_exp, LANES), F32)],
        compiler_params=pltpu.CompilerParams(
            dimension_semantics=("arbitrary",), vmem_limit_bytes=VMEM_LIMIT),
        name="router_topk",
    )(x1, wh, wl, router_bias.astype(F32).reshape(n_exp, 1), u, ones)


def _dest_kernel(idx_ref, rank_ref, pstart_ref, dstx_ref, dsty_ref, *, tile, n_exp, nchunk):
    eid = lax.broadcasted_iota(jnp.int32, (n_exp, tile), 0)
    pstart = jnp.broadcast_to(pstart_ref[...], (n_exp, tile))
    rows = []
    for k in range(TOP_K):
        hit = eid == idx_ref[k:k + 1, :]
        base = jnp.sum(jnp.where(hit, pstart, 0.0), axis=0, keepdims=True)
        rows.append(base.astype(jnp.int32) + rank_ref[k:k + 1, :])
    dst = jnp.concatenate(rows, axis=0)
    dstx_ref[...] = dst * (nchunk // 2)
    dsty_ref[...] = dst * nchunk


def _dest_rows(idx_t, rank_t, pstart, nchunk, tile):
    t = idx_t.shape[1]
    n_exp = pstart.shape[0]
    tok = lambda i: (0, i)
    return pl.pallas_call(
        functools.partial(_dest_kernel, tile=tile, n_exp=n_exp, nchunk=nchunk),
        grid=(t // tile,),
        in_specs=[pl.BlockSpec((TOP_K, tile), tok), pl.BlockSpec((TOP_K, tile), tok),
                  pl.BlockSpec((n_exp, 1), lambda i: (0, 0))],
        out_specs=[pl.BlockSpec((TOP_K, tile), tok)] * 2,
        out_shape=[jax.ShapeDtypeStruct((TOP_K, t), jnp.int32)] * 2,
        compiler_params=pltpu.CompilerParams(
            dimension_semantics=("parallel",), vmem_limit_bytes=VMEM_LIMIT),
        name="dest_rows",
    )(idx_t, rank_t, pstart.astype(F32).reshape(n_exp, 1))


def _dispatch_kernel(pstart_ref, pend_ref, vend_ref, dst_hbm, x_hbm, xs_hbm,
                     dst_s, xbuf, zbuf, isem, lsem, dsem, zsem,
                     *, td, nchunk, mb, sub, n_exp, nsteps, n_blocks):
    i = pl.program_id(0)
    rows = sub * nchunk
    trows = td * nchunk

    def idx_copy(step, slot):
        return pltpu.make_async_copy(dst_hbm.at[:, pl.ds(step * td, td)], dst_s.at[slot], isem.at[slot])

    def tile_load(step, slot):
        start = pl.multiple_of(step * trows, trows)
        return pltpu.make_async_copy(x_hbm.at[pl.ds(start, trows), :], xbuf.at[slot], lsem.at[slot])

    def wait_rows(slot):
        for _ in range(TOP_K):
            pltpu.make_async_copy(xbuf.at[slot], xbuf.at[slot], dsem.at[slot]).wait()

    @pl.when(i == 0)
    def _():
        idx_copy(0, 0).start()
        tile_load(0, 0).start()
        zbuf[...] = jnp.zeros_like(zbuf)
        pieces = mb // sub

        def zero_copy(e, p):
            start = pl.multiple_of((pend_ref[e] - (p + 1) * sub) * nchunk, nchunk)
            return pltpu.make_async_copy(zbuf, xs_hbm.at[pl.ds(start, rows), :], zsem)

        def needs_zero(e, p):
            return pend_ref[e] - vend_ref[e] > p * sub

        def zstart(e, _):
            for p in range(pieces):
                @pl.when(needs_zero(e, p))
                def _():
                    zero_copy(e, p).start()
            return 0

        def zwait(e, _):
            for p in range(pieces):
                @pl.when(needs_zero(e, p))
                def _():
                    zero_copy(e, p).wait()
            return 0

        lax.fori_loop(0, n_exp, zstart, 0)
        lax.fori_loop(0, n_exp, zwait, 0)

        def tail_copy(b):
            start = pl.multiple_of(b * rows, rows)
            return pltpu.make_async_copy(zbuf, xs_hbm.at[pl.ds(start, rows), :], zsem)

        def tstart(b, _):
            tail_copy(b).start()
            return 0

        def twait(b, _):
            tail_copy(b).wait()
            return 0

        first_unused = pend_ref[n_exp - 1] // sub
        lax.fori_loop(first_unused, n_blocks * pieces, tstart, 0)
        lax.fori_loop(first_unused, n_blocks * pieces, twait, 0)

    for cur in range(3):
        nxt = (cur + 1) % 3

        @pl.when(i % 3 == cur)
        def _():
            idx_copy(i, cur).wait()

            @pl.when(i >= 2)
            def _():
                wait_rows(nxt)

            @pl.when(i + 1 < nsteps)
            def _():
                idx_copy(i + 1, nxt).start()
                tile_load(i + 1, nxt).start()

            tile_load(i, cur).wait()

            def body(r, _):
                src = pl.multiple_of(r * nchunk, nchunk)
                for k in range(TOP_K):
                    dst = pl.multiple_of(dst_s[cur, k, r], nchunk)
                    pltpu.make_async_copy(xbuf.at[cur, pl.ds(src, nchunk), :],
                                          xs_hbm.at[pl.ds(dst, nchunk), :], dsem.at[cur]).start(priority=k % 2)
                return 0

            lax.fori_loop(0, td, body, 0, unroll=2)

            @pl.when(i == nsteps - 1)
            def _():
                @pl.when(i >= 1)
                def _():
                    wait_rows((cur + 2) % 3)

                wait_rows(cur)


def _dispatch(x2, dst_t, pstart, pend, vend, p_rows, td):
    n_exp = pstart.shape[0]
    t = dst_t.shape[1]
    nchunk = x2.shape[0] // t
    nsteps = t // td
    grid_spec = pltpu.PrefetchScalarGridSpec(
        num_scalar_prefetch=3,
        grid=(nsteps,),
        in_specs=[pl.BlockSpec(memory_space=pl.ANY)] * 2,
        out_specs=pl.BlockSpec(memory_space=pl.ANY),
        scratch_shapes=[pltpu.SMEM((3, TOP_K, td), jnp.int32),
                        pltpu.VMEM((3, td * nchunk, LANES), x2.dtype),
                        pltpu.VMEM((MOE_SUB * nchunk, LANES), x2.dtype),
                        pltpu.SemaphoreType.DMA((3,)),
                        pltpu.SemaphoreType.DMA((3,)),
                        pltpu.SemaphoreType.DMA((3,)),
                        pltpu.SemaphoreType.DMA],
    )
    return pl.pallas_call(
        functools.partial(_dispatch_kernel, td=td, nchunk=nchunk, mb=MOE_BLOCK, sub=MOE_SUB, n_exp=n_exp,
                          nsteps=nsteps, n_blocks=p_rows // MOE_BLOCK),
        grid_spec=grid_spec,
        out_shape=jax.ShapeDtypeStruct((p_rows * nchunk, LANES), x2.dtype),
        compiler_params=pltpu.CompilerParams(
            dimension_semantics=("arbitrary",), vmem_limit_bytes=VMEM_LIMIT),
        name="moe_dispatch",
    )(pstart, pend, vend, dst_t, x2)


def _experts_kernel(bexp_ref, nused_ref, bvalid_ref, xs_ref, wg_ref, wu_ref, wd_ref, ys_ref,
                    wg_b, wu_b, wd_b, *, mb, sub, nchunk):
    i = pl.program_id(0)
    half = nchunk // 2
    used = i < nused_ref[0]

    @pl.when(used)
    def _():
        new_expert = jnp.logical_or(i == 0, bexp_ref[i] != bexp_ref[jnp.maximum(i - 1, 0)])

        @pl.when(new_expert)
        def _():
            wg_b[...] = wg_ref[0].astype(BF16)
            wu_b[...] = wu_ref[0].astype(BF16)
            wd_b[...] = wd_ref[0].astype(BF16)

    for p in range(mb // sub):
        live = jnp.logical_and(used, bvalid_ref[i] > p * sub)

        @pl.when(live)
        def _():
            x0 = p * sub * half
            words = [xs_ref[pl.ds(x0 + c, sub, stride=half), :] for c in range(half)]
            lows = [pltpu.bitcast(wv << 16, F32) for wv in words]
            highs = [pltpu.bitcast(wv & jnp.uint32(0xFFFF0000), F32) for wv in words]
            xb = jnp.concatenate(lows + highs, axis=1).astype(BF16)
            gate = _dot(xb, wg_b[...])
            up = _dot(xb, wu_b[...])
            hid = (gate * _sigmoid(gate)) * up
            y = _dot(hid.astype(BF16), wd_b[...])
            y0 = p * sub * nchunk
            for c in range(nchunk):
                ys_ref[pl.ds(y0 + c, sub, stride=nchunk), :] = y[:, c * LANES:(c + 1) * LANES]

        @pl.when(jnp.logical_not(live))
        def _():
            ys_ref[pl.ds(p * sub * nchunk, sub * nchunk), :] = jnp.zeros((sub * nchunk, LANES), F32)


def _experts(xs, block_expert, nused, block_valid, w_gate, w_up, w_down):
    e, d, de = w_gate.shape
    nchunk = d // LANES
    mb = MOE_BLOCK
    rows = mb * nchunk
    xrows = rows // 2
    n_blocks = xs.shape[0] // xrows
    blk = lambda i, be, nu, bv: (jnp.minimum(i, nu[0] - 1), 0)
    wsel = lambda i, be, nu, bv: (be[i], 0, 0)
    grid_spec = pltpu.PrefetchScalarGridSpec(
        num_scalar_prefetch=3,
        grid=(n_blocks,),
        in_specs=[pl.BlockSpec((xrows, LANES), blk),
                  pl.BlockSpec((1, d, de), wsel), pl.BlockSpec((1, d, de), wsel),
                  pl.BlockSpec((1, de, d), wsel)],
        out_specs=pl.BlockSpec((rows, LANES), lambda i, be, nu, bv: (i, 0)),
        scratch_shapes=[pltpu.VMEM((d, de), BF16), pltpu.VMEM((d, de), BF16), pltpu.VMEM((de, d), BF16)],
    )
    return pl.pallas_call(
        functools.partial(_experts_kernel, mb=mb, sub=MOE_SUB, nchunk=nchunk),
        grid_spec=grid_spec,
        out_shape=jax.ShapeDtypeStruct((n_blocks * rows, LANES), F32),
        compiler_params=pltpu.CompilerParams(
            dimension_semantics=("arbitrary",), vmem_limit_bytes=VMEM_LIMIT),
        name="routed_experts",
    )(block_expert, nused, block_valid, xs, w_gate, w_up, w_down)


def _final_kernel(dst_hbm, w_hbm, ys_hbm, x1_ref, wsg_ref, wsu_ref, wsd_ref,
                  lg_ref, lb_ref, o_ref, ybuf, rbuf, dst_s, w_s, isem, gsem,
                  *, tm, nchunk, alpha, nsteps):
    i = pl.program_id(0)
    tok_rows = TOP_K * nchunk

    def idx_copies(step, slot):
        cols = pl.ds(step * tm, tm)
        return (pltpu.make_async_copy(dst_hbm.at[:, cols], dst_s.at[slot], isem.at[slot]),
                pltpu.make_async_copy(w_hbm.at[:, cols], w_s.at[slot], isem.at[slot]))

    def issue_gathers(slot):
        def body(r, _):
            for k in range(TOP_K):
                src = pl.multiple_of(dst_s[slot, k, r], nchunk)
                pltpu.make_async_copy(ys_hbm.at[pl.ds(src, nchunk), :],
                                      ybuf.at[slot, pl.ds(r * tok_rows + k * nchunk, nchunk), :],
                                      gsem.at[slot]).start(priority=k % 2)
            return 0

        lax.fori_loop(0, tm, body, 0, unroll=2)

    def combine(slot):
        def body(r, _):
            acc = ybuf[slot, pl.ds(r * tok_rows, nchunk), :] * w_s[slot, 0, r]
            for k in range(1, TOP_K):
                acc = acc + ybuf[slot, pl.ds(r * tok_rows + k * nchunk, nchunk), :] * w_s[slot, k, r]
            rbuf[pl.ds(r * nchunk, nchunk), :] = acc
            return 0

        lax.fori_loop(0, tm, body, 0, unroll=2)

    @pl.when(i == 0)
    def _():
        for cp in idx_copies(0, 0):
            cp.start()
        for cp in idx_copies(0, 0):
            cp.wait()
        issue_gathers(0)

        @pl.when(nsteps > 1)
        def _():
            for cp in idx_copies(1, 1):
                cp.start()

    for cur in range(2):
        nxt = 1 - cur

        @pl.when(i % 2 == cur)
        def _():
            @pl.when(i + 1 < nsteps)
            def _():
                for cp in idx_copies(i + 1, nxt):
                    cp.wait()
                issue_gathers(nxt)

            pltpu.make_async_copy(ybuf.at[cur], ybuf.at[cur], gsem.at[cur]).wait()
            combine(cur)

            @pl.when(i + 2 < nsteps)
            def _():
                for cp in idx_copies(i + 2, cur):
                    cp.start()

    routed = jnp.concatenate([rbuf[pl.ds(c, tm, stride=nchunk), :] for c in range(nchunk)], axis=1)

    x1 = x1_ref[...]
    xb = x1.astype(BF16)
    gate = _dot(xb, wsg_ref[...])
    up = _dot(xb, wsu_ref[...])
    shared = _dot(((gate * _sigmoid(gate)) * up).astype(BF16), wsd_ref[...])
    o_ref[...] = _layer_norm(alpha * x1 + (routed + shared), lg_ref[...], lb_ref[...])


def _final(ys, dst_t, w_t, x1, wsg, wsu, wsd, lg, lb, alpha, tm):
    t, d = x1.shape
    nchunk = d // LANES
    ds = wsg.shape[1]
    nsteps = t // tm
    full = lambda i: (0, 0)
    return pl.pallas_call(
        functools.partial(_final_kernel, tm=tm, nchunk=nchunk, alpha=alpha, nsteps=nsteps),
        grid=(nsteps,),
        in_specs=[pl.BlockSpec(memory_space=pl.ANY)] * 3 + [
            pl.BlockSpec((tm, d), lambda i: (i, 0)),
            pl.BlockSpec((d, ds), full), pl.BlockSpec((d, ds), full), pl.BlockSpec((ds, d), full),
            pl.BlockSpec((1, d), full), pl.BlockSpec((1, d), full)],
        out_specs=pl.BlockSpec((tm, d), lambda i: (i, 0)),
        out_shape=jax.ShapeDtypeStruct((t, d), F32),
        scratch_shapes=[pltpu.VMEM((2, tm * TOP_K * nchunk, LANES), F32),
                        pltpu.VMEM((tm * nchunk, LANES), F32),
                        pltpu.SMEM((2, TOP_K, tm), jnp.int32),
                        pltpu.SMEM((2, TOP_K, tm), F32),
                        pltpu.SemaphoreType.DMA((2,)),
                        pltpu.SemaphoreType.DMA((2,))],
        compiler_params=pltpu.CompilerParams(
            dimension_semantics=("arbitrary",), vmem_limit_bytes=VMEM_LIMIT),
        name="combine_shared_ln2",
    )(dst_t, w_t, ys, x1, wsg, wsu, wsd, lg, lb)


def _block_tables(counts, n_rows):
    mb = MOE_BLOCK
    n_exp = counts.shape[0]
    padded = (counts + mb - 1) // mb * mb
    pad_end = jnp.cumsum(padded)
    pad_start = pad_end - padded
    valid_end = pad_start + counts
    n_blocks = n_rows // mb
    first_row = jnp.arange(n_blocks, dtype=jnp.int32) * mb
    block_expert = jnp.minimum(
        jnp.sum((pad_end[None, :] <= first_row[:, None]).astype(jnp.int32), axis=1), n_exp - 1)
    block_valid = jnp.clip(valid_end[block_expert] - first_row, 0, mb)
    nused = (pad_end[-1:] // mb).astype(jnp.int32)
    i32 = lambda a: a.astype(jnp.int32)
    return i32(pad_start), i32(pad_end), i32(valid_end), block_expert, i32(block_valid), nused


def _tile(n, pref):
    while n % pref:
        pref //= 2
    return pref


def kernel(x, w_in, lower_bounds, hg_norm_g, w_branch_hg, w_branch_sb, w_out, ln1_g, ln1_b,
           w_router, router_bias, w_exp_gate, w_exp_up, w_exp_down,
           w_sh_gate, w_sh_up, w_sh_down, ln2_g, ln2_b):
    depth = w_in.shape[0]
    assert depth == 1, "single-layer block only"
    batch, seq, d = x.shape
    t = batch * seq
    n_exp = w_router.shape[-1]
    alpha = (2.0 * depth) ** 0.25
    nchunk = d // LANES

    lb = jnp.cumsum(jax.nn.softmax(lower_bounds.astype(F32), axis=0), axis=0)[0]
    xf = x.reshape(t, d)

    w = w_in[0].astype(BF16)
    c = d
    o_hg = _hgrn(xf, w[:, 0:4 * c], lb, hg_norm_g[0].astype(F32), batch, seq, _tile(seq, 256))
    pbf = _matmul(xf, w[:, 4 * c:7 * c], BF16, _tile(t, 1024), 512)
    o_sb = _sb_attention(pbf, batch, seq, _tile(seq, 256), 2)

    x1, x1s = _merge(o_hg, o_sb, xf, w[:, 7 * c:9 * c],
                     w_branch_hg[0].astype(BF16), w_branch_sb[0].astype(BF16), w_out[0].astype(BF16),
                     ln1_g[0].reshape(1, d).astype(F32), ln1_b[0].reshape(1, d).astype(F32),
                     alpha, _tile(t, 512))

    idx_t, rank_t, w_t, cnt = _route(x1, w_router[0], router_bias[0], _tile(t, 256))
    p_rows = t * TOP_K + n_exp * MOE_BLOCK
    pstart, pend, vend, block_expert, block_valid, nused = _block_tables(cnt[:, 0].astype(jnp.int32), p_rows)
    dstx_t, dsty_t = _dest_rows(idx_t, rank_t, pstart, nchunk, _tile(t, 256))
    xs = _dispatch(x1s, dstx_t, pstart, pend, vend, p_rows, _tile(t, 128))
    ys = _experts(xs, block_expert, nused, block_valid, w_exp_gate[0], w_exp_up[0], w_exp_down[0])
    out = _final(ys, dsty_t, w_t, x1,
                 w_sh_gate[0].astype(BF16), w_sh_up[0].astype(BF16), w_sh_down[0].astype(BF16),
                 ln2_g[0].reshape(1, d).astype(F32), ln2_b[0].reshape(1, d).astype(F32),
                 alpha, _tile(t, 128))
    return out.reshape(batch, seq, d)
```

```python
import functools

import jax
import jax.numpy as jnp
from jax import lax
from jax.experimental import pallas as pl
from jax.experimental.pallas import tpu as pltpu

F32 = jnp.float32
BF16 = jnp.bfloat16

LANES = 128
SUBLANES = 8
VMEM_LIMIT = 48 * 1024 * 1024

HG_HEADS = 8
HG_CHUNK = 32
SB_HEADS = 8
N_GROUPS = 8
TOPK_GROUPS = 4
TOP_K = 8
ROUTED_SCALE = 2.5
MOE_BLOCK = 512
MOE_SUB = 512
LN_EPS = 1e-5
RMS_EPS = 1e-6
STICK_DEAD = 110.0

NT_DIMS = (((1,), (1,)), ((), ()))
TN_DIMS = (((0,), (0,)), ((), ()))


def _dot(a, b):
    return jnp.dot(a, b, preferred_element_type=F32)


def _sigmoid(x):
    return 1.0 / (1.0 + jnp.exp(-x))


def _split2(x):
    hi = x.astype(BF16)
    lo = (x - hi.astype(F32)).astype(BF16)
    return hi, lo


def _layer_norm(r, g, b):
    mu = jnp.mean(r, axis=-1, keepdims=True)
    d = r - mu
    var = jnp.mean(d * d, axis=-1, keepdims=True)
    return d * lax.rsqrt(var + LN_EPS) * g + b


def _mm_kernel(x_ref, w_ref, o_ref):
    o_ref[...] = _dot(x_ref[...].astype(BF16), w_ref[...]).astype(o_ref.dtype)


def _matmul(x, w, out_dtype, tm, tn):
    m, k = x.shape
    n = w.shape[1]
    return pl.pallas_call(
        _mm_kernel,
        grid=(m // tm, n // tn),
        in_specs=[pl.BlockSpec((tm, k), lambda i, j: (i, 0)),
                  pl.BlockSpec((k, tn), lambda i, j: (0, j))],
        out_specs=pl.BlockSpec((tm, tn), lambda i, j: (i, j)),
        out_shape=jax.ShapeDtypeStruct((m, n), out_dtype),
        compiler_params=pltpu.CompilerParams(
            dimension_semantics=("parallel", "arbitrary"), vmem_limit_bytes=VMEM_LIMIT),
        name="in_proj",
    )(x, w)


def _hgrn_kernel(x_ref, w_ref, lb_ref, g_ref, o_ref, st_ref, *, ts, chunk, heads):
    @pl.when(pl.program_id(1) == 0)
    def _():
        st_ref[...] = jnp.zeros_like(st_ref)

    shift = chunk.bit_length() - 1
    row = lax.broadcasted_iota(jnp.int32, (ts, ts), 0)
    col = lax.broadcasted_iota(jnp.int32, (ts, ts), 1)
    same = (row >> shift) == (col >> shift)
    causal = jnp.logical_and(same, col <= row)
    tri = jnp.where(causal, 1.0, 0.0).astype(BF16)
    ones = jnp.where(same, 1.0, 0.0).astype(BF16)
    d = LANES
    hd = heads * d

    xb = x_ref[...].astype(BF16)
    hq_all = _dot(xb, w_ref[:, 0:hd])
    hf_all = _dot(xb, w_ref[:, hd:2 * hd])
    hi_all = _dot(xb, w_ref[:, 2 * hd:3 * hd]).astype(BF16)
    hg_all = _dot(xb, w_ref[:, 3 * hd:4 * hd])

    for hh in range(heads):
        cols = slice(hh * d, (hh + 1) * d)
        lb = lb_ref[:, cols]
        hf = hf_all[:, cols]
        log_f = jnp.log(lb + (1.0 - lb) * _sigmoid(hf))
        k_in = (1.0 - lb) * _sigmoid(-hf)

        parts = jnp.concatenate(_split2(log_f), axis=1)
        cs = _dot(tri, parts)
        tot = _dot(ones, parts)
        b = cs[:, :d] + cs[:, d:]
        b_end = tot[:, :d] + tot[:, d:]

        q = hq_all[:, cols]
        q_start = (q * jnp.exp(b)).astype(BF16)
        q_end = (q * jnp.exp(b - b_end)).astype(BF16)
        k_end = (k_in * jnp.exp(b_end - b)).astype(BF16)
        v = hi_all[:, cols]

        scores = lax.dot_general(q_end, k_end, NT_DIMS, preferred_element_type=F32)
        scores = jnp.where(causal, scores, 0.0)
        o_intra = _dot(scores.astype(BF16), v)

        dec = jnp.exp(b_end)
        st = st_ref[hh]
        outs = []
        for c in range(ts // chunk):
            lo = c * chunk
            outs.append(lax.dot_general(q_start[lo:lo + chunk], st.astype(BF16), NT_DIMS,
                                        preferred_element_type=F32))
            kv = lax.dot_general(v[lo:lo + chunk], k_end[lo:lo + chunk], TN_DIMS,
                                 preferred_element_type=F32)
            st = dec[lo:lo + 1, :] * st + kv
        st_ref[hh] = st
        o = o_intra + jnp.concatenate(outs, axis=0)

        o = o * lax.rsqrt(jnp.mean(o * o, axis=-1, keepdims=True) + RMS_EPS)
        o = o * g_ref[:, cols]
        hg = hg_all[:, cols]
        o_ref[:, cols] = (o * (hg * _sigmoid(hg))).astype(o_ref.dtype)


def _hgrn(x, w_hg, lb, g, batch, seq, ts):
    t, dm = x.shape
    h = HG_HEADS
    hd = h * LANES
    ns = seq // ts
    full = lambda b, s: (0, 0)
    return pl.pallas_call(
        functools.partial(_hgrn_kernel, ts=ts, chunk=HG_CHUNK, heads=h),
        grid=(batch, ns),
        in_specs=[pl.BlockSpec((ts, dm), lambda b, s: (b * ns + s, 0)),
                  pl.BlockSpec((dm, 4 * hd), full),
                  pl.BlockSpec((1, hd), full),
                  pl.BlockSpec((1, hd), full)],
        out_specs=pl.BlockSpec((ts, hd), lambda b, s: (b * ns + s, 0)),
        out_shape=jax.ShapeDtypeStruct((t, hd), BF16),
        scratch_shapes=[pltpu.VMEM((h, LANES, LANES), F32)],
        compiler_params=pltpu.CompilerParams(
            dimension_semantics=("parallel", "arbitrary"), vmem_limit_bytes=VMEM_LIMIT),
        name="hgrn2",
    )(x, w_hg, lb.reshape(1, hd), g.reshape(1, hd))


def _sb_kernel(q_ref, k_ref, v_ref, u_ref, o_ref, *, tq, scale, hp):
    i = pl.program_id(2)
    u = u_ref[...]
    rep = tq // LANES
    d = LANES

    def block(hh, j, carry, acc, masked, live=None):
        cols = slice(hh * d, (hh + 1) * d)
        start = pl.multiple_of(j * tq, tq)
        q = q_ref[:, cols]
        kj = k_ref[pl.ds(start, tq), cols]
        vj = v_ref[pl.ds(start, tq), cols]
        z = lax.dot_general(q, kj, NT_DIMS, preferred_element_type=F32) * scale
        sp = jnp.maximum(z, 0.0) + jnp.log(1.0 + jnp.exp(-jnp.abs(z)))
        if masked:
            row = lax.broadcasted_iota(jnp.int32, (tq, tq), 0)
            col = lax.broadcasted_iota(jnp.int32, (tq, tq), 1)
            before = col < row
            drop = jnp.where(before, sp, 0.0)
        elif live is not None:
            drop = sp * live
        else:
            drop = sp
        hi = drop.astype(BF16)
        lo = (drop - hi.astype(F32)).astype(BF16)
        later = _dot(hi, u) + _dot(lo, u)
        stick = later + jnp.concatenate([carry] * rep, axis=1)
        w = jnp.exp((z - sp) - stick)
        if masked:
            w = jnp.where(before, w, 0.0)
        elif live is not None:
            w = w * live
        acc = acc + _dot(w.astype(BF16), vj)
        carry = carry + jnp.broadcast_to(later[:, 0:1] + drop[:, 0:1], (tq, LANES))
        return carry, acc

    def some_row_alive(carries):
        m = jnp.min(carries[0])
        for c in carries[1:]:
            m = jnp.minimum(m, jnp.min(c))
        return (m < STICK_DEAD).astype(jnp.int32)

    zero = jnp.zeros((tq, LANES), F32)
    live = (i > 0).astype(F32)
    prev = jnp.maximum(i - 1, 0)
    carries, accs = [], []
    for hh in range(hp):
        carry, acc = block(hh, i, zero, zero, True)
        carry, acc = block(hh, prev, carry, acc, False, live=live)
        carries.append(carry)
        accs.append(acc)

    def cond(state):
        return jnp.logical_and(state[0] >= 0, state[1] > 0)

    def body(state):
        j, _, carries, accs = state
        out = [block(hh, j, carries[hh], accs[hh], False) for hh in range(hp)]
        carries = tuple(o[0] for o in out)
        accs = tuple(o[1] for o in out)
        return j - 1, some_row_alive(carries), carries, accs

    _, _, carries, accs = lax.while_loop(
        cond, body, (i - 2, some_row_alive(carries), tuple(carries), tuple(accs)))
    for hh in range(hp):
        o_ref[:, hh * d:(hh + 1) * d] = accs[hh].astype(o_ref.dtype)


def _sb_attention(pbf, batch, seq, tq, hp):
    t = batch * seq
    h = SB_HEADS
    ng = h // hp
    w = hp * LANES
    nq = seq // tq
    r = jnp.arange(tq)
    u = (r[:, None] > r[None, :]).astype(BF16)
    return pl.pallas_call(
        functools.partial(_sb_kernel, tq=tq, scale=float(LANES) ** -0.5, hp=hp),
        grid=(batch, ng, nq),
        in_specs=[pl.BlockSpec((tq, w), lambda b, hg, i: (b * nq + i, hg)),
                  pl.BlockSpec((seq, w), lambda b, hg, i: (b, ng + hg)),
                  pl.BlockSpec((seq, w), lambda b, hg, i: (b, 2 * ng + hg)),
                  pl.BlockSpec((tq, tq), lambda b, hg, i: (0, 0))],
        out_specs=pl.BlockSpec((tq, w), lambda b, hg, i: (b * nq + i, hg)),
        out_shape=jax.ShapeDtypeStruct((t, h * LANES), BF16),
        compiler_params=pltpu.CompilerParams(
            dimension_semantics=("parallel", "parallel", "arbitrary"), vmem_limit_bytes=VMEM_LIMIT),
        name="stick_breaking",
    )(pbf, pbf, pbf, u)


def _merge_kernel(ohg_ref, osb_ref, x_ref, wgate_ref, wbh_ref, wbs_ref, wo_ref,
                  lg_ref, lb_ref, x1_ref, x1s_ref, *, alpha, tm, nchunk):
    x = x_ref[...]
    xb = x.astype(BF16)
    d = x.shape[1]
    g_hg = _dot(xb, wgate_ref[:, 0:d])
    g_sb = _dot(xb, wgate_ref[:, d:2 * d])
    y_hg = _dot(ohg_ref[...], wbh_ref[...])
    y_sb = _dot(osb_ref[...], wbs_ref[...])
    merged = _sigmoid(g_hg) * y_hg + _sigmoid(g_sb) * y_sb
    hmix = _dot(merged.astype(BF16), wo_ref[...])
    x1 = _layer_norm(alpha * x + hmix, lg_ref[...], lb_ref[...])
    x1_ref[...] = x1
    half = nchunk // 2
    for c in range(half):
        lo = x1[:, c * LANES:(c + 1) * LANES].astype(BF16).astype(F32)
        hi = x1[:, (c + half) * LANES:(c + half + 1) * LANES].astype(BF16).astype(F32)
        word = pltpu.bitcast(hi, jnp.uint32) | (pltpu.bitcast(lo, jnp.uint32) >> 16)
        x1s_ref[pl.ds(c, tm, stride=half), :] = word


def _merge(o_hg, o_sb, x, wgate, wbh, wbs, wo, lg, lb, alpha, tm):
    t, d = x.shape
    nchunk = d // LANES
    row = lambda off: (lambda i: (i, off))
    full = lambda i: (0, 0)
    return pl.pallas_call(
        functools.partial(_merge_kernel, alpha=alpha, tm=tm, nchunk=nchunk),
        grid=(t // tm,),
        in_specs=[pl.BlockSpec((tm, d), row(0)), pl.BlockSpec((tm, d), row(0)),
                  pl.BlockSpec((tm, d), row(0)),
                  pl.BlockSpec((d, 2 * d), full),
                  pl.BlockSpec((d, d), full), pl.BlockSpec((d, d), full), pl.BlockSpec((d, d), full),
                  pl.BlockSpec((1, d), full), pl.BlockSpec((1, d), full)],
        out_specs=[pl.BlockSpec((tm, d), row(0)), pl.BlockSpec((tm * nchunk // 2, LANES), row(0))],
        out_shape=[jax.ShapeDtypeStruct((t, d), F32),
                   jax.ShapeDtypeStruct((t * nchunk // 2, LANES), jnp.uint32)],
        compiler_params=pltpu.CompilerParams(
            dimension_semantics=("parallel",), vmem_limit_bytes=VMEM_LIMIT),
        name="merge_ln1",
    )(o_hg, o_sb, x, wgate, wbh, wbs, wo, lg, lb)


def _route_kernel(x_ref, wh_ref, wl_ref, bias_ref, u_ref, ones_ref,
                  idx_ref, rank_ref, w_ref, cnt_ref, carry, *, tile, n_exp):
    @pl.when(pl.program_id(0) == 0)
    def _():
        carry[...] = jnp.zeros_like(carry)

    x = x_ref[...]
    xh = x.astype(BF16)
    xl = (x - xh.astype(F32)).astype(BF16)
    wh = wh_ref[...]
    nt = functools.partial(lax.dot_general, dimension_numbers=NT_DIMS, preferred_element_type=F32)
    logits = nt(wh, xh) + nt(wh, xl) + nt(wl_ref[...], xh)
    scores = _sigmoid(logits)
    biased = scores + bias_ref[...]

    neg_inf = -jnp.inf
    gsz = n_exp // N_GROUPS
    groups = [biased[g * gsz:(g + 1) * gsz] for g in range(N_GROUPS)]
    gscore = []
    for xg in groups:
        m1 = jnp.max(xg, axis=0, keepdims=True)
        n1 = jnp.sum(jnp.where(xg == m1, 1.0, 0.0), axis=0, keepdims=True)
        m2 = jnp.max(jnp.where(xg < m1, xg, neg_inf), axis=0, keepdims=True)
        gscore.append(m1 + jnp.where(n1 >= 2.0, m1, m2))
    masked = []
    for g in range(N_GROUPS):
        beaten_by = jnp.zeros_like(gscore[g])
        for h in range(N_GROUPS):
            if h < g:
                beaten_by = beaten_by + jnp.where(gscore[h] >= gscore[g], 1.0, 0.0)
            elif h > g:
                beaten_by = beaten_by + jnp.where(gscore[h] > gscore[g], 1.0, 0.0)
        keep = jnp.broadcast_to(beaten_by, (gsz, tile)) < float(TOPK_GROUPS)
        masked.append(jnp.where(keep, groups[g], jnp.finfo(F32).min))
    cur = jnp.concatenate(masked, axis=0)

    eid = lax.broadcasted_iota(jnp.int32, (n_exp, tile), 0).astype(F32)
    chosen = jnp.zeros((n_exp, tile), F32)
    idxs, ws = [], []
    for _ in range(TOP_K):
        m = jnp.max(cur, axis=0, keepdims=True)
        ik = jnp.min(jnp.where(cur == m, eid, float(n_exp)), axis=0, keepdims=True)
        sel = eid == ik
        idxs.append(ik)
        ws.append(jnp.sum(jnp.where(sel, scores, 0.0), axis=0, keepdims=True))
        chosen = chosen + jnp.where(sel, 1.0, 0.0)
        cur = jnp.where(sel, neg_inf, cur)

    chosen_b = chosen.astype(BF16)
    run = carry[...]
    rank = _dot(chosen_b, u_ref[...]) + jnp.concatenate([run] * (tile // LANES), axis=1)
    ranks = [jnp.sum(jnp.where(eid == ik, rank, 0.0), axis=0, keepdims=True) for ik in idxs]
    run = run + _dot(chosen_b, ones_ref[...])
    carry[...] = run
    cnt_ref[...] = run

    wsum = ws[0]
    for wk in ws[1:]:
        wsum = wsum + wk
    idx_ref[...] = jnp.concatenate(idxs, axis=0).astype(jnp.int32)
    rank_ref[...] = jnp.concatenate(ranks, axis=0).astype(jnp.int32)
    w_ref[...] = jnp.concatenate([wk / wsum * ROUTED_SCALE for wk in ws], axis=0)


def _route(x1, w_router, router_bias, tile):
    t, d = x1.shape
    n_exp = w_router.shape[1]
    wt = w_router.astype(F32).T
    wh = wt.astype(BF16)
    wl = (wt - wh.astype(F32)).astype(BF16)
    r = jnp.arange(tile)
    u = (r[:, None] < r[None, :]).astype(BF16)
    ones = jnp.ones((tile, LANES), BF16)
    full = lambda i: (0, 0)
    tok = lambda i: (0, i)
    return pl.pallas_call(
        functools.partial(_route_kernel, tile=tile, n_exp=n_exp),
        grid=(t // tile,),
        in_specs=[pl.BlockSpec((tile, d), lambda i: (i, 0)),
                  pl.BlockSpec((n_exp, d), full), pl.BlockSpec((n_exp, d), full),
                  pl.BlockSpec((n_exp, 1), full),
                  pl.BlockSpec((tile, tile), full), pl.BlockSpec((tile, LANES), full)],
        out_specs=[pl.BlockSpec((TOP_K, tile), tok), pl.BlockSpec((TOP_K, tile), tok),
                   pl.BlockSpec((TOP_K, tile), tok), pl.BlockSpec((n_exp, LANES), full)],
        out_shape=[jax.ShapeDtypeStruct((TOP_K, t), jnp.int32), jax.ShapeDtypeStruct((TOP_K, t), jnp.int32),
                   jax.ShapeDtypeStruct((TOP_K, t), F32), jax.ShapeDtypeStruct((n_exp, LANES), F32)],
        scratch_shapes=[pltpu.VMEM((n_exp, LANES), F32)],
        compiler_params=pltpu.CompilerParams(
            dimension_semantics=("arbitrary",), vmem_limit_bytes=VMEM_LIMIT),
        name="router_topk",
    )(x1, wh, wl, router_bias.astype(F32).reshape(n_exp, 1), u, ones)


def _dest_kernel(idx_ref, rank_ref, pstart_ref, dstx_ref, dsty_ref, *, tile, n_exp, nchunk):
    eid = lax.broadcasted_iota(jnp.int32, (n_exp, tile), 0)
    pstart = jnp.broadcast_to(pstart_ref[...], (n_exp, tile))
    rows = []
    for k in range(TOP_K):
        hit = eid == idx_ref[k:k + 1, :]
        base = jnp.sum(jnp.where(hit, pstart, 0.0), axis=0, keepdims=True)
        rows.append(base.astype(jnp.int32) + rank_ref[k:k + 1, :])
    dst = jnp.concatenate(rows, axis=0)
    dstx_ref[...] = dst * (nchunk // 2)
    dsty_ref[...] = dst * nchunk


def _dest_rows(idx_t, rank_t, pstart, nchunk, tile):
    t = idx_t.shape[1]
    n_exp = pstart.shape[0]
    tok = lambda i: (0, i)
    return pl.pallas_call(
        functools.partial(_dest_kernel, tile=tile, n_exp=n_exp, nchunk=nchunk),
        grid=(t // tile,),
        in_specs=[pl.BlockSpec((TOP_K, tile), tok), pl.BlockSpec((TOP_K, tile), tok),
                  pl.BlockSpec((n_exp, 1), lambda i: (0, 0))],
        out_specs=[pl.BlockSpec((TOP_K, tile), tok)] * 2,
        out_shape=[jax.ShapeDtypeStruct((TOP_K, t), jnp.int32)] * 2,
        compiler_params=pltpu.CompilerParams(
            dimension_semantics=("parallel",), vmem_limit_bytes=VMEM_LIMIT),
        name="dest_rows",
    )(idx_t, rank_t, pstart.astype(F32).reshape(n_exp, 1))


def _dispatch_kernel(pstart_ref, pend_ref, vend_ref, dst_hbm, x_hbm, xs_hbm,
                     dst_s, xbuf, zbuf, isem, lsem, dsem, zsem,
                     *, td, nchunk, mb, sub, n_exp, nsteps, n_blocks):
    i = pl.program_id(0)
    rows = sub * nchunk
    trows = td * nchunk

    def idx_copy(step, slot):
        return pltpu.make_async_copy(dst_hbm.at[:, pl.ds(step * td, td)], dst_s.at[slot], isem.at[slot])

    def tile_load(step, slot):
        start = pl.multiple_of(step * trows, trows)
        return pltpu.make_async_copy(x_hbm.at[pl.ds(start, trows), :], xbuf.at[slot], lsem.at[slot])

    def wait_rows(slot):
        for _ in range(TOP_K):
            pltpu.make_async_copy(xbuf.at[slot], xbuf.at[slot], dsem.at[slot]).wait()

    @pl.when(i == 0)
    def _():
        idx_copy(0, 0).start()
        tile_load(0, 0).start()
        zbuf[...] = jnp.zeros_like(zbuf)
        pieces = mb // sub

        def zero_copy(e, p):
            start = pl.multiple_of((pend_ref[e] - (p + 1) * sub) * nchunk, nchunk)
            return pltpu.make_async_copy(zbuf, xs_hbm.at[pl.ds(start, rows), :], zsem)

        def needs_zero(e, p):
            return pend_ref[e] - vend_ref[e] > p * sub

        def zstart(e, _):
            for p in range(pieces):
                @pl.when(needs_zero(e, p))
                def _():
                    zero_copy(e, p).start()
            return 0

        def zwait(e, _):
            for p in range(pieces):
                @pl.when(needs_zero(e, p))
                def _():
                    zero_copy(e, p).wait()
            return 0

        lax.fori_loop(0, n_exp, zstart, 0)
        lax.fori_loop(0, n_exp, zwait, 0)

        def tail_copy(b):
            start = pl.multiple_of(b * rows, rows)
            return pltpu.make_async_copy(zbuf, xs_hbm.at[pl.ds(start, rows), :], zsem)

        def tstart(b, _):
            tail_copy(b).start()
            return 0

        def twait(b, _):
            tail_copy(b).wait()
            return 0

        first_unused = pend_ref[n_exp - 1] // sub
        lax.fori_loop(first_unused, n_blocks * pieces, tstart, 0)
        lax.fori_loop(first_unused, n_blocks * pieces, twait, 0)

    for cur in range(3):
        nxt = (cur + 1) % 3

        @pl.when(i % 3 == cur)
        def _():
            idx_copy(i, cur).wait()

            @pl.when(i >= 2)
            def _():
                wait_rows(nxt)

            @pl.when(i + 1 < nsteps)
            def _():
                idx_copy(i + 1, nxt).start()
                tile_load(i + 1, nxt).start()

            tile_load(i, cur).wait()

            def body(r, _):
                src = pl.multiple_of(r * nchunk, nchunk)
                for k in range(TOP_K):
                    dst = pl.multiple_of(dst_s[cur, k, r], nchunk)
                    pltpu.make_async_copy(xbuf.at[cur, pl.ds(src, nchunk), :],
                                          xs_hbm.at[pl.ds(dst, nchunk), :], dsem.at[cur]).start(priority=k % 2)
                return 0

            lax.fori_loop(0, td, body, 0, unroll=2)

            @pl.when(i == nsteps - 1)
            def _():
                @pl.when(i >= 1)
                def _():
                    wait_rows((cur + 2) % 3)

                wait_rows(cur)


def _dispatch(x2, dst_t, pstart, pend, vend, p_rows, td):
    n_exp = pstart.shape[0]
    t = dst_t.shape[1]
    nchunk = x2.shape[0] // t
    nsteps = t // td
    grid_spec = pltpu.PrefetchScalarGridSpec(
        num_scalar_prefetch=3,
        grid=(nsteps,),
        in_specs=[pl.BlockSpec(memory_space=pl.ANY)] * 2,
        out_specs=pl.BlockSpec(memory_space=pl.ANY),
        scratch_shapes=[pltpu.SMEM((3, TOP_K, td), jnp.int32),
                        pltpu.VMEM((3, td * nchunk, LANES), x2.dtype),
                        pltpu.VMEM((MOE_SUB * nchunk, LANES), x2.dtype),
                        pltpu.SemaphoreType.DMA((3,)),
                        pltpu.SemaphoreType.DMA((3,)),
                        pltpu.SemaphoreType.DMA((3,)),
                        pltpu.SemaphoreType.DMA],
    )
    return pl.pallas_call(
        functools.partial(_dispatch_kernel, td=td, nchunk=nchunk, mb=MOE_BLOCK, sub=MOE_SUB, n_exp=n_exp,
                          nsteps=nsteps, n_blocks=p_rows // MOE_BLOCK),
        grid_spec=grid_spec,
        out_shape=jax.ShapeDtypeStruct((p_rows * nchunk, LANES), x2.dtype),
        compiler_params=pltpu.CompilerParams(
            dimension_semantics=("arbitrary",), vmem_limit_bytes=VMEM_LIMIT),
        name="moe_dispatch",
    )(pstart, pend, vend, dst_t, x2)


def _experts_kernel(bexp_ref, nused_ref, bvalid_ref, xs_ref, wg_ref, wu_ref, wd_ref, ys_ref,
                    wg_b, wu_b, wd_b, *, mb, sub, nchunk):
    i = pl.program_id(0)
    half = nchunk // 2
    used = i < nused_ref[0]

    @pl.when(used)
    def _():
        new_expert = jnp.logical_or(i == 0, bexp_ref[i] != bexp_ref[jnp.maximum(i - 1, 0)])

        @pl.when(new_expert)
        def _():
            wg_b[...] = wg_ref[0].astype(BF16)
            wu_b[...] = wu_ref[0].astype(BF16)
            wd_b[...] = wd_ref[0].astype(BF16)

    for p in range(mb // sub):
        live = jnp.logical_and(used, bvalid_ref[i] > p * sub)

        @pl.when(live)
        def _():
            x0 = p * sub * half
            words = [xs_ref[pl.ds(x0 + c, sub, stride=half), :] for c in range(half)]
            lows = [pltpu.bitcast(wv << 16, F32) for wv in words]
            highs = [pltpu.bitcast(wv & jnp.uint32(0xFFFF0000), F32) for wv in words]
            xb = jnp.concatenate(lows + highs, axis=1).astype(BF16)
            gate = _dot(xb, wg_b[...])
            up = _dot(xb, wu_b[...])
            hid = (gate * _sigmoid(gate)) * up
            y = _dot(hid.astype(BF16), wd_b[...])
            y0 = p * sub * nchunk
            for c in range(nchunk):
                ys_ref[pl.ds(y0 + c, sub, stride=nchunk), :] = y[:, c * LANES:(c + 1) * LANES]

        @pl.when(jnp.logical_not(live))
        def _():
            ys_ref[pl.ds(p * sub * nchunk, sub * nchunk), :] = jnp.zeros((sub * nchunk, LANES), F32)


def _experts(xs, block_expert, nused, block_valid, w_gate, w_up, w_down):
    e, d, de = w_gate.shape
    nchunk = d // LANES
    mb = MOE_BLOCK
    rows = mb * nchunk
    xrows = rows // 2
    n_blocks = xs.shape[0] // xrows
    blk = lambda i, be, nu, bv: (jnp.minimum(i, nu[0] - 1), 0)
    wsel = lambda i, be, nu, bv: (be[i], 0, 0)
    grid_spec = pltpu.PrefetchScalarGridSpec(
        num_scalar_prefetch=3,
        grid=(n_blocks,),
        in_specs=[pl.BlockSpec((xrows, LANES), blk),
                  pl.BlockSpec((1, d, de), wsel), pl.BlockSpec((1, d, de), wsel),
                  pl.BlockSpec((1, de, d), wsel)],
        out_specs=pl.BlockSpec((rows, LANES), lambda i, be, nu, bv: (i, 0)),
        scratch_shapes=[pltpu.VMEM((d, de), BF16), pltpu.VMEM((d, de), BF16), pltpu.VMEM((de, d), BF16)],
    )
    return pl.pallas_call(
        functools.partial(_experts_kernel, mb=mb, sub=MOE_SUB, nchunk=nchunk),
        grid_spec=grid_spec,
        out_shape=jax.ShapeDtypeStruct((n_blocks * rows, LANES), F32),
        compiler_params=pltpu.CompilerParams(
            dimension_semantics=("arbitrary",), vmem_limit_bytes=VMEM_LIMIT),
        name="routed_experts",
    )(block_expert, nused, block_valid, xs, w_gate, w_up, w_down)


def _final_kernel(dst_hbm, w_hbm, ys_hbm, x1_ref, wsg_ref, wsu_ref, wsd_ref,
                  lg_ref, lb_ref, o_ref, ybuf, rbuf, dst_s, w_s, isem, gsem,
                  *, tm, nchunk, alpha, nsteps):
    i = pl.program_id(0)
    tok_rows = TOP_K * nchunk

    def idx_copies(step, slot):
        cols = pl.ds(step * tm, tm)
        return (pltpu.make_async_copy(dst_hbm.at[:, cols], dst_s.at[slot], isem.at[slot]),
                pltpu.make_async_copy(w_hbm.at[:, cols], w_s.at[slot], isem.at[slot]))

    def issue_gathers(slot):
        def body(r, _):
            for k in range(TOP_K):
                src = pl.multiple_of(dst_s[slot, k, r], nchunk)
                pltpu.make_async_copy(ys_hbm.at[pl.ds(src, nchunk), :],
                                      ybuf.at[slot, pl.ds(r * tok_rows + k * nchunk, nchunk), :],
                                      gsem.at[slot]).start(priority=k % 2)
            return 0

        lax.fori_loop(0, tm, body, 0, unroll=2)

    def combine(slot):
        def body(r, _):
            acc = ybuf[slot, pl.ds(r * tok_rows, nchunk), :] * w_s[slot, 0, r]
            for k in range(1, TOP_K):
                acc = acc + ybuf[slot, pl.ds(r * tok_rows + k * nchunk, nchunk), :] * w_s[slot, k, r]
            rbuf[pl.ds(r * nchunk, nchunk), :] = acc
            return 0

        lax.fori_loop(0, tm, body, 0, unroll=2)

    @pl.when(i == 0)
    def _():
        for cp in idx_copies(0, 0):
            cp.start()
        for cp in idx_copies(0, 0):
            cp.wait()
        issue_gathers(0)

        @pl.when(nsteps > 1)
        def _():
            for cp in idx_copies(1, 1):
                cp.start()

    for cur in range(2):
        nxt = 1 - cur

        @pl.when(i % 2 == cur)
        def _():
            @pl.when(i + 1 < nsteps)
            def _():
                for cp in idx_copies(i + 1, nxt):
                    cp.wait()
                issue_gathers(nxt)

            pltpu.make_async_copy(ybuf.at[cur], ybuf.at[cur], gsem.at[cur]).wait()
            combine(cur)

            @pl.when(i + 2 < nsteps)
            def _():
                for cp in idx_copies(i + 2, cur):
                    cp.start()

    routed = jnp.concatenate([rbuf[pl.ds(c, tm, stride=nchunk), :] for c in range(nchunk)], axis=1)

    x1 = x1_ref[...]
    xb = x1.astype(BF16)
    gate = _dot(xb, wsg_ref[...])
    up = _dot(xb, wsu_ref[...])
    shared = _dot(((gate * _sigmoid(gate)) * up).astype(BF16), wsd_ref[...])
    o_ref[...] = _layer_norm(alpha * x1 + (routed + shared), lg_ref[...], lb_ref[...])


def _final(ys, dst_t, w_t, x1, wsg, wsu, wsd, lg, lb, alpha, tm):
    t, d = x1.shape
    nchunk = d // LANES
    ds = wsg.shape[1]
    nsteps = t // tm
    full = lambda i: (0, 0)
    return pl.pallas_call(
        functools.partial(_final_kernel, tm=tm, nchunk=nchunk, alpha=alpha, nsteps=nsteps),
        grid=(nsteps,),
        in_specs=[pl.BlockSpec(memory_space=pl.ANY)] * 3 + [
            pl.BlockSpec((tm, d), lambda i: (i, 0)),
            pl.BlockSpec((d, ds), full), pl.BlockSpec((d, ds), full), pl.BlockSpec((ds, d), full),
            pl.BlockSpec((1, d), full), pl.BlockSpec((1, d), full)],
        out_specs=pl.BlockSpec((tm, d), lambda i: (i, 0)),
        out_shape=jax.ShapeDtypeStruct((t, d), F32),
        scratch_shapes=[pltpu.VMEM((2, tm * TOP_K * nchunk, LANES), F32),
                        pltpu.VMEM((tm * nchunk, LANES), F32),
                        pltpu.SMEM((2, TOP_K, tm), jnp.int32),
                        pltpu.SMEM((2, TOP_K, tm), F32),
                        pltpu.SemaphoreType.DMA((2,)),
                        pltpu.SemaphoreType.DMA((2,))],
        compiler_params=pltpu.CompilerParams(
            dimension_semantics=("arbitrary",), vmem_limit_bytes=VMEM_LIMIT),
        name="combine_shared_ln2",
    )(dst_t, w_t, ys, x1, wsg, wsu, wsd, lg, lb)


def _block_tables(counts, n_rows):
    mb = MOE_BLOCK
    n_exp = counts.shape[0]
    padded = (counts + mb - 1) // mb * mb
    pad_end = jnp.cumsum(padded)
    pad_start = pad_end - padded
    valid_end = pad_start + counts
    n_blocks = n_rows // mb
    first_row = jnp.arange(n_blocks, dtype=jnp.int32) * mb
    block_expert = jnp.minimum(
        jnp.sum((pad_end[None, :] <= first_row[:, None]).astype(jnp.int32), axis=1), n_exp - 1)
    block_valid = jnp.clip(valid_end[block_expert] - first_row, 0, mb)
    nused = (pad_end[-1:] // mb).astype(jnp.int32)
    i32 = lambda a: a.astype(jnp.int32)
    return i32(pad_start), i32(pad_end), i32(valid_end), block_expert, i32(block_valid), nused


def _tile(n, pref):
    while n % pref:
        pref //= 2
    return pref


def kernel(x, w_in, lower_bounds, hg_norm_g, w_branch_hg, w_branch_sb, w_out, ln1_g, ln1_b,
           w_router, router_bias, w_exp_gate, w_exp_up, w_exp_down,
           w_sh_gate, w_sh_up, w_sh_down, ln2_g, ln2_b):
    depth = w_in.shape[0]
    assert depth == 1, "single-layer block only"
    batch, seq, d = x.shape
    t = batch * seq
    n_exp = w_router.shape[-1]
    alpha = (2.0 * depth) ** 0.25
    nchunk = d // LANES

    lb = jnp.cumsum(jax.nn.softmax(lower_bounds.astype(F32), axis=0), axis=0)[0]
    xf = x.reshape(t, d)

    w = w_in[0].astype(BF16)
    c = d
    o_hg = _hgrn(xf, w[:, 0:4 * c], lb, hg_norm_g[0].astype(F32), batch, seq, _tile(seq, 256))
    pbf = _matmul(xf, w[:, 4 * c:7 * c], BF16, _tile(t, 1024), 1024)
    o_sb = _sb_attention(pbf, batch, seq, _tile(seq, 256), 4)

    x1, x1s = _merge(o_hg, o_sb, xf, w[:, 7 * c:9 * c],
                     w_branch_hg[0].astype(BF16), w_branch_sb[0].astype(BF16), w_out[0].astype(BF16),
                     ln1_g[0].reshape(1, d).astype(F32), ln1_b[0].reshape(1, d).astype(F32),
                     alpha, _tile(t, 512))

    idx_t, rank_t, w_t, cnt = _route(x1, w_router[0], router_bias[0], _tile(t, 256))
    p_rows = t * TOP_K + n_exp * MOE_BLOCK
    pstart, pend, vend, block_expert, block_valid, nused = _block_tables(cnt[:, 0].astype(jnp.int32), p_rows)
    dstx_t, dsty_t = _dest_rows(idx_t, rank_t, pstart, nchunk, _tile(t, 256))
    xs = _dispatch(x1s, dstx_t, pstart, pend, vend, p_rows, _tile(t, 128))
    ys = _experts(xs, block_expert, nused, block_valid, w_exp_gate[0], w_exp_up[0], w_exp_down[0])
    out = _final(ys, dsty_t, w_t, x1,
                 w_sh_gate[0].astype(BF16), w_sh_up[0].astype(BF16), w_sh_down[0].astype(BF16),
                 ln2_g[0].reshape(1, d).astype(F32), ln2_b[0].reshape(1, d).astype(F32),
                 alpha, _tile(t, 128))
    return out.reshape(batch, seq, d)
```

```python
import functools

import jax
import jax.numpy as jnp
from jax import lax
from jax.experimental import pallas as pl
from jax.experimental.pallas import tpu as pltpu

F32 = jnp.float32
BF16 = jnp.bfloat16

LANES = 128
SUBLANES = 8
VMEM_LIMIT = 48 * 1024 * 1024

HG_HEADS = 8
HG_CHUNK = 32
SB_HEADS = 8
N_GROUPS = 8
TOPK_GROUPS = 4
TOP_K = 8
ROUTED_SCALE = 2.5
MOE_BLOCK = 512
MOE_SUB = 512
LN_EPS = 1e-5
RMS_EPS = 1e-6
STICK_DEAD = 110.0

NT_DIMS = (((1,), (1,)), ((), ()))
TN_DIMS = (((0,), (0,)), ((), ()))


def _dot(a, b):
    return jnp.dot(a, b, preferred_element_type=F32)


def _sigmoid(x):
    return 1.0 / (1.0 + jnp.exp(-x))


def _split2(x):
    hi = x.astype(BF16)
    lo = (x - hi.astype(F32)).astype(BF16)
    return hi, lo


def _layer_norm(r, g, b):
    mu = jnp.mean(r, axis=-1, keepdims=True)
    d = r - mu
    var = jnp.mean(d * d, axis=-1, keepdims=True)
    return d * lax.rsqrt(var + LN_EPS) * g + b


def _mm_kernel(x_ref, w_ref, o_ref):
    o_ref[...] = _dot(x_ref[...].astype(BF16), w_ref[...]).astype(o_ref.dtype)


def _matmul(x, w, out_dtype, tm, tn):
    m, k = x.shape
    n = w.shape[1]
    return pl.pallas_call(
        _mm_kernel,
        grid=(m // tm, n // tn),
        in_specs=[pl.BlockSpec((tm, k), lambda i, j: (i, 0)),
                  pl.BlockSpec((k, tn), lambda i, j: (0, j))],
        out_specs=pl.BlockSpec((tm, tn), lambda i, j: (i, j)),
        out_shape=jax.ShapeDtypeStruct((m, n), out_dtype),
        compiler_params=pltpu.CompilerParams(
            dimension_semantics=("parallel", "arbitrary"), vmem_limit_bytes=VMEM_LIMIT),
        name="in_proj",
    )(x, w)


def _hgrn_kernel(x_ref, w_ref, lb_ref, g_ref, o_ref, st_ref, *, ts, chunk, heads):
    @pl.when(pl.program_id(1) == 0)
    def _():
        st_ref[...] = jnp.zeros_like(st_ref)

    shift = chunk.bit_length() - 1
    row = lax.broadcasted_iota(jnp.int32, (ts, ts), 0)
    col = lax.broadcasted_iota(jnp.int32, (ts, ts), 1)
    same = (row >> shift) == (col >> shift)
    causal = jnp.logical_and(same, col <= row)
    tri = jnp.where(causal, 1.0, 0.0).astype(BF16)
    ones = jnp.where(same, 1.0, 0.0).astype(BF16)
    d = LANES
    hd = heads * d

    xb = x_ref[...].astype(BF16)
    hq_all = _dot(xb, w_ref[:, 0:hd])
    hf_all = _dot(xb, w_ref[:, hd:2 * hd])
    hi_all = _dot(xb, w_ref[:, 2 * hd:3 * hd]).astype(BF16)
    hg_all = _dot(xb, w_ref[:, 3 * hd:4 * hd])

    for hh in range(heads):
        cols = slice(hh * d, (hh + 1) * d)
        lb = lb_ref[:, cols]
        hf = hf_all[:, cols]
        log_f = jnp.log(lb + (1.0 - lb) * _sigmoid(hf))
        k_in = (1.0 - lb) * _sigmoid(-hf)

        parts = jnp.concatenate(_split2(log_f), axis=1)
        cs = _dot(tri, parts)
        tot = _dot(ones, parts)
        b = cs[:, :d] + cs[:, d:]
        b_end = tot[:, :d] + tot[:, d:]

        q = hq_all[:, cols]
        q_start = (q * jnp.exp(b)).astype(BF16)
        q_end = (q * jnp.exp(b - b_end)).astype(BF16)
        k_end = (k_in * jnp.exp(b_end - b)).astype(BF16)
        v = hi_all[:, cols]

        scores = lax.dot_general(q_end, k_end, NT_DIMS, preferred_element_type=F32)
        scores = jnp.where(causal, scores, 0.0)
        o_intra = _dot(scores.astype(BF16), v)

        dec = jnp.exp(b_end)
        st = st_ref[hh]
        outs = []
        for c in range(ts // chunk):
            lo = c * chunk
            outs.append(lax.dot_general(q_start[lo:lo + chunk], st.astype(BF16), NT_DIMS,
                                        preferred_element_type=F32))
            kv = lax.dot_general(v[lo:lo + chunk], k_end[lo:lo + chunk], TN_DIMS,
                                 preferred_element_type=F32)
            st = dec[lo:lo + 1, :] * st + kv
        st_ref[hh] = st
        o = o_intra + jnp.concatenate(outs, axis=0)

        o = o * lax.rsqrt(jnp.mean(o * o, axis=-1, keepdims=True) + RMS_EPS)
        o = o * g_ref[:, cols]
        hg = hg_all[:, cols]
        o_ref[:, cols] = (o * (hg * _sigmoid(hg))).astype(o_ref.dtype)


def _hgrn(x, w_hg, lb, g, batch, seq, ts):
    t, dm = x.shape
    h = HG_HEADS
    hd = h * LANES
    ns = seq // ts
    full = lambda b, s: (0, 0)
    return pl.pallas_call(
        functools.partial(_hgrn_kernel, ts=ts, chunk=HG_CHUNK, heads=h),
        grid=(batch, ns),
        in_specs=[pl.BlockSpec((ts, dm), lambda b, s: (b * ns + s, 0)),
                  pl.BlockSpec((dm, 4 * hd), full),
                  pl.BlockSpec((1, hd), full),
                  pl.BlockSpec((1, hd), full)],
        out_specs=pl.BlockSpec((ts, hd), lambda b, s: (b * ns + s, 0)),
        out_shape=jax.ShapeDtypeStruct((t, hd), BF16),
        scratch_shapes=[pltpu.VMEM((h, LANES, LANES), F32)],
        compiler_params=pltpu.CompilerParams(
            dimension_semantics=("parallel", "arbitrary"), vmem_limit_bytes=VMEM_LIMIT),
        name="hgrn2",
    )(x, w_hg, lb.reshape(1, hd), g.reshape(1, hd))


def _sb_kernel(q_ref, k_ref, v_ref, u_ref, o_ref, *, tq, scale, hp):
    i = pl.program_id(2)
    u = u_ref[...]
    rep = tq // LANES
    d = LANES

    def block(hh, j, carry, acc, masked, live=None):
        cols = slice(hh * d, (hh + 1) * d)
        start = pl.multiple_of(j * tq, tq)
        q = q_ref[:, cols]
        kj = k_ref[pl.ds(start, tq), cols]
        vj = v_ref[pl.ds(start, tq), cols]
        z = lax.dot_general(q, kj, NT_DIMS, preferred_element_type=F32) * scale
        sp = jnp.maximum(z, 0.0) + jnp.log(1.0 + jnp.exp(-jnp.abs(z)))
        if masked:
            row = lax.broadcasted_iota(jnp.int32, (tq, tq), 0)
            col = lax.broadcasted_iota(jnp.int32, (tq, tq), 1)
            before = col < row
            drop = jnp.where(before, sp, 0.0)
        elif live is not None:
            drop = sp * live
        else:
            drop = sp
        hi = drop.astype(BF16)
        lo = (drop - hi.astype(F32)).astype(BF16)
        later = _dot(hi, u) + _dot(lo, u)
        stick = later + jnp.concatenate([carry] * rep, axis=1)
        w = jnp.exp((z - sp) - stick)
        if masked:
            w = jnp.where(before, w, 0.0)
        elif live is not None:
            w = w * live
        acc = acc + _dot(w.astype(BF16), vj)
        carry = carry + jnp.broadcast_to(later[:, 0:1] + drop[:, 0:1], (tq, LANES))
        return carry, acc

    def some_row_alive(carries):
        m = jnp.min(carries[0])
        for c in carries[1:]:
            m = jnp.minimum(m, jnp.min(c))
        return (m < STICK_DEAD).astype(jnp.int32)

    zero = jnp.zeros((tq, LANES), F32)
    live = (i > 0).astype(F32)
    prev = jnp.maximum(i - 1, 0)
    carries, accs = [], []
    for hh in range(hp):
        carry, acc = block(hh, i, zero, zero, True)
        carry, acc = block(hh, prev, carry, acc, False, live=live)
        carries.append(carry)
        accs.append(acc)

    def cond(state):
        return jnp.logical_and(state[0] >= 0, state[1] > 0)

    def body(state):
        j, _, carries, accs = state
        out = [block(hh, j, carries[hh], accs[hh], False) for hh in range(hp)]
        carries = tuple(o[0] for o in out)
        accs = tuple(o[1] for o in out)
        return j - 1, some_row_alive(carries), carries, accs

    _, _, carries, accs = lax.while_loop(
        cond, body, (i - 2, some_row_alive(carries), tuple(carries), tuple(accs)))
    for hh in range(hp):
        o_ref[:, hh * d:(hh + 1) * d] = accs[hh].astype(o_ref.dtype)


def _sb_attention(pbf, batch, seq, tq, hp):
    t = batch * seq
    h = SB_HEADS
    ng = h // hp
    w = hp * LANES
    nq = seq // tq
    r = jnp.arange(tq)
    u = (r[:, None] > r[None, :]).astype(BF16)
    return pl.pallas_call(
        functools.partial(_sb_kernel, tq=tq, scale=float(LANES) ** -0.5, hp=hp),
        grid=(batch, ng, nq),
        in_specs=[pl.BlockSpec((tq, w), lambda b, hg, i: (b * nq + i, hg)),
                  pl.BlockSpec((seq, w), lambda b, hg, i: (b, ng + hg)),
                  pl.BlockSpec((seq, w), lambda b, hg, i: (b, 2 * ng + hg)),
                  pl.BlockSpec((tq, tq), lambda b, hg, i: (0, 0))],
        out_specs=pl.BlockSpec((tq, w), lambda b, hg, i: (b * nq + i, hg)),
        out_shape=jax.ShapeDtypeStruct((t, h * LANES), BF16),
        compiler_params=pltpu.CompilerParams(
            dimension_semantics=("parallel", "parallel", "arbitrary"), vmem_limit_bytes=VMEM_LIMIT),
        name="stick_breaking",
    )(pbf, pbf, pbf, u)


def _merge_kernel(ohg_ref, osb_ref, x_ref, wgate_ref, wbh_ref, wbs_ref, wo_ref,
                  lg_ref, lb_ref, x1_ref, x1s_ref, *, alpha, tm, nchunk):
    x = x_ref[...]
    xb = x.astype(BF16)
    d = x.shape[1]
    g_hg = _dot(xb, wgate_ref[:, 0:d])
    g_sb = _dot(xb, wgate_ref[:, d:2 * d])
    y_hg = _dot(ohg_ref[...], wbh_ref[...])
    y_sb = _dot(osb_ref[...], wbs_ref[...])
    merged = _sigmoid(g_hg) * y_hg + _sigmoid(g_sb) * y_sb
    hmix = _dot(merged.astype(BF16), wo_ref[...])
    x1 = _layer_norm(alpha * x + hmix, lg_ref[...], lb_ref[...])
    x1_ref[...] = x1
    half = nchunk // 2
    for c in range(half):
        lo = x1[:, c * LANES:(c + 1) * LANES].astype(BF16).astype(F32)
        hi = x1[:, (c + half) * LANES:(c + half + 1) * LANES].astype(BF16).astype(F32)
        word = pltpu.bitcast(hi, jnp.uint32) | (pltpu.bitcast(lo, jnp.uint32) >> 16)
        x1s_ref[pl.ds(c, tm, stride=half), :] = word


def _merge(o_hg, o_sb, x, wgate, wbh, wbs, wo, lg, lb, alpha, tm):
    t, d = x.shape
    nchunk = d // LANES
    row = lambda off: (lambda i: (i, off))
    full = lambda i: (0, 0)
    return pl.pallas_call(
        functools.partial(_merge_kernel, alpha=alpha, tm=tm, nchunk=nchunk),
        grid=(t // tm,),
        in_specs=[pl.BlockSpec((tm, d), row(0)), pl.BlockSpec((tm, d), row(0)),
                  pl.BlockSpec((tm, d), row(0)),
                  pl.BlockSpec((d, 2 * d), full),
                  pl.BlockSpec((d, d), full), pl.BlockSpec((d, d), full), pl.BlockSpec((d, d), full),
                  pl.BlockSpec((1, d), full), pl.BlockSpec((1, d), full)],
        out_specs=[pl.BlockSpec((tm, d), row(0)), pl.BlockSpec((tm * nchunk // 2, LANES), row(0))],
        out_shape=[jax.ShapeDtypeStruct((t, d), F32),
                   jax.ShapeDtypeStruct((t * nchunk // 2, LANES), jnp.uint32)],
        compiler_params=pltpu.CompilerParams(
            dimension_semantics=("parallel",), vmem_limit_bytes=VMEM_LIMIT),
        name="merge_ln1",
    )(o_hg, o_sb, x, wgate, wbh, wbs, wo, lg, lb)


def _route_kernel(x_ref, wh_ref, wl_ref, bias_ref, u_ref, ones_ref,
                  idx_ref, rank_ref, w_ref, cnt_ref, carry, *, tile, n_exp):
    @pl.when(pl.program_id(0) == 0)
    def _():
        carry[...] = jnp.zeros_like(carry)

    x = x_ref[...]
    xh = x.astype(BF16)
    xl = (x - xh.astype(F32)).astype(BF16)
    wh = wh_ref[...]
    nt = functools.partial(lax.dot_general, dimension_numbers=NT_DIMS, preferred_element_type=F32)
    logits = nt(wh, xh) + nt(wh, xl) + nt(wl_ref[...], xh)
    scores = _sigmoid(logits)
    biased = scores + bias_ref[...]

    neg_inf = -jnp.inf
    gsz = n_exp // N_GROUPS
    groups = [biased[g * gsz:(g + 1) * gsz] for g in range(N_GROUPS)]
    gscore = []
    for xg in groups:
        m1 = jnp.max(xg, axis=0, keepdims=True)
        n1 = jnp.sum(jnp.where(xg == m1, 1.0, 0.0), axis=0, keepdims=True)
        m2 = jnp.max(jnp.where(xg < m1, xg, neg_inf), axis=0, keepdims=True)
        gscore.append(m1 + jnp.where(n1 >= 2.0, m1, m2))
    masked = []
    for g in range(N_GROUPS):
        beaten_by = jnp.zeros_like(gscore[g])
        for h in range(N_GROUPS):
            if h < g:
                beaten_by = beaten_by + jnp.where(gscore[h] >= gscore[g], 1.0, 0.0)
            elif h > g:
                beaten_by = beaten_by + jnp.where(gscore[h] > gscore[g], 1.0, 0.0)
        keep = jnp.broadcast_to(beaten_by, (gsz, tile)) < float(TOPK_GROUPS)
        masked.append(jnp.where(keep, groups[g], jnp.finfo(F32).min))
    cur = jnp.concatenate(masked, axis=0)

    eid = lax.broadcasted_iota(jnp.int32, (n_exp, tile), 0).astype(F32)
    chosen = jnp.zeros((n_exp, tile), F32)
    idxs, ws = [], []
    for _ in range(TOP_K):
        m = jnp.max(cur, axis=0, keepdims=True)
        ik = jnp.min(jnp.where(cur == m, eid, float(n_exp)), axis=0, keepdims=True)
        sel = eid == ik
        idxs.append(ik)
        ws.append(jnp.sum(jnp.where(sel, scores, 0.0), axis=0, keepdims=True))
        chosen = chosen + jnp.where(sel, 1.0, 0.0)
        cur = jnp.where(sel, neg_inf, cur)

    chosen_b = chosen.astype(BF16)
    run = carry[...]
    rank = _dot(chosen_b, u_ref[...]) + jnp.concatenate([run] * (tile // LANES), axis=1)
    ranks = [jnp.sum(jnp.where(eid == ik, rank, 0.0), axis=0, keepdims=True) for ik in idxs]
    run = run + _dot(chosen_b, ones_ref[...])
    carry[...] = run
    cnt_ref[...] = run

    wsum = ws[0]
    for wk in ws[1:]:
        wsum = wsum + wk
    idx_ref[...] = jnp.concatenate(idxs, axis=0).astype(jnp.int32)
    rank_ref[...] = jnp.concatenate(ranks, axis=0).astype(jnp.int32)
    w_ref[...] = jnp.concatenate([wk / wsum * ROUTED_SCALE for wk in ws], axis=0)


def _route(x1, w_router, router_bias, tile):
    t, d = x1.shape
    n_exp = w_router.shape[1]
    wt = w_router.astype(F32).T
    wh = wt.astype(BF16)
    wl = (wt - wh.astype(F32)).astype(BF16)
    r = jnp.arange(tile)
    u = (r[:, None] < r[None, :]).astype(BF16)
    ones = jnp.ones((tile, LANES), BF16)
    full = lambda i: (0, 0)
    tok = lambda i: (0, i)
    return pl.pallas_call(
        functools.partial(_route_kernel, tile=tile, n_exp=n_exp),
        grid=(t // tile,),
        in_specs=[pl.BlockSpec((tile, d), lambda i: (i, 0)),
                  pl.BlockSpec((n_exp, d), full), pl.BlockSpec((n_exp, d), full),
                  pl.BlockSpec((n_exp, 1), full),
                  pl.BlockSpec((tile, tile), full), pl.BlockSpec((tile, LANES), full)],
        out_specs=[pl.BlockSpec((TOP_K, tile), tok), pl.BlockSpec((TOP_K, tile), tok),
                   pl.BlockSpec((TOP_K, tile), tok), pl.BlockSpec((n_exp, LANES), full)],
        out_shape=[jax.ShapeDtypeStruct((TOP_K, t), jnp.int32), jax.ShapeDtypeStruct((TOP_K, t), jnp.int32),
                   jax.ShapeDtypeStruct((TOP_K, t), F32), jax.ShapeDtypeStruct((n_exp, LANES), F32)],
        scratch_shapes=[pltpu.VMEM((n_exp, LANES), F32)],
        compiler_params=pltpu.CompilerParams(
            dimension_semantics=("arbitrary",), vmem_limit_bytes=VMEM_LIMIT),
        name="router_topk",
    )(x1, wh, wl, router_bias.astype(F32).reshape(n_exp, 1), u, ones)


def _dest_kernel(idx_ref, rank_ref, pstart_ref, dstx_ref, dsty_ref, *, tile, n_exp, nchunk):
    eid = lax.broadcasted_iota(jnp.int32, (n_exp, tile), 0)
    pstart = jnp.broadcast_to(pstart_ref[...], (n_exp, tile))
    rows = []
    for k in range(TOP_K):
        hit = eid == idx_ref[k:k + 1, :]
        base = jnp.sum(jnp.where(hit, pstart, 0.0), axis=0, keepdims=True)
        rows.append(base.astype(jnp.int32) + rank_ref[k:k + 1, :])
    dst = jnp.concatenate(rows, axis=0)
    dstx_ref[...] = dst * (nchunk // 2)
    dsty_ref[...] = dst * nchunk


def _dest_rows(idx_t, rank_t, pstart, nchunk, tile):
    t = idx_t.shape[1]
    n_exp = pstart.shape[0]
    tok = lambda i: (0, i)
    return pl.pallas_call(
        functools.partial(_dest_kernel, tile=tile, n_exp=n_exp, nchunk=nchunk),
        grid=(t // tile,),
        in_specs=[pl.BlockSpec((TOP_K, tile), tok), pl.BlockSpec((TOP_K, tile), tok),
                  pl.BlockSpec((n_exp, 1), lambda i: (0, 0))],
        out_specs=[pl.BlockSpec((TOP_K, tile), tok)] * 2,
        out_shape=[jax.ShapeDtypeStruct((TOP_K, t), jnp.int32)] * 2,
        compiler_params=pltpu.CompilerParams(
            dimension_semantics=("parallel",), vmem_limit_bytes=VMEM_LIMIT),
        name="dest_rows",
    )(idx_t, rank_t, pstart.astype(F32).reshape(n_exp, 1))


def _dispatch_kernel(pstart_ref, pend_ref, vend_ref, dst_hbm, x_hbm, xs_hbm,
                     dst_s, xbuf, zbuf, isem, lsem, dsem, zsem,
                     *, td, nchunk, mb, sub, n_exp, nsteps, n_blocks):
    i = pl.program_id(0)
    rows = sub * nchunk
    trows = td * nchunk

    def idx_copy(step, slot):
        return pltpu.make_async_copy(dst_hbm.at[:, pl.ds(step * td, td)], dst_s.at[slot], isem.at[slot])

    def tile_load(step, slot):
        start = pl.multiple_of(step * trows, trows)
        return pltpu.make_async_copy(x_hbm.at[pl.ds(start, trows), :], xbuf.at[slot], lsem.at[slot])

    def wait_rows(slot):
        for _ in range(TOP_K):
            pltpu.make_async_copy(xbuf.at[slot], xbuf.at[slot], dsem.at[slot]).wait()

    @pl.when(i == 0)
    def _():
        idx_copy(0, 0).start()
        tile_load(0, 0).start()
        zbuf[...] = jnp.zeros_like(zbuf)
        pieces = mb // sub

        def zero_copy(e, p):
            start = pl.multiple_of((pend_ref[e] - (p + 1) * sub) * nchunk, nchunk)
            return pltpu.make_async_copy(zbuf, xs_hbm.at[pl.ds(start, rows), :], zsem)

        def needs_zero(e, p):
            return pend_ref[e] - vend_ref[e] > p * sub

        def zstart(e, _):
            for p in range(pieces):
                @pl.when(needs_zero(e, p))
                def _():
                    zero_copy(e, p).start()
            return 0

        def zwait(e, _):
            for p in range(pieces):
                @pl.when(needs_zero(e, p))
                def _():
                    zero_copy(e, p).wait()
            return 0

        lax.fori_loop(0, n_exp, zstart, 0)
        lax.fori_loop(0, n_exp, zwait, 0)

        def tail_copy(b):
            start = pl.multiple_of(b * rows, rows)
            return pltpu.make_async_copy(zbuf, xs_hbm.at[pl.ds(start, rows), :], zsem)

        def tstart(b, _):
            tail_copy(b).start()
            return 0

        def twait(b, _):
            tail_copy(b).wait()
            return 0

        first_unused = pend_ref[n_exp - 1] // sub
        lax.fori_loop(first_unused, n_blocks * pieces, tstart, 0)
        lax.fori_loop(first_unused, n_blocks * pieces, twait, 0)

    for cur in range(3):
        nxt = (cur + 1) % 3

        @pl.when(i % 3 == cur)
        def _():
            idx_copy(i, cur).wait()

            @pl.when(i >= 2)
            def _():
                wait_rows(nxt)

            @pl.when(i + 1 < nsteps)
            def _():
                idx_copy(i + 1, nxt).start()
                tile_load(i + 1, nxt).start()

            tile_load(i, cur).wait()

            def body(r, _):
                src = pl.multiple_of(r * nchunk, nchunk)
                for k in range(TOP_K):
                    dst = pl.multiple_of(dst_s[cur, k, r], nchunk)
                    pltpu.make_async_copy(xbuf.at[cur, pl.ds(src, nchunk), :],
                                          xs_hbm.at[pl.ds(dst, nchunk), :], dsem.at[cur]).start(priority=k % 2)
                return 0

            lax.fori_loop(0, td, body, 0, unroll=2)

            @pl.when(i == nsteps - 1)
            def _():
                @pl.when(i >= 1)
                def _():
                    wait_rows((cur + 2) % 3)

                wait_rows(cur)


def _dispatch(x2, dst_t, pstart, pend, vend, p_rows, td):
    n_exp = pstart.shape[0]
    t = dst_t.shape[1]
    nchunk = x2.shape[0] // t
    nsteps = t // td
    grid_spec = pltpu.PrefetchScalarGridSpec(
        num_scalar_prefetch=3,
        grid=(nsteps,),
        in_specs=[pl.BlockSpec(memory_space=pl.ANY)] * 2,
        out_specs=pl.BlockSpec(memory_space=pl.ANY),
        scratch_shapes=[pltpu.SMEM((3, TOP_K, td), jnp.int32),
                        pltpu.VMEM((3, td * nchunk, LANES), x2.dtype),
                        pltpu.VMEM((MOE_SUB * nchunk, LANES), x2.dtype),
                        pltpu.SemaphoreType.DMA((3,)),
                        pltpu.SemaphoreType.DMA((3,)),
                        pltpu.SemaphoreType.DMA((3,)),
                        pltpu.SemaphoreType.DMA],
    )
    return pl.pallas_call(
        functools.partial(_dispatch_kernel, td=td, nchunk=nchunk, mb=MOE_BLOCK, sub=MOE_SUB, n_exp=n_exp,
                          nsteps=nsteps, n_blocks=p_rows // MOE_BLOCK),
        grid_spec=grid_spec,
        out_shape=jax.ShapeDtypeStruct((p_rows * nchunk, LANES), x2.dtype),
        compiler_params=pltpu.CompilerParams(
            dimension_semantics=("arbitrary",), vmem_limit_bytes=VMEM_LIMIT),
        name="moe_dispatch",
    )(pstart, pend, vend, dst_t, x2)


def _experts_kernel(bexp_ref, nused_ref, bvalid_ref, xs_ref, wg_ref, wu_ref, wd_ref, ys_ref,
                    wg_b, wu_b, wd_b, *, mb, sub, nchunk):
    i = pl.program_id(0)
    half = nchunk // 2
    used = i < nused_ref[0]

    @pl.when(used)
    def _():
        new_expert = jnp.logical_or(i == 0, bexp_ref[i] != bexp_ref[jnp.maximum(i - 1, 0)])

        @pl.when(new_expert)
        def _():
            wg_b[...] = wg_ref[0].astype(BF16)
            wu_b[...] = wu_ref[0].astype(BF16)
            wd_b[...] = wd_ref[0].astype(BF16)

    for p in range(mb // sub):
        live = jnp.logical_and(used, bvalid_ref[i] > p * sub)

        @pl.when(live)
        def _():
            x0 = p * sub * half
            words = [xs_ref[pl.ds(x0 + c, sub, stride=half), :] for c in range(half)]
            lows = [pltpu.bitcast(wv << 16, F32) for wv in words]
            highs = [pltpu.bitcast(wv & jnp.uint32(0xFFFF0000), F32) for wv in words]
            xb = jnp.concatenate(lows + highs, axis=1).astype(BF16)
            gate = _dot(xb, wg_b[...])
            up = _dot(xb, wu_b[...])
            hid = (gate * _sigmoid(gate)) * up
            y = _dot(hid.astype(BF16), wd_b[...])
            y0 = p * sub * nchunk
            for c in range(nchunk):
                ys_ref[pl.ds(y0 + c, sub, stride=nchunk), :] = y[:, c * LANES:(c + 1) * LANES]

        @pl.when(jnp.logical_not(live))
        def _():
            ys_ref[pl.ds(p * sub * nchunk, sub * nchunk), :] = jnp.zeros((sub * nchunk, LANES), F32)


def _experts(xs, block_expert, nused, block_valid, w_gate, w_up, w_down):
    e, d, de = w_gate.shape
    nchunk = d // LANES
    mb = MOE_BLOCK
    rows = mb * nchunk
    xrows = rows // 2
    n_blocks = xs.shape[0] // xrows
    blk = lambda i, be, nu, bv: (jnp.minimum(i, nu[0] - 1), 0)
    wsel = lambda i, be, nu, bv: (be[i], 0, 0)
    grid_spec = pltpu.PrefetchScalarGridSpec(
        num_scalar_prefetch=3,
        grid=(n_blocks,),
        in_specs=[pl.BlockSpec((xrows, LANES), blk),
                  pl.BlockSpec((1, d, de), wsel), pl.BlockSpec((1, d, de), wsel),
                  pl.BlockSpec((1, de, d), wsel)],
        out_specs=pl.BlockSpec((rows, LANES), lambda i, be, nu, bv: (i, 0)),
        scratch_shapes=[pltpu.VMEM((d, de), BF16), pltpu.VMEM((d, de), BF16), pltpu.VMEM((de, d), BF16)],
    )
    return pl.pallas_call(
        functools.partial(_experts_kernel, mb=mb, sub=MOE_SUB, nchunk=nchunk),
        grid_spec=grid_spec,
        out_shape=jax.ShapeDtypeStruct((n_blocks * rows, LANES), F32),
        compiler_params=pltpu.CompilerParams(
            dimension_semantics=("arbitrary",), vmem_limit_bytes=VMEM_LIMIT),
        name="routed_experts",
    )(block_expert, nused, block_valid, xs, w_gate, w_up, w_down)


def _final_kernel(dst_hbm, w_hbm, ys_hbm, x1_ref, wsg_ref, wsu_ref, wsd_ref,
                  lg_ref, lb_ref, o_ref, ybuf, rbuf, dst_s, w_s, isem, gsem,
                  *, tm, nchunk, alpha, nsteps):
    i = pl.program_id(0)
    tok_rows = TOP_K * nchunk

    def idx_copies(step, slot):
        cols = pl.ds(step * tm, tm)
        return (pltpu.make_async_copy(dst_hbm.at[:, cols], dst_s.at[slot], isem.at[slot]),
                pltpu.make_async_copy(w_hbm.at[:, cols], w_s.at[slot], isem.at[slot]))

    def issue_gathers(slot):
        def body(r, _):
            for k in range(TOP_K):
                src = pl.multiple_of(dst_s[slot, k, r], nchunk)
                pltpu.make_async_copy(ys_hbm.at[pl.ds(src, nchunk), :],
                                      ybuf.at[slot, pl.ds(r * tok_rows + k * nchunk, nchunk), :],
                                      gsem.at[slot]).start(priority=k % 2)
            return 0

        lax.fori_loop(0, tm, body, 0, unroll=2)

    def combine(slot):
        def body(r, _):
            acc = ybuf[slot, pl.ds(r * tok_rows, nchunk), :] * w_s[slot, 0, r]
            for k in range(1, TOP_K):
                acc = acc + ybuf[slot, pl.ds(r * tok_rows + k * nchunk, nchunk), :] * w_s[slot, k, r]
            rbuf[pl.ds(r * nchunk, nchunk), :] = acc
            return 0

        lax.fori_loop(0, tm, body, 0, unroll=2)

    @pl.when(i == 0)
    def _():
        for cp in idx_copies(0, 0):
            cp.start()
        for cp in idx_copies(0, 0):
            cp.wait()
        issue_gathers(0)

        @pl.when(nsteps > 1)
        def _():
            for cp in idx_copies(1, 1):
                cp.start()

    for cur in range(2):
        nxt = 1 - cur

        @pl.when(i % 2 == cur)
        def _():
            @pl.when(i + 1 < nsteps)
            def _():
                for cp in idx_copies(i + 1, nxt):
                    cp.wait()
                issue_gathers(nxt)

            pltpu.make_async_copy(ybuf.at[cur], ybuf.at[cur], gsem.at[cur]).wait()
            combine(cur)

            @pl.when(i + 2 < nsteps)
            def _():
                for cp in idx_copies(i + 2, cur):
                    cp.start()

    routed = jnp.concatenate([rbuf[pl.ds(c, tm, stride=nchunk), :] for c in range(nchunk)], axis=1)

    x1 = x1_ref[...]
    xb = x1.astype(BF16)
    gate = _dot(xb, wsg_ref[...])
    up = _dot(xb, wsu_ref[...])
    shared = _dot(((gate * _sigmoid(gate)) * up).astype(BF16), wsd_ref[...])
    o_ref[...] = _layer_norm(alpha * x1 + (routed + shared), lg_ref[...], lb_ref[...])


def _final(ys, dst_t, w_t, x1, wsg, wsu, wsd, lg, lb, alpha, tm):
    t, d = x1.shape
    nchunk = d // LANES
    ds = wsg.shape[1]
    nsteps = t // tm
    full = lambda i: (0, 0)
    return pl.pallas_call(
        functools.partial(_final_kernel, tm=tm, nchunk=nchunk, alpha=alpha, nsteps=nsteps),
        grid=(nsteps,),
        in_specs=[pl.BlockSpec(memory_space=pl.ANY)] * 3 + [
            pl.BlockSpec((tm, d), lambda i: (i, 0)),
            pl.BlockSpec((d, ds), full), pl.BlockSpec((d, ds), full), pl.BlockSpec((ds, d), full),
            pl.BlockSpec((1, d), full), pl.BlockSpec((1, d), full)],
        out_specs=pl.BlockSpec((tm, d), lambda i: (i, 0)),
        out_shape=jax.ShapeDtypeStruct((t, d), F32),
        scratch_shapes=[pltpu.VMEM((2, tm * TOP_K * nchunk, LANES), F32),
                        pltpu.VMEM((tm * nchunk, LANES), F32),
                        pltpu.SMEM((2, TOP_K, tm), jnp.int32),
                        pltpu.SMEM((2, TOP_K, tm), F32),
                        pltpu.SemaphoreType.DMA((2,)),
                        pltpu.SemaphoreType.DMA((2,))],
        compiler_params=pltpu.CompilerParams(
            dimension_semantics=("arbitrary",), vmem_limit_bytes=VMEM_LIMIT),
        name="combine_shared_ln2",
    )(dst_t, w_t, ys, x1, wsg, wsu, wsd, lg, lb)


def _block_tables(counts, n_rows):
    mb = MOE_BLOCK
    n_exp = counts.shape[0]
    padded = (counts + mb - 1) // mb * mb
    pad_end = jnp.cumsum(padded)
    pad_start = pad_end - padded
    valid_end = pad_start + counts
    n_blocks = n_rows // mb
    first_row = jnp.arange(n_blocks, dtype=jnp.int32) * mb
    block_expert = jnp.minimum(
        jnp.sum((pad_end[None, :] <= first_row[:, None]).astype(jnp.int32), axis=1), n_exp - 1)
    block_valid = jnp.clip(valid_end[block_expert] - first_row, 0, mb)
    nused = (pad_end[-1:] // mb).astype(jnp.int32)
    i32 = lambda a: a.astype(jnp.int32)
    return i32(pad_start), i32(pad_end), i32(valid_end), block_expert, i32(block_valid), nused


def _tile(n, pref):
    while n % pref:
        pref //= 2
    return pref


def kernel(x, w_in, lower_bounds, hg_norm_g, w_branch_hg, w_branch_sb, w_out, ln1_g, ln1_b,
           w_router, router_bias, w_exp_gate, w_exp_up, w_exp_down,
           w_sh_gate, w_sh_up, w_sh_down, ln2_g, ln2_b):
    depth = w_in.shape[0]
    assert depth == 1, "single-layer block only"
    batch, seq, d = x.shape
    t = batch * seq
    n_exp = w_router.shape[-1]
    alpha = (2.0 * depth) ** 0.25
    nchunk = d // LANES

    lb = jnp.cumsum(jax.nn.softmax(lower_bounds.astype(F32), axis=0), axis=0)[0]
    xf = x.reshape(t, d)

    w = w_in[0].astype(BF16)
    c = d
    o_hg = _hgrn(xf, w[:, 0:4 * c], lb, hg_norm_g[0].astype(F32), batch, seq, _tile(seq, 256))
    pbf = _matmul(xf, w[:, 4 * c:7 * c], BF16, _tile(t, 2048), 1024)
    o_sb = _sb_attention(pbf, batch, seq, _tile(seq, 256), 4)

    x1, x1s = _merge(o_hg, o_sb, xf, w[:, 7 * c:9 * c],
                     w_branch_hg[0].astype(BF16), w_branch_sb[0].astype(BF16), w_out[0].astype(BF16),
                     ln1_g[0].reshape(1, d).astype(F32), ln1_b[0].reshape(1, d).astype(F32),
                     alpha, _tile(t, 512))

    idx_t, rank_t, w_t, cnt = _route(x1, w_router[0], router_bias[0], _tile(t, 256))
    p_rows = t * TOP_K + n_exp * MOE_BLOCK
    pstart, pend, vend, block_expert, block_valid, nused = _block_tables(cnt[:, 0].astype(jnp.int32), p_rows)
    dstx_t, dsty_t = _dest_rows(idx_t, rank_t, pstart, nchunk, _tile(t, 256))
    xs = _dispatch(x1s, dstx_t, pstart, pend, vend, p_rows, _tile(t, 256))
    ys = _experts(xs, block_expert, nused, block_valid, w_exp_gate[0], w_exp_up[0], w_exp_down[0])
    out = _final(ys, dsty_t, w_t, x1,
                 w_sh_gate[0].astype(BF16), w_sh_up[0].astype(BF16), w_sh_down[0].astype(BF16),
                 ln2_g[0].reshape(1, d).astype(F32), ln2_b[0].reshape(1, d).astype(F32),
                 alpha, _tile(t, 256))
    return out.reshape(batch, seq, d)
```

```python
import functools

import jax
import jax.numpy as jnp
from jax import lax
from jax.experimental import pallas as pl
from jax.experimental.pallas import tpu as pltpu

F32 = jnp.float32
BF16 = jnp.bfloat16

LANES = 128
SUBLANES = 8
VMEM_LIMIT = 48 * 1024 * 1024

HG_HEADS = 8
HG_CHUNK = 32
SB_HEADS = 8
N_GROUPS = 8
TOPK_GROUPS = 4
TOP_K = 8
ROUTED_SCALE = 2.5
MOE_BLOCK = 512
MOE_SUB = 512
LN_EPS = 1e-5
RMS_EPS = 1e-6
STICK_DEAD = 110.0

NT_DIMS = (((1,), (1,)), ((), ()))
TN_DIMS = (((0,), (0,)), ((), ()))


def _dot(a, b):
    return jnp.dot(a, b, preferred_element_type=F32)


def _sigmoid(x):
    return 1.0 / (1.0 + jnp.exp(-x))


def _split2(x):
    hi = x.astype(BF16)
    lo = (x - hi.astype(F32)).astype(BF16)
    return hi, lo


def _layer_norm(r, g, b):
    mu = jnp.mean(r, axis=-1, keepdims=True)
    d = r - mu
    var = jnp.mean(d * d, axis=-1, keepdims=True)
    return d * lax.rsqrt(var + LN_EPS) * g + b


def _mm_kernel(x_ref, w_ref, o_ref):
    o_ref[...] = _dot(x_ref[...].astype(BF16), w_ref[...]).astype(o_ref.dtype)


def _matmul(x, w, out_dtype, tm, tn):
    m, k = x.shape
    n = w.shape[1]
    return pl.pallas_call(
        _mm_kernel,
        grid=(m // tm, n // tn),
        in_specs=[pl.BlockSpec((tm, k), lambda i, j: (i, 0)),
                  pl.BlockSpec((k, tn), lambda i, j: (0, j))],
        out_specs=pl.BlockSpec((tm, tn), lambda i, j: (i, j)),
        out_shape=jax.ShapeDtypeStruct((m, n), out_dtype),
        compiler_params=pltpu.CompilerParams(
            dimension_semantics=("parallel", "arbitrary"), vmem_limit_bytes=VMEM_LIMIT),
        name="in_proj",
    )(x, w)


def _hgrn_kernel(x_ref, w_ref, lb_ref, g_ref, o_ref, st_ref, *, ts, chunk, heads):
    @pl.when(pl.program_id(1) == 0)
    def _():
        st_ref[...] = jnp.zeros_like(st_ref)

    shift = chunk.bit_length() - 1
    row = lax.broadcasted_iota(jnp.int32, (ts, ts), 0)
    col = lax.broadcasted_iota(jnp.int32, (ts, ts), 1)
    same = (row >> shift) == (col >> shift)
    causal = jnp.logical_and(same, col <= row)
    tri = jnp.where(causal, 1.0, 0.0).astype(BF16)
    ones = jnp.where(same, 1.0, 0.0).astype(BF16)
    d = LANES
    hd = heads * d

    xb = x_ref[...].astype(BF16)
    hq_all = _dot(xb, w_ref[:, 0:hd])
    hf_all = _dot(xb, w_ref[:, hd:2 * hd])
    hi_all = _dot(xb, w_ref[:, 2 * hd:3 * hd]).astype(BF16)
    hg_all = _dot(xb, w_ref[:, 3 * hd:4 * hd])

    for hh in range(heads):
        cols = slice(hh * d, (hh + 1) * d)
        lb = lb_ref[:, cols]
        hf = hf_all[:, cols]
        log_f = jnp.log(lb + (1.0 - lb) * _sigmoid(hf))
        k_in = (1.0 - lb) * _sigmoid(-hf)

        parts = jnp.concatenate(_split2(log_f), axis=1)
        cs = _dot(tri, parts)
        tot = _dot(ones, parts)
        b = cs[:, :d] + cs[:, d:]
        b_end = tot[:, :d] + tot[:, d:]

        q = hq_all[:, cols]
        q_start = (q * jnp.exp(b)).astype(BF16)
        q_end = (q * jnp.exp(b - b_end)).astype(BF16)
        k_end = (k_in * jnp.exp(b_end - b)).astype(BF16)
        v = hi_all[:, cols]

        scores = lax.dot_general(q_end, k_end, NT_DIMS, preferred_element_type=F32)
        scores = jnp.where(causal, scores, 0.0)
        o_intra = _dot(scores.astype(BF16), v)

        dec = jnp.exp(b_end)
        st = st_ref[hh]
        outs = []
        for c in range(ts // chunk):
            lo = c * chunk
            outs.append(lax.dot_general(q_start[lo:lo + chunk], st.astype(BF16), NT_DIMS,
                                        preferred_element_type=F32))
            kv = lax.dot_general(v[lo:lo + chunk], k_end[lo:lo + chunk], TN_DIMS,
                                 preferred_element_type=F32)
            st = dec[lo:lo + 1, :] * st + kv
        st_ref[hh] = st
        o = o_intra + jnp.concatenate(outs, axis=0)

        o = o * lax.rsqrt(jnp.mean(o * o, axis=-1, keepdims=True) + RMS_EPS)
        o = o * g_ref[:, cols]
        hg = hg_all[:, cols]
        o_ref[:, cols] = (o * (hg * _sigmoid(hg))).astype(o_ref.dtype)


def _hgrn(x, w_hg, lb, g, batch, seq, ts):
    t, dm = x.shape
    h = HG_HEADS
    hd = h * LANES
    ns = seq // ts
    full = lambda b, s: (0, 0)
    return pl.pallas_call(
        functools.partial(_hgrn_kernel, ts=ts, chunk=HG_CHUNK, heads=h),
        grid=(batch, ns),
        in_specs=[pl.BlockSpec((ts, dm), lambda b, s: (b * ns + s, 0)),
                  pl.BlockSpec((dm, 4 * hd), full),
                  pl.BlockSpec((1, hd), full),
                  pl.BlockSpec((1, hd), full)],
        out_specs=pl.BlockSpec((ts, hd), lambda b, s: (b * ns + s, 0)),
        out_shape=jax.ShapeDtypeStruct((t, hd), BF16),
        scratch_shapes=[pltpu.VMEM((h, LANES, LANES), F32)],
        compiler_params=pltpu.CompilerParams(
            dimension_semantics=("parallel", "arbitrary"), vmem_limit_bytes=VMEM_LIMIT),
        name="hgrn2",
    )(x, w_hg, lb.reshape(1, hd), g.reshape(1, hd))


def _sb_kernel(q_ref, k_ref, v_ref, u_ref, o_ref, *, tq, scale, hp):
    i = pl.program_id(2)
    u = u_ref[...]
    rep = tq // LANES
    d = LANES

    def block(hh, j, carry, acc, masked, live=None):
        cols = slice(hh * d, (hh + 1) * d)
        start = pl.multiple_of(j * tq, tq)
        q = q_ref[:, cols]
        kj = k_ref[pl.ds(start, tq), cols]
        vj = v_ref[pl.ds(start, tq), cols]
        z = lax.dot_general(q, kj, NT_DIMS, preferred_element_type=F32) * scale
        sp = jnp.maximum(z, 0.0) + jnp.log(1.0 + jnp.exp(-jnp.abs(z)))
        if masked:
            row = lax.broadcasted_iota(jnp.int32, (tq, tq), 0)
            col = lax.broadcasted_iota(jnp.int32, (tq, tq), 1)
            before = col < row
            drop = jnp.where(before, sp, 0.0)
        elif live is not None:
            drop = sp * live
        else:
            drop = sp
        hi = drop.astype(BF16)
        lo = (drop - hi.astype(F32)).astype(BF16)
        later = _dot(hi, u) + _dot(lo, u)
        stick = later + jnp.concatenate([carry] * rep, axis=1)
        w = jnp.exp((z - sp) - stick)
        if masked:
            w = jnp.where(before, w, 0.0)
        elif live is not None:
            w = w * live
        acc = acc + _dot(w.astype(BF16), vj)
        carry = carry + jnp.broadcast_to(later[:, 0:1] + drop[:, 0:1], (tq, LANES))
        return carry, acc

    def some_row_alive(carries):
        m = jnp.min(carries[0])
        for c in carries[1:]:
            m = jnp.minimum(m, jnp.min(c))
        return (m < STICK_DEAD).astype(jnp.int32)

    zero = jnp.zeros((tq, LANES), F32)
    live = (i > 0).astype(F32)
    prev = jnp.maximum(i - 1, 0)
    carries, accs = [], []
    for hh in range(hp):
        carry, acc = block(hh, i, zero, zero, True)
        carry, acc = block(hh, prev, carry, acc, False, live=live)
        carries.append(carry)
        accs.append(acc)

    def cond(state):
        return jnp.logical_and(state[0] >= 0, state[1] > 0)

    def body(state):
        j, _, carries, accs = state
        out = [block(hh, j, carries[hh], accs[hh], False) for hh in range(hp)]
        carries = tuple(o[0] for o in out)
        accs = tuple(o[1] for o in out)
        return j - 1, some_row_alive(carries), carries, accs

    _, _, carries, accs = lax.while_loop(
        cond, body, (i - 2, some_row_alive(carries), tuple(carries), tuple(accs)))
    for hh in range(hp):
        o_ref[:, hh * d:(hh + 1) * d] = accs[hh].astype(o_ref.dtype)


def _sb_attention(pbf, batch, seq, tq, hp):
    t = batch * seq
    h = SB_HEADS
    ng = h // hp
    w = hp * LANES
    nq = seq // tq
    r = jnp.arange(tq)
    u = (r[:, None] > r[None, :]).astype(BF16)
    return pl.pallas_call(
        functools.partial(_sb_kernel, tq=tq, scale=float(LANES) ** -0.5, hp=hp),
        grid=(batch, ng, nq),
        in_specs=[pl.BlockSpec((tq, w), lambda b, hg, i: (b * nq + i, hg)),
                  pl.BlockSpec((seq, w), lambda b, hg, i: (b, ng + hg)),
                  pl.BlockSpec((seq, w), lambda b, hg, i: (b, 2 * ng + hg)),
                  pl.BlockSpec((tq, tq), lambda b, hg, i: (0, 0))],
        out_specs=pl.BlockSpec((tq, w), lambda b, hg, i: (b * nq + i, hg)),
        out_shape=jax.ShapeDtypeStruct((t, h * LANES), BF16),
        compiler_params=pltpu.CompilerParams(
            dimension_semantics=("parallel", "parallel", "arbitrary"), vmem_limit_bytes=VMEM_LIMIT),
        name="stick_breaking",
    )(pbf, pbf, pbf, u)


def _merge_kernel(ohg_ref, osb_ref, x_ref, wgate_ref, wbh_ref, wbs_ref, wo_ref,
                  lg_ref, lb_ref, x1_ref, x1s_ref, *, alpha, tm, nchunk):
    x = x_ref[...]
    xb = x.astype(BF16)
    d = x.shape[1]
    g_hg = _dot(xb, wgate_ref[:, 0:d])
    g_sb = _dot(xb, wgate_ref[:, d:2 * d])
    y_hg = _dot(ohg_ref[...], wbh_ref[...])
    y_sb = _dot(osb_ref[...], wbs_ref[...])
    merged = _sigmoid(g_hg) * y_hg + _sigmoid(g_sb) * y_sb
    hmix = _dot(merged.astype(BF16), wo_ref[...])
    x1 = _layer_norm(alpha * x + hmix, lg_ref[...], lb_ref[...])
    x1_ref[...] = x1
    half = nchunk // 2
    for c in range(half):
        lo = x1[:, c * LANES:(c + 1) * LANES].astype(BF16).astype(F32)
        hi = x1[:, (c + half) * LANES:(c + half + 1) * LANES].astype(BF16).astype(F32)
        word = pltpu.bitcast(hi, jnp.uint32) | (pltpu.bitcast(lo, jnp.uint32) >> 16)
        x1s_ref[pl.ds(c, tm, stride=half), :] = word


def _merge(o_hg, o_sb, x, wgate, wbh, wbs, wo, lg, lb, alpha, tm):
    t, d = x.shape
    nchunk = d // LANES
    row = lambda off: (lambda i: (i, off))
    full = lambda i: (0, 0)
    return pl.pallas_call(
        functools.partial(_merge_kernel, alpha=alpha, tm=tm, nchunk=nchunk),
        grid=(t // tm,),
        in_specs=[pl.BlockSpec((tm, d), row(0)), pl.BlockSpec((tm, d), row(0)),
                  pl.BlockSpec((tm, d), row(0)),
                  pl.BlockSpec((d, 2 * d), full),
                  pl.BlockSpec((d, d), full), pl.BlockSpec((d, d), full), pl.BlockSpec((d, d), full),
                  pl.BlockSpec((1, d), full), pl.BlockSpec((1, d), full)],
        out_specs=[pl.BlockSpec((tm, d), row(0)), pl.BlockSpec((tm * nchunk // 2, LANES), row(0))],
        out_shape=[jax.ShapeDtypeStruct((t, d), F32),
                   jax.ShapeDtypeStruct((t * nchunk // 2, LANES), jnp.uint32)],
        compiler_params=pltpu.CompilerParams(
            dimension_semantics=("parallel",), vmem_limit_bytes=VMEM_LIMIT),
        name="merge_ln1",
    )(o_hg, o_sb, x, wgate, wbh, wbs, wo, lg, lb)


def _route_kernel(x_ref, wh_ref, wl_ref, bias_ref, u_ref, ones_ref,
                  idx_ref, rank_ref, w_ref, cnt_ref, carry, *, tile, n_exp):
    @pl.when(pl.program_id(0) == 0)
    def _():
        carry[...] = jnp.zeros_like(carry)

    x = x_ref[...]
    xh = x.astype(BF16)
    xl = (x - xh.astype(F32)).astype(BF16)
    wh = wh_ref[...]
    nt = functools.partial(lax.dot_general, dimension_numbers=NT_DIMS, preferred_element_type=F32)
    logits = nt(wh, xh) + nt(wh, xl) + nt(wl_ref[...], xh)
    scores = _sigmoid(logits)
    biased = scores + bias_ref[...]

    neg_inf = -jnp.inf
    gsz = n_exp // N_GROUPS
    groups = [biased[g * gsz:(g + 1) * gsz] for g in range(N_GROUPS)]
    gscore = []
    for xg in groups:
        m1 = jnp.max(xg, axis=0, keepdims=True)
        n1 = jnp.sum(jnp.where(xg == m1, 1.0, 0.0), axis=0, keepdims=True)
        m2 = jnp.max(jnp.where(xg < m1, xg, neg_inf), axis=0, keepdims=True)
        gscore.append(m1 + jnp.where(n1 >= 2.0, m1, m2))
    masked = []
    for g in range(N_GROUPS):
        beaten_by = jnp.zeros_like(gscore[g])
        for h in range(N_GROUPS):
            if h < g:
                beaten_by = beaten_by + jnp.where(gscore[h] >= gscore[g], 1.0, 0.0)
            elif h > g:
                beaten_by = beaten_by + jnp.where(gscore[h] > gscore[g], 1.0, 0.0)
        keep = jnp.broadcast_to(beaten_by, (gsz, tile)) < float(TOPK_GROUPS)
        masked.append(jnp.where(keep, groups[g], jnp.finfo(F32).min))
    cur = jnp.concatenate(masked, axis=0)

    eid = lax.broadcasted_iota(jnp.int32, (n_exp, tile), 0).astype(F32)
    chosen = jnp.zeros((n_exp, tile), F32)
    idxs, ws = [], []
    for _ in range(TOP_K):
        m = jnp.max(cur, axis=0, keepdims=True)
        ik = jnp.min(jnp.where(cur == m, eid, float(n_exp)), axis=0, keepdims=True)
        sel = eid == ik
        idxs.append(ik)
        ws.append(jnp.sum(jnp.where(sel, scores, 0.0), axis=0, keepdims=True))
        chosen = chosen + jnp.where(sel, 1.0, 0.0)
        cur = jnp.where(sel, neg_inf, cur)

    chosen_b = chosen.astype(BF16)
    run = carry[...]
    rank = _dot(chosen_b, u_ref[...]) + jnp.concatenate([run] * (tile // LANES), axis=1)
    ranks = [jnp.sum(jnp.where(eid == ik, rank, 0.0), axis=0, keepdims=True) for ik in idxs]
    run = run + _dot(chosen_b, ones_ref[...])
    carry[...] = run
    cnt_ref[...] = run

    wsum = ws[0]
    for wk in ws[1:]:
        wsum = wsum + wk
    idx_ref[...] = jnp.concatenate(idxs, axis=0).astype(jnp.int32)
    rank_ref[...] = jnp.concatenate(ranks, axis=0).astype(jnp.int32)
    w_ref[...] = jnp.concatenate([wk / wsum * ROUTED_SCALE for wk in ws], axis=0)


def _route(x1, w_router, router_bias, tile):
    t, d = x1.shape
    n_exp = w_router.shape[1]
    wt = w_router.astype(F32).T
    wh = wt.astype(BF16)
    wl = (wt - wh.astype(F32)).astype(BF16)
    r = jnp.arange(tile)
    u = (r[:, None] < r[None, :]).astype(BF16)
    ones = jnp.ones((tile, LANES), BF16)
    full = lambda i: (0, 0)
    tok = lambda i: (0, i)
    return pl.pallas_call(
        functools.partial(_route_kernel, tile=tile, n_exp=n_exp),
        grid=(t // tile,),
        in_specs=[pl.BlockSpec((tile, d), lambda i: (i, 0)),
                  pl.BlockSpec((n_exp, d), full), pl.BlockSpec((n_exp, d), full),
                  pl.BlockSpec((n_exp, 1), full),
                  pl.BlockSpec((tile, tile), full), pl.BlockSpec((tile, LANES), full)],
        out_specs=[pl.BlockSpec((TOP_K, tile), tok), pl.BlockSpec((TOP_K, tile), tok),
                   pl.BlockSpec((TOP_K, tile), tok), pl.BlockSpec((n_exp, LANES), full)],
        out_shape=[jax.ShapeDtypeStruct((TOP_K, t), jnp.int32), jax.ShapeDtypeStruct((TOP_K, t), jnp.int32),
                   jax.ShapeDtypeStruct((TOP_K, t), F32), jax.ShapeDtypeStruct((n_exp, LANES), F32)],
        scratch_shapes=[pltpu.VMEM((n_exp, LANES), F32)],
        compiler_params=pltpu.CompilerParams(
            dimension_semantics=("arbitrary",), vmem_limit_bytes=VMEM_LIMIT),
        name="router_topk",
    )(x1, wh, wl, router_bias.astype(F32).reshape(n_exp, 1), u, ones)


def _dest_kernel(idx_ref, rank_ref, pstart_ref, dstx_ref, dsty_ref, *, tile, n_exp, nchunk):
    eid = lax.broadcasted_iota(jnp.int32, (n_exp, tile), 0)
    pstart = jnp.broadcast_to(pstart_ref[...], (n_exp, tile))
    rows = []
    for k in range(TOP_K):
        hit = eid == idx_ref[k:k + 1, :]
        base = jnp.sum(jnp.where(hit, pstart, 0.0), axis=0, keepdims=True)
        rows.append(base.astype(jnp.int32) + rank_ref[k:k + 1, :])
    dst = jnp.concatenate(rows, axis=0)
    dstx_ref[...] = dst * (nchunk // 2)
    dsty_ref[...] = dst * nchunk


def _dest_rows(idx_t, rank_t, pstart, nchunk, tile):
    t = idx_t.shape[1]
    n_exp = pstart.shape[0]
    tok = lambda i: (0, i)
    return pl.pallas_call(
        functools.partial(_dest_kernel, tile=tile, n_exp=n_exp, nchunk=nchunk),
        grid=(t // tile,),
        in_specs=[pl.BlockSpec((TOP_K, tile), tok), pl.BlockSpec((TOP_K, tile), tok),
                  pl.BlockSpec((n_exp, 1), lambda i: (0, 0))],
        out_specs=[pl.BlockSpec((TOP_K, tile), tok)] * 2,
        out_shape=[jax.ShapeDtypeStruct((TOP_K, t), jnp.int32)] * 2,
        compiler_params=pltpu.CompilerParams(
            dimension_semantics=("parallel",), vmem_limit_bytes=VMEM_LIMIT),
        name="dest_rows",
    )(idx_t, rank_t, pstart.astype(F32).reshape(n_exp, 1))


def _dispatch_kernel(pstart_ref, pend_ref, vend_ref, dst_hbm, x_hbm, xs_hbm,
                     dst_s, xbuf, zbuf, isem, lsem, dsem, zsem,
                     *, td, nchunk, mb, sub, n_exp, nsteps, n_blocks):
    i = pl.program_id(0)
    rows = sub * nchunk
    trows = td * nchunk

    def idx_copy(step, slot):
        return pltpu.make_async_copy(dst_hbm.at[:, pl.ds(step * td, td)], dst_s.at[slot], isem.at[slot])

    def tile_load(step, slot):
        start = pl.multiple_of(step * trows, trows)
        return pltpu.make_async_copy(x_hbm.at[pl.ds(start, trows), :], xbuf.at[slot], lsem.at[slot])

    def wait_rows(slot):
        for _ in range(TOP_K):
            pltpu.make_async_copy(xbuf.at[slot], xbuf.at[slot], dsem.at[slot]).wait()

    @pl.when(i == 0)
    def _():
        idx_copy(0, 0).start()
        tile_load(0, 0).start()
        zbuf[...] = jnp.zeros_like(zbuf)
        pieces = mb // sub

        def zero_copy(e, p):
            start = pl.multiple_of((pend_ref[e] - (p + 1) * sub) * nchunk, nchunk)
            return pltpu.make_async_copy(zbuf, xs_hbm.at[pl.ds(start, rows), :], zsem)

        def needs_zero(e, p):
            return pend_ref[e] - vend_ref[e] > p * sub

        def zstart(e, _):
            for p in range(pieces):
                @pl.when(needs_zero(e, p))
                def _():
                    zero_copy(e, p).start()
            return 0

        def zwait(e, _):
            for p in range(pieces):
                @pl.when(needs_zero(e, p))
                def _():
                    zero_copy(e, p).wait()
            return 0

        lax.fori_loop(0, n_exp, zstart, 0)
        lax.fori_loop(0, n_exp, zwait, 0)

        def tail_copy(b):
            start = pl.multiple_of(b * rows, rows)
            return pltpu.make_async_copy(zbuf, xs_hbm.at[pl.ds(start, rows), :], zsem)

        def tstart(b, _):
            tail_copy(b).start()
            return 0

        def twait(b, _):
            tail_copy(b).wait()
            return 0

        first_unused = pend_ref[n_exp - 1] // sub
        lax.fori_loop(first_unused, n_blocks * pieces, tstart, 0)
        lax.fori_loop(first_unused, n_blocks * pieces, twait, 0)

    for cur in range(3):
        nxt = (cur + 1) % 3

        @pl.when(i % 3 == cur)
        def _():
            idx_copy(i, cur).wait()

            @pl.when(i >= 2)
            def _():
                wait_rows(nxt)

            @pl.when(i + 1 < nsteps)
            def _():
                idx_copy(i + 1, nxt).start()
                tile_load(i + 1, nxt).start()

            tile_load(i, cur).wait()

            def body(r, _):
                src = pl.multiple_of(r * nchunk, nchunk)
                for k in range(TOP_K):
                    dst = pl.multiple_of(dst_s[cur, k, r], nchunk)
                    pltpu.make_async_copy(xbuf.at[cur, pl.ds(src, nchunk), :],
                                          xs_hbm.at[pl.ds(dst, nchunk), :], dsem.at[cur]).start(priority=k % 2)
                return 0

            lax.fori_loop(0, td, body, 0, unroll=2)

            @pl.when(i == nsteps - 1)
            def _():
                @pl.when(i >= 1)
                def _():
                    wait_rows((cur + 2) % 3)

                wait_rows(cur)


def _dispatch(x2, dst_t, pstart, pend, vend, p_rows, td):
    n_exp = pstart.shape[0]
    t = dst_t.shape[1]
    nchunk = x2.shape[0] // t
    nsteps = t // td
    grid_spec = pltpu.PrefetchScalarGridSpec(
        num_scalar_prefetch=3,
        grid=(nsteps,),
        in_specs=[pl.BlockSpec(memory_space=pl.ANY)] * 2,
        out_specs=pl.BlockSpec(memory_space=pl.ANY),
        scratch_shapes=[pltpu.SMEM((3, TOP_K, td), jnp.int32),
                        pltpu.VMEM((3, td * nchunk, LANES), x2.dtype),
                        pltpu.VMEM((MOE_SUB * nchunk, LANES), x2.dtype),
                        pltpu.SemaphoreType.DMA((3,)),
                        pltpu.SemaphoreType.DMA((3,)),
                        pltpu.SemaphoreType.DMA((3,)),
                        pltpu.SemaphoreType.DMA],
    )
    return pl.pallas_call(
        functools.partial(_dispatch_kernel, td=td, nchunk=nchunk, mb=MOE_BLOCK, sub=MOE_SUB, n_exp=n_exp,
                          nsteps=nsteps, n_blocks=p_rows // MOE_BLOCK),
        grid_spec=grid_spec,
        out_shape=jax.ShapeDtypeStruct((p_rows * nchunk, LANES), x2.dtype),
        compiler_params=pltpu.CompilerParams(
            dimension_semantics=("arbitrary",), vmem_limit_bytes=VMEM_LIMIT),
        name="moe_dispatch",
    )(pstart, pend, vend, dst_t, x2)


def _experts_kernel(bexp_ref, nused_ref, bvalid_ref, xs_ref, wg_ref, wu_ref, wd_ref, ys_ref,
                    wg_b, wu_b, wd_b, *, mb, sub, nchunk):
    i = pl.program_id(0)
    half = nchunk // 2
    used = i < nused_ref[0]

    @pl.when(used)
    def _():
        new_expert = jnp.logical_or(i == 0, bexp_ref[i] != bexp_ref[jnp.maximum(i - 1, 0)])

        @pl.when(new_expert)
        def _():
            wg_b[...] = wg_ref[0].astype(BF16)
            wu_b[...] = wu_ref[0].astype(BF16)
            wd_b[...] = wd_ref[0].astype(BF16)

    for p in range(mb // sub):
        live = jnp.logical_and(used, bvalid_ref[i] > p * sub)

        @pl.when(live)
        def _():
            x0 = p * sub * half
            words = [xs_ref[pl.ds(x0 + c, sub, stride=half), :] for c in range(half)]
            lows = [pltpu.bitcast(wv << 16, F32) for wv in words]
            highs = [pltpu.bitcast(wv & jnp.uint32(0xFFFF0000), F32) for wv in words]
            xb = jnp.concatenate(lows + highs, axis=1).astype(BF16)
            gate = _dot(xb, wg_b[...])
            up = _dot(xb, wu_b[...])
            hid = (gate * _sigmoid(gate)) * up
            y = _dot(hid.astype(BF16), wd_b[...])
            y0 = p * sub * nchunk
            for c in range(nchunk):
                ys_ref[pl.ds(y0 + c, sub, stride=nchunk), :] = y[:, c * LANES:(c + 1) * LANES]

        @pl.when(jnp.logical_not(live))
        def _():
            ys_ref[pl.ds(p * sub * nchunk, sub * nchunk), :] = jnp.zeros((sub * nchunk, LANES), F32)


def _experts(xs, block_expert, nused, block_valid, w_gate, w_up, w_down):
    e, d, de = w_gate.shape
    nchunk = d // LANES
    mb = MOE_BLOCK
    rows = mb * nchunk
    xrows = rows // 2
    n_blocks = xs.shape[0] // xrows
    blk = lambda i, be, nu, bv: (jnp.minimum(i, nu[0] - 1), 0)
    wsel = lambda i, be, nu, bv: (be[i], 0, 0)
    grid_spec = pltpu.PrefetchScalarGridSpec(
        num_scalar_prefetch=3,
        grid=(n_blocks,),
        in_specs=[pl.BlockSpec((xrows, LANES), blk),
                  pl.BlockSpec((1, d, de), wsel), pl.BlockSpec((1, d, de), wsel),
                  pl.BlockSpec((1, de, d), wsel)],
        out_specs=pl.BlockSpec((rows, LANES), lambda i, be, nu, bv: (i, 0)),
        scratch_shapes=[pltpu.VMEM((d, de), BF16), pltpu.VMEM((d, de), BF16), pltpu.VMEM((de, d), BF16)],
    )
    return pl.pallas_call(
        functools.partial(_experts_kernel, mb=mb, sub=MOE_SUB, nchunk=nchunk),
        grid_spec=grid_spec,
        out_shape=jax.ShapeDtypeStruct((n_blocks * rows, LANES), F32),
        compiler_params=pltpu.CompilerParams(
            dimension_semantics=("arbitrary",), vmem_limit_bytes=VMEM_LIMIT),
        name="routed_experts",
    )(block_expert, nused, block_valid, xs, w_gate, w_up, w_down)


def _final_kernel(dst_hbm, w_hbm, ys_hbm, x1_ref, wsg_ref, wsu_ref, wsd_ref,
                  lg_ref, lb_ref, o_ref, ybuf, rbuf, dst_s, w_s, isem, gsem,
                  *, tm, nchunk, alpha, nsteps):
    i = pl.program_id(0)
    tok_rows = TOP_K * nchunk

    def idx_copies(step, slot):
        cols = pl.ds(step * tm, tm)
        return (pltpu.make_async_copy(dst_hbm.at[:, cols], dst_s.at[slot], isem.at[slot]),
                pltpu.make_async_copy(w_hbm.at[:, cols], w_s.at[slot], isem.at[slot]))

    def issue_gathers(slot):
        def body(r, _):
            for k in range(TOP_K):
                src = pl.multiple_of(dst_s[slot, k, r], nchunk)
                pltpu.make_async_copy(ys_hbm.at[pl.ds(src, nchunk), :],
                                      ybuf.at[slot, pl.ds(r * tok_rows + k * nchunk, nchunk), :],
                                      gsem.at[slot]).start(priority=k % 2)
            return 0

        lax.fori_loop(0, tm, body, 0, unroll=2)

    def combine(slot):
        def body(r, _):
            acc = ybuf[slot, pl.ds(r * tok_rows, nchunk), :] * w_s[slot, 0, r]
            for k in range(1, TOP_K):
                acc = acc + ybuf[slot, pl.ds(r * tok_rows + k * nchunk, nchunk), :] * w_s[slot, k, r]
            rbuf[pl.ds(r * nchunk, nchunk), :] = acc
            return 0

        lax.fori_loop(0, tm, body, 0, unroll=2)

    @pl.when(i == 0)
    def _():
        for cp in idx_copies(0, 0):
            cp.start()
        for cp in idx_copies(0, 0):
            cp.wait()
        issue_gathers(0)

        @pl.when(nsteps > 1)
        def _():
            for cp in idx_copies(1, 1):
                cp.start()

    for cur in range(2):
        nxt = 1 - cur

        @pl.when(i % 2 == cur)
        def _():
            @pl.when(i + 1 < nsteps)
            def _():
                for cp in idx_copies(i + 1, nxt):
                    cp.wait()
                issue_gathers(nxt)

            pltpu.make_async_copy(ybuf.at[cur], ybuf.at[cur], gsem.at[cur]).wait()
            combine(cur)

            @pl.when(i + 2 < nsteps)
            def _():
                for cp in idx_copies(i + 2, cur):
                    cp.start()

    routed = jnp.concatenate([rbuf[pl.ds(c, tm, stride=nchunk), :] for c in range(nchunk)], axis=1)

    x1 = x1_ref[...]
    xb = x1.astype(BF16)
    gate = _dot(xb, wsg_ref[...])
    up = _dot(xb, wsu_ref[...])
    shared = _dot(((gate * _sigmoid(gate)) * up).astype(BF16), wsd_ref[...])
    o_ref[...] = _layer_norm(alpha * x1 + (routed + shared), lg_ref[...], lb_ref[...])


def _final(ys, dst_t, w_t, x1, wsg, wsu, wsd, lg, lb, alpha, tm):
    t, d = x1.shape
    nchunk = d // LANES
    ds = wsg.shape[1]
    nsteps = t // tm
    full = lambda i: (0, 0)
    return pl.pallas_call(
        functools.partial(_final_kernel, tm=tm, nchunk=nchunk, alpha=alpha, nsteps=nsteps),
        grid=(nsteps,),
        in_specs=[pl.BlockSpec(memory_space=pl.ANY)] * 3 + [
            pl.BlockSpec((tm, d), lambda i: (i, 0)),
            pl.BlockSpec((d, ds), full), pl.BlockSpec((d, ds), full), pl.BlockSpec((ds, d), full),
            pl.BlockSpec((1, d), full), pl.BlockSpec((1, d), full)],
        out_specs=pl.BlockSpec((tm, d), lambda i: (i, 0)),
        out_shape=jax.ShapeDtypeStruct((t, d), F32),
        scratch_shapes=[pltpu.VMEM((2, tm * TOP_K * nchunk, LANES), F32),
                        pltpu.VMEM((tm * nchunk, LANES), F32),
                        pltpu.SMEM((2, TOP_K, tm), jnp.int32),
                        pltpu.SMEM((2, TOP_K, tm), F32),
                        pltpu.SemaphoreType.DMA((2,)),
                        pltpu.SemaphoreType.DMA((2,))],
        compiler_params=pltpu.CompilerParams(
            dimension_semantics=("arbitrary",), vmem_limit_bytes=VMEM_LIMIT),
        name="combine_shared_ln2",
    )(dst_t, w_t, ys, x1, wsg, wsu, wsd, lg, lb)


def _block_tables(counts, n_rows):
    mb = MOE_BLOCK
    n_exp = counts.shape[0]
    padded = (counts + mb - 1) // mb * mb
    pad_end = jnp.cumsum(padded)
    pad_start = pad_end - padded
    valid_end = pad_start + counts
    n_blocks = n_rows // mb
    first_row = jnp.arange(n_blocks, dtype=jnp.int32) * mb
    block_expert = jnp.minimum(
        jnp.sum((pad_end[None, :] <= first_row[:, None]).astype(jnp.int32), axis=1), n_exp - 1)
    block_valid = jnp.clip(valid_end[block_expert] - first_row, 0, mb)
    nused = (pad_end[-1:] // mb).astype(jnp.int32)
    i32 = lambda a: a.astype(jnp.int32)
    return i32(pad_start), i32(pad_end), i32(valid_end), block_expert, i32(block_valid), nused


def _tile(n, pref):
    while n % pref:
        pref //= 2
    return pref


def kernel(x, w_in, lower_bounds, hg_norm_g, w_branch_hg, w_branch_sb, w_out, ln1_g, ln1_b,
           w_router, router_bias, w_exp_gate, w_exp_up, w_exp_down,
           w_sh_gate, w_sh_up, w_sh_down, ln2_g, ln2_b):
    depth = w_in.shape[0]
    assert depth == 1, "single-layer block only"
    batch, seq, d = x.shape
    t = batch * seq
    n_exp = w_router.shape[-1]
    alpha = (2.0 * depth) ** 0.25
    nchunk = d // LANES

    lb = jnp.cumsum(jax.nn.softmax(lower_bounds.astype(F32), axis=0), axis=0)[0]
    xf = x.reshape(t, d)

    w = w_in[0].astype(BF16)
    c = d
    o_hg = _hgrn(xf, w[:, 0:4 * c], lb, hg_norm_g[0].astype(F32), batch, seq, _tile(seq, 256))
    pbf = _matmul(xf, w[:, 4 * c:7 * c], BF16, _tile(t, 2048), 1024)
    o_sb = _sb_attention(pbf, batch, seq, _tile(seq, 256), 4)

    x1, x1s = _merge(o_hg, o_sb, xf, w[:, 7 * c:9 * c],
                     w_branch_hg[0].astype(BF16), w_branch_sb[0].astype(BF16), w_out[0].astype(BF16),
                     ln1_g[0].reshape(1, d).astype(F32), ln1_b[0].reshape(1, d).astype(F32),
                     alpha, _tile(t, 512))

    idx_t, rank_t, w_t, cnt = _route(x1, w_router[0], router_bias[0], _tile(t, 256))
    p_rows = t * TOP_K + n_exp * MOE_BLOCK
    pstart, pend, vend, block_expert, block_valid, nused = _block_tables(cnt[:, 0].astype(jnp.int32), p_rows)
    dstx_t, dsty_t = _dest_rows(idx_t, rank_t, pstart, nchunk, _tile(t, 256))
    xs = _dispatch(x1s, dstx_t, pstart, pend, vend, p_rows, _tile(t, 512))
    ys = _experts(xs, block_expert, nused, block_valid, w_exp_gate[0], w_exp_up[0], w_exp_down[0])
    out = _final(ys, dsty_t, w_t, x1,
                 w_sh_gate[0].astype(BF16), w_sh_up[0].astype(BF16), w_sh_down[0].astype(BF16),
                 ln2_g[0].reshape(1, d).astype(F32), ln2_b[0].reshape(1, d).astype(F32),
                 alpha, _tile(t, 512))
    return out.reshape(batch, seq, d)
```

```python
import functools

import jax
import jax.numpy as jnp
from jax import lax
from jax.experimental import pallas as pl
from jax.experimental.pallas import tpu as pltpu

F32 = jnp.float32
BF16 = jnp.bfloat16

LANES = 128
SUBLANES = 8
VMEM_LIMIT = 48 * 1024 * 1024

HG_HEADS = 8
HG_CHUNK = 32
SB_HEADS = 8
N_GROUPS = 8
TOPK_GROUPS = 4
TOP_K = 8
ROUTED_SCALE = 2.5
MOE_BLOCK = 512
MOE_SUB = 512
LN_EPS = 1e-5
RMS_EPS = 1e-6
STICK_DEAD = 110.0

NT_DIMS = (((1,), (1,)), ((), ()))
TN_DIMS = (((0,), (0,)), ((), ()))


def _dot(a, b):
    return jnp.dot(a, b, preferred_element_type=F32)


def _sigmoid(x):
    return 1.0 / (1.0 + jnp.exp(-x))


def _split2(x):
    hi = x.astype(BF16)
    lo = (x - hi.astype(F32)).astype(BF16)
    return hi, lo


def _layer_norm(r, g, b):
    mu = jnp.mean(r, axis=-1, keepdims=True)
    d = r - mu
    var = jnp.mean(d * d, axis=-1, keepdims=True)
    return d * lax.rsqrt(var + LN_EPS) * g + b


def _mm_kernel(x_ref, w_ref, o_ref):
    o_ref[...] = _dot(x_ref[...].astype(BF16), w_ref[...]).astype(o_ref.dtype)


def _matmul(x, w, out_dtype, tm, tn):
    m, k = x.shape
    n = w.shape[1]
    return pl.pallas_call(
        _mm_kernel,
        grid=(m // tm, n // tn),
        in_specs=[pl.BlockSpec((tm, k), lambda i, j: (i, 0)),
                  pl.BlockSpec((k, tn), lambda i, j: (0, j))],
        out_specs=pl.BlockSpec((tm, tn), lambda i, j: (i, j)),
        out_shape=jax.ShapeDtypeStruct((m, n), out_dtype),
        compiler_params=pltpu.CompilerParams(
            dimension_semantics=("parallel", "arbitrary"), vmem_limit_bytes=VMEM_LIMIT),
        name="in_proj",
    )(x, w)


def _hgrn_kernel(x_ref, w_ref, lb_ref, g_ref, o_ref, st_ref, *, ts, chunk, heads):
    @pl.when(pl.program_id(1) == 0)
    def _():
        st_ref[...] = jnp.zeros_like(st_ref)

    shift = chunk.bit_length() - 1
    row = lax.broadcasted_iota(jnp.int32, (ts, ts), 0)
    col = lax.broadcasted_iota(jnp.int32, (ts, ts), 1)
    same = (row >> shift) == (col >> shift)
    causal = jnp.logical_and(same, col <= row)
    tri = jnp.where(causal, 1.0, 0.0).astype(BF16)
    ones = jnp.where(same, 1.0, 0.0).astype(BF16)
    d = LANES
    hd = heads * d

    xb = x_ref[...].astype(BF16)
    hq_all = _dot(xb, w_ref[:, 0:hd])
    hf_all = _dot(xb, w_ref[:, hd:2 * hd])
    hi_all = _dot(xb, w_ref[:, 2 * hd:3 * hd]).astype(BF16)
    hg_all = _dot(xb, w_ref[:, 3 * hd:4 * hd])

    for hh in range(heads):
        cols = slice(hh * d, (hh + 1) * d)
        lb = lb_ref[:, cols]
        hf = hf_all[:, cols]
        log_f = jnp.log(lb + (1.0 - lb) * _sigmoid(hf))
        k_in = (1.0 - lb) * _sigmoid(-hf)

        parts = jnp.concatenate(_split2(log_f), axis=1)
        cs = _dot(tri, parts)
        tot = _dot(ones, parts)
        b = cs[:, :d] + cs[:, d:]
        b_end = tot[:, :d] + tot[:, d:]

        q = hq_all[:, cols]
        q_start = (q * jnp.exp(b)).astype(BF16)
        q_end = (q * jnp.exp(b - b_end)).astype(BF16)
        k_end = (k_in * jnp.exp(b_end - b)).astype(BF16)
        v = hi_all[:, cols]

        scores = lax.dot_general(q_end, k_end, NT_DIMS, preferred_element_type=F32)
        scores = jnp.where(causal, scores, 0.0)
        o_intra = _dot(scores.astype(BF16), v)

        dec = jnp.exp(b_end)
        st = st_ref[hh]
        outs = []
        for c in range(ts // chunk):
            lo = c * chunk
            outs.append(lax.dot_general(q_start[lo:lo + chunk], st.astype(BF16), NT_DIMS,
                                        preferred_element_type=F32))
            kv = lax.dot_general(v[lo:lo + chunk], k_end[lo:lo + chunk], TN_DIMS,
                                 preferred_element_type=F32)
            st = dec[lo:lo + 1, :] * st + kv
        st_ref[hh] = st
        o = o_intra + jnp.concatenate(outs, axis=0)

        o = o * lax.rsqrt(jnp.mean(o * o, axis=-1, keepdims=True) + RMS_EPS)
        o = o * g_ref[:, cols]
        hg = hg_all[:, cols]
        o_ref[:, cols] = (o * (hg * _sigmoid(hg))).astype(o_ref.dtype)


def _hgrn(x, w_hg, lb, g, batch, seq, ts):
    t, dm = x.shape
    h = HG_HEADS
    hd = h * LANES
    ns = seq // ts
    full = lambda b, s: (0, 0)
    return pl.pallas_call(
        functools.partial(_hgrn_kernel, ts=ts, chunk=HG_CHUNK, heads=h),
        grid=(batch, ns),
        in_specs=[pl.BlockSpec((ts, dm), lambda b, s: (b * ns + s, 0)),
                  pl.BlockSpec((dm, 4 * hd), full),
                  pl.BlockSpec((1, hd), full),
                  pl.BlockSpec((1, hd), full)],
        out_specs=pl.BlockSpec((ts, hd), lambda b, s: (b * ns + s, 0)),
        out_shape=jax.ShapeDtypeStruct((t, hd), BF16),
        scratch_shapes=[pltpu.VMEM((h, LANES, LANES), F32)],
        compiler_params=pltpu.CompilerParams(
            dimension_semantics=("parallel", "arbitrary"), vmem_limit_bytes=VMEM_LIMIT),
        name="hgrn2",
    )(x, w_hg, lb.reshape(1, hd), g.reshape(1, hd))


def _sb_kernel(q_ref, k_ref, v_ref, u_ref, o_ref, *, tq, scale, hp):
    i = pl.program_id(2)
    u = u_ref[...]
    rep = tq // LANES
    d = LANES

    def block(hh, j, carry, acc, masked, live=None):
        cols = slice(hh * d, (hh + 1) * d)
        start = pl.multiple_of(j * tq, tq)
        q = q_ref[:, cols]
        kj = k_ref[pl.ds(start, tq), cols]
        vj = v_ref[pl.ds(start, tq), cols]
        z = lax.dot_general(q, kj, NT_DIMS, preferred_element_type=F32) * scale
        sp = jnp.maximum(z, 0.0) + jnp.log(1.0 + jnp.exp(-jnp.abs(z)))
        if masked:
            row = lax.broadcasted_iota(jnp.int32, (tq, tq), 0)
            col = lax.broadcasted_iota(jnp.int32, (tq, tq), 1)
            before = col < row
            drop = jnp.where(before, sp, 0.0)
        elif live is not None:
            drop = sp * live
        else:
            drop = sp
        hi = drop.astype(BF16)
        lo = (drop - hi.astype(F32)).astype(BF16)
        later = _dot(hi, u) + _dot(lo, u)
        stick = later + jnp.concatenate([carry] * rep, axis=1)
        w = jnp.exp((z - sp) - stick)
        if masked:
            w = jnp.where(before, w, 0.0)
        elif live is not None:
            w = w * live
        acc = acc + _dot(w.astype(BF16), vj)
        carry = carry + jnp.broadcast_to(later[:, 0:1] + drop[:, 0:1], (tq, LANES))
        return carry, acc

    def some_row_alive(carries):
        m = jnp.min(carries[0])
        for c in carries[1:]:
            m = jnp.minimum(m, jnp.min(c))
        return (m < STICK_DEAD).astype(jnp.int32)

    zero = jnp.zeros((tq, LANES), F32)
    live = (i > 0).astype(F32)
    prev = jnp.maximum(i - 1, 0)
    carries, accs = [], []
    for hh in range(hp):
        carry, acc = block(hh, i, zero, zero, True)
        carry, acc = block(hh, prev, carry, acc, False, live=live)
        carries.append(carry)
        accs.append(acc)

    def cond(state):
        return jnp.logical_and(state[0] >= 0, state[1] > 0)

    def body(state):
        j, _, carries, accs = state
        out = [block(hh, j, carries[hh], accs[hh], False) for hh in range(hp)]
        carries = tuple(o[0] for o in out)
        accs = tuple(o[1] for o in out)
        return j - 1, some_row_alive(carries), carries, accs

    _, _, carries, accs = lax.while_loop(
        cond, body, (i - 2, some_row_alive(carries), tuple(carries), tuple(accs)))
    for hh in range(hp):
        o_ref[:, hh * d:(hh + 1) * d] = accs[hh].astype(o_ref.dtype)


def _sb_attention(pbf, batch, seq, tq, hp):
    t = batch * seq
    h = SB_HEADS
    ng = h // hp
    w = hp * LANES
    nq = seq // tq
    r = jnp.arange(tq)
    u = (r[:, None] > r[None, :]).astype(BF16)
    return pl.pallas_call(
        functools.partial(_sb_kernel, tq=tq, scale=float(LANES) ** -0.5, hp=hp),
        grid=(batch, ng, nq),
        in_specs=[pl.BlockSpec((tq, w), lambda b, hg, i: (b * nq + i, hg)),
                  pl.BlockSpec((seq, w), lambda b, hg, i: (b, ng + hg)),
                  pl.BlockSpec((seq, w), lambda b, hg, i: (b, 2 * ng + hg)),
                  pl.BlockSpec((tq, tq), lambda b, hg, i: (0, 0))],
        out_specs=pl.BlockSpec((tq, w), lambda b, hg, i: (b * nq + i, hg)),
        out_shape=jax.ShapeDtypeStruct((t, h * LANES), BF16),
        compiler_params=pltpu.CompilerParams(
            dimension_semantics=("parallel", "parallel", "arbitrary"), vmem_limit_bytes=VMEM_LIMIT),
        name="stick_breaking",
    )(pbf, pbf, pbf, u)


def _merge_kernel(ohg_ref, osb_ref, x_ref, wgate_ref, wbh_ref, wbs_ref, wo_ref,
                  lg_ref, lb_ref, x1_ref, x1s_ref, *, alpha, tm, nchunk):
    x = x_ref[...]
    xb = x.astype(BF16)
    d = x.shape[1]
    g_hg = _dot(xb, wgate_ref[:, 0:d])
    g_sb = _dot(xb, wgate_ref[:, d:2 * d])
    y_hg = _dot(ohg_ref[...], wbh_ref[...])
    y_sb = _dot(osb_ref[...], wbs_ref[...])
    merged = _sigmoid(g_hg) * y_hg + _sigmoid(g_sb) * y_sb
    hmix = _dot(merged.astype(BF16), wo_ref[...])
    x1 = _layer_norm(alpha * x + hmix, lg_ref[...], lb_ref[...])
    x1_ref[...] = x1
    half = nchunk // 2
    for c in range(half):
        lo = x1[:, c * LANES:(c + 1) * LANES].astype(BF16).astype(F32)
        hi = x1[:, (c + half) * LANES:(c + half + 1) * LANES].astype(BF16).astype(F32)
        word = pltpu.bitcast(hi, jnp.uint32) | (pltpu.bitcast(lo, jnp.uint32) >> 16)
        x1s_ref[pl.ds(c, tm, stride=half), :] = word


def _merge(o_hg, o_sb, x, wgate, wbh, wbs, wo, lg, lb, alpha, tm):
    t, d = x.shape
    nchunk = d // LANES
    row = lambda off: (lambda i: (i, off))
    full = lambda i: (0, 0)
    return pl.pallas_call(
        functools.partial(_merge_kernel, alpha=alpha, tm=tm, nchunk=nchunk),
        grid=(t // tm,),
        in_specs=[pl.BlockSpec((tm, d), row(0)), pl.BlockSpec((tm, d), row(0)),
                  pl.BlockSpec((tm, d), row(0)),
                  pl.BlockSpec((d, 2 * d), full),
                  pl.BlockSpec((d, d), full), pl.BlockSpec((d, d), full), pl.BlockSpec((d, d), full),
                  pl.BlockSpec((1, d), full), pl.BlockSpec((1, d), full)],
        out_specs=[pl.BlockSpec((tm, d), row(0)), pl.BlockSpec((tm * nchunk // 2, LANES), row(0))],
        out_shape=[jax.ShapeDtypeStruct((t, d), F32),
                   jax.ShapeDtypeStruct((t * nchunk // 2, LANES), jnp.uint32)],
        compiler_params=pltpu.CompilerParams(
            dimension_semantics=("parallel",), vmem_limit_bytes=VMEM_LIMIT),
        name="merge_ln1",
    )(o_hg, o_sb, x, wgate, wbh, wbs, wo, lg, lb)


def _route_kernel(x_ref, wh_ref, wl_ref, bias_ref, u_ref, ones_ref,
                  idx_ref, rank_ref, w_ref, cnt_ref, carry, *, tile, n_exp):
    @pl.when(pl.program_id(0) == 0)
    def _():
        carry[...] = jnp.zeros_like(carry)

    x = x_ref[...]
    xh = x.astype(BF16)
    xl = (x - xh.astype(F32)).astype(BF16)
    wh = wh_ref[...]
    nt = functools.partial(lax.dot_general, dimension_numbers=NT_DIMS, preferred_element_type=F32)
    logits = nt(wh, xh) + nt(wh, xl) + nt(wl_ref[...], xh)
    scores = _sigmoid(logits)
    biased = scores + bias_ref[...]

    neg_inf = -jnp.inf
    gsz = n_exp // N_GROUPS
    groups = [biased[g * gsz:(g + 1) * gsz] for g in range(N_GROUPS)]
    gscore = []
    for xg in groups:
        m1 = jnp.max(xg, axis=0, keepdims=True)
        n1 = jnp.sum(jnp.where(xg == m1, 1.0, 0.0), axis=0, keepdims=True)
        m2 = jnp.max(jnp.where(xg < m1, xg, neg_inf), axis=0, keepdims=True)
        gscore.append(m1 + jnp.where(n1 >= 2.0, m1, m2))
    masked = []
    for g in range(N_GROUPS):
        beaten_by = jnp.zeros_like(gscore[g])
        for h in range(N_GROUPS):
            if h < g:
                beaten_by = beaten_by + jnp.where(gscore[h] >= gscore[g], 1.0, 0.0)
            elif h > g:
                beaten_by = beaten_by + jnp.where(gscore[h] > gscore[g], 1.0, 0.0)
        keep = jnp.broadcast_to(beaten_by, (gsz, tile)) < float(TOPK_GROUPS)
        masked.append(jnp.where(keep, groups[g], jnp.finfo(F32).min))
    cur = jnp.concatenate(masked, axis=0)

    eid = lax.broadcasted_iota(jnp.int32, (n_exp, tile), 0).astype(F32)
    chosen = jnp.zeros((n_exp, tile), F32)
    idxs, ws = [], []
    for _ in range(TOP_K):
        m = jnp.max(cur, axis=0, keepdims=True)
        ik = jnp.min(jnp.where(cur == m, eid, float(n_exp)), axis=0, keepdims=True)
        sel = eid == ik
        idxs.append(ik)
        ws.append(jnp.sum(jnp.where(sel, scores, 0.0), axis=0, keepdims=True))
        chosen = chosen + jnp.where(sel, 1.0, 0.0)
        cur = jnp.where(sel, neg_inf, cur)

    chosen_b = chosen.astype(BF16)
    run = carry[...]
    rank = _dot(chosen_b, u_ref[...]) + jnp.concatenate([run] * (tile // LANES), axis=1)
    ranks = [jnp.sum(jnp.where(eid == ik, rank, 0.0), axis=0, keepdims=True) for ik in idxs]
    run = run + _dot(chosen_b, ones_ref[...])
    carry[...] = run
    cnt_ref[...] = run

    wsum = ws[0]
    for wk in ws[1:]:
        wsum = wsum + wk
    idx_ref[...] = jnp.concatenate(idxs, axis=0).astype(jnp.int32)
    rank_ref[...] = jnp.concatenate(ranks, axis=0).astype(jnp.int32)
    w_ref[...] = jnp.concatenate([wk / wsum * ROUTED_SCALE for wk in ws], axis=0)


def _route(x1, w_router, router_bias, tile):
    t, d = x1.shape
    n_exp = w_router.shape[1]
    wt = w_router.astype(F32).T
    wh = wt.astype(BF16)
    wl = (wt - wh.astype(F32)).astype(BF16)
    r = jnp.arange(tile)
    u = (r[:, None] < r[None, :]).astype(BF16)
    ones = jnp.ones((tile, LANES), BF16)
    full = lambda i: (0, 0)
    tok = lambda i: (0, i)
    return pl.pallas_call(
        functools.partial(_route_kernel, tile=tile, n_exp=n_exp),
        grid=(t // tile,),
        in_specs=[pl.BlockSpec((tile, d), lambda i: (i, 0)),
                  pl.BlockSpec((n_exp, d), full), pl.BlockSpec((n_exp, d), full),
                  pl.BlockSpec((n_exp, 1), full),
                  pl.BlockSpec((tile, tile), full), pl.BlockSpec((tile, LANES), full)],
        out_specs=[pl.BlockSpec((TOP_K, tile), tok), pl.BlockSpec((TOP_K, tile), tok),
                   pl.BlockSpec((TOP_K, tile), tok), pl.BlockSpec((n_exp, LANES), full)],
        out_shape=[jax.ShapeDtypeStruct((TOP_K, t), jnp.int32), jax.ShapeDtypeStruct((TOP_K, t), jnp.int32),
                   jax.ShapeDtypeStruct((TOP_K, t), F32), jax.ShapeDtypeStruct((n_exp, LANES), F32)],
        scratch_shapes=[pltpu.VMEM((n_exp, LANES), F32)],
        compiler_params=pltpu.CompilerParams(
            dimension_semantics=("arbitrary",), vmem_limit_bytes=VMEM_LIMIT),
        name="router_topk",
    )(x1, wh, wl, router_bias.astype(F32).reshape(n_exp, 1), u, ones)


def _dest_kernel(idx_ref, rank_ref, pstart_ref, dstx_ref, dsty_ref, *, tile, n_exp, nchunk):
    eid = lax.broadcasted_iota(jnp.int32, (n_exp, tile), 0)
    pstart = jnp.broadcast_to(pstart_ref[...], (n_exp, tile))
    rows = []
    for k in range(TOP_K):
        hit = eid == idx_ref[k:k + 1, :]
        base = jnp.sum(jnp.where(hit, pstart, 0.0), axis=0, keepdims=True)
        rows.append(base.astype(jnp.int32) + rank_ref[k:k + 1, :])
    dst = jnp.concatenate(rows, axis=0)
    dstx_ref[...] = dst * (nchunk // 2)
    dsty_ref[...] = dst * nchunk


def _dest_rows(idx_t, rank_t, pstart, nchunk, tile):
    t = idx_t.shape[1]
    n_exp = pstart.shape[0]
    tok = lambda i: (0, i)
    return pl.pallas_call(
        functools.partial(_dest_kernel, tile=tile, n_exp=n_exp, nchunk=nchunk),
        grid=(t // tile,),
        in_specs=[pl.BlockSpec((TOP_K, tile), tok), pl.BlockSpec((TOP_K, tile), tok),
                  pl.BlockSpec((n_exp, 1), lambda i: (0, 0))],
        out_specs=[pl.BlockSpec((TOP_K, tile), tok)] * 2,
        out_shape=[jax.ShapeDtypeStruct((TOP_K, t), jnp.int32)] * 2,
        compiler_params=pltpu.CompilerParams(
            dimension_semantics=("parallel",), vmem_limit_bytes=VMEM_LIMIT),
        name="dest_rows",
    )(idx_t, rank_t, pstart.astype(F32).reshape(n_exp, 1))


def _dispatch_kernel(pstart_ref, pend_ref, vend_ref, dst_hbm, x_hbm, xs_hbm,
                     dst_s, xbuf, zbuf, isem, lsem, dsem, zsem,
                     *, td, nchunk, mb, sub, n_exp, nsteps, n_blocks):
    i = pl.program_id(0)
    rows = sub * nchunk
    trows = td * nchunk

    def idx_copy(step, slot):
        return pltpu.make_async_copy(dst_hbm.at[:, pl.ds(step * td, td)], dst_s.at[slot], isem.at[slot])

    def tile_load(step, slot):
        start = pl.multiple_of(step * trows, trows)
        return pltpu.make_async_copy(x_hbm.at[pl.ds(start, trows), :], xbuf.at[slot], lsem.at[slot])

    def wait_rows(slot):
        for _ in range(TOP_K):
            pltpu.make_async_copy(xbuf.at[slot], xbuf.at[slot], dsem.at[slot]).wait()

    @pl.when(i == 0)
    def _():
        idx_copy(0, 0).start()
        tile_load(0, 0).start()
        zbuf[...] = jnp.zeros_like(zbuf)
        pieces = mb // sub

        def zero_copy(e, p):
            start = pl.multiple_of((pend_ref[e] - (p + 1) * sub) * nchunk, nchunk)
            return pltpu.make_async_copy(zbuf, xs_hbm.at[pl.ds(start, rows), :], zsem)

        def needs_zero(e, p):
            return pend_ref[e] - vend_ref[e] > p * sub

        def zstart(e, _):
            for p in range(pieces):
                @pl.when(needs_zero(e, p))
                def _():
                    zero_copy(e, p).start()
            return 0

        def zwait(e, _):
            for p in range(pieces):
                @pl.when(needs_zero(e, p))
                def _():
                    zero_copy(e, p).wait()
            return 0

        lax.fori_loop(0, n_exp, zstart, 0)
        lax.fori_loop(0, n_exp, zwait, 0)

        def tail_copy(b):
            start = pl.multiple_of(b * rows, rows)
            return pltpu.make_async_copy(zbuf, xs_hbm.at[pl.ds(start, rows), :], zsem)

        def tstart(b, _):
            tail_copy(b).start()
            return 0

        def twait(b, _):
            tail_copy(b).wait()
            return 0

        first_unused = pend_ref[n_exp - 1] // sub
        lax.fori_loop(first_unused, n_blocks * pieces, tstart, 0)
        lax.fori_loop(first_unused, n_blocks * pieces, twait, 0)

    for cur in range(3):
        nxt = (cur + 1) % 3

        @pl.when(i % 3 == cur)
        def _():
            idx_copy(i, cur).wait()

            @pl.when(i >= 2)
            def _():
                wait_rows(nxt)

            @pl.when(i + 1 < nsteps)
            def _():
                idx_copy(i + 1, nxt).start()
                tile_load(i + 1, nxt).start()

            tile_load(i, cur).wait()

            def body(r, _):
                src = pl.multiple_of(r * nchunk, nchunk)
                for k in range(TOP_K):
                    dst = pl.multiple_of(dst_s[cur, k, r], nchunk)
                    pltpu.make_async_copy(xbuf.at[cur, pl.ds(src, nchunk), :],
                                          xs_hbm.at[pl.ds(dst, nchunk), :], dsem.at[cur]).start(priority=k % 2)
                return 0

            lax.fori_loop(0, td, body, 0, unroll=2)

            @pl.when(i == nsteps - 1)
            def _():
                @pl.when(i >= 1)
                def _():
                    wait_rows((cur + 2) % 3)

                wait_rows(cur)


def _dispatch(x2, dst_t, pstart, pend, vend, p_rows, td):
    n_exp = pstart.shape[0]
    t = dst_t.shape[1]
    nchunk = x2.shape[0] // t
    nsteps = t // td
    grid_spec = pltpu.PrefetchScalarGridSpec(
        num_scalar_prefetch=3,
        grid=(nsteps,),
        in_specs=[pl.BlockSpec(memory_space=pl.ANY)] * 2,
        out_specs=pl.BlockSpec(memory_space=pl.ANY),
        scratch_shapes=[pltpu.SMEM((3, TOP_K, td), jnp.int32),
                        pltpu.VMEM((3, td * nchunk, LANES), x2.dtype),
                        pltpu.VMEM((MOE_SUB * nchunk, LANES), x2.dtype),
                        pltpu.SemaphoreType.DMA((3,)),
                        pltpu.SemaphoreType.DMA((3,)),
                        pltpu.SemaphoreType.DMA((3,)),
                        pltpu.SemaphoreType.DMA],
    )
    return pl.pallas_call(
        functools.partial(_dispatch_kernel, td=td, nchunk=nchunk, mb=MOE_BLOCK, sub=MOE_SUB, n_exp=n_exp,
                          nsteps=nsteps, n_blocks=p_rows // MOE_BLOCK),
        grid_spec=grid_spec,
        out_shape=jax.ShapeDtypeStruct((p_rows * nchunk, LANES), x2.dtype),
        compiler_params=pltpu.CompilerParams(
            dimension_semantics=("arbitrary",), vmem_limit_bytes=VMEM_LIMIT),
        name="moe_dispatch",
    )(pstart, pend, vend, dst_t, x2)


def _experts_kernel(bexp_ref, nused_ref, bvalid_ref, xs_hbm, wg_ref, wu_ref, wd_ref, ys_ref,
                    xring, wg_b, wu_b, wd_b, xsem, *, mb, sub, nchunk):
    i = pl.program_id(0)
    half = nchunk // 2
    xrows = mb * half
    nused = nused_ref[0]
    used = i < nused

    def fetch(b, slot):
        start = pl.multiple_of(b * xrows, xrows)
        return pltpu.make_async_copy(xs_hbm.at[pl.ds(start, xrows), :], xring.at[slot], xsem.at[slot])

    @pl.when(i == 0)
    def _():
        fetch(0, 0).start()

        @pl.when(nused > 1)
        def _():
            fetch(1, 1).start()

    @pl.when(used)
    def _():
        new_expert = jnp.logical_or(i == 0, bexp_ref[i] != bexp_ref[jnp.maximum(i - 1, 0)])

        @pl.when(new_expert)
        def _():
            wg_b[...] = wg_ref[0].astype(BF16)
            wu_b[...] = wu_ref[0].astype(BF16)
            wd_b[...] = wd_ref[0].astype(BF16)

    for cur in range(3):
        @pl.when(jnp.logical_and(used, i % 3 == cur))
        def _():
            @pl.when(i + 2 < nused)
            def _():
                fetch(i + 2, (cur + 2) % 3).start()

            fetch(i, cur).wait()
            _experts_block(bvalid_ref[i], xring.at[cur], ys_ref, wg_b, wu_b, wd_b,
                           mb=mb, sub=sub, nchunk=nchunk)

    @pl.when(jnp.logical_not(used))
    def _():
        ys_ref[...] = jnp.zeros_like(ys_ref)


def _experts_block(valid, xs_ref, ys_ref, wg_b, wu_b, wd_b, *, mb, sub, nchunk):
    half = nchunk // 2
    for p in range(mb // sub):
        live = valid > p * sub

        @pl.when(live)
        def _():
            x0 = p * sub * half
            words = [xs_ref[pl.ds(x0 + c, sub, stride=half), :] for c in range(half)]
            lows = [pltpu.bitcast(wv << 16, F32) for wv in words]
            highs = [pltpu.bitcast(wv & jnp.uint32(0xFFFF0000), F32) for wv in words]
            xb = jnp.concatenate(lows + highs, axis=1).astype(BF16)
            gate = _dot(xb, wg_b[...])
            up = _dot(xb, wu_b[...])
            hid = (gate * _sigmoid(gate)) * up
            y = _dot(hid.astype(BF16), wd_b[...])
            y0 = p * sub * nchunk
            for c in range(nchunk):
                ys_ref[pl.ds(y0 + c, sub, stride=nchunk), :] = y[:, c * LANES:(c + 1) * LANES]

        @pl.when(jnp.logical_not(live))
        def _():
            ys_ref[pl.ds(p * sub * nchunk, sub * nchunk), :] = jnp.zeros((sub * nchunk, LANES), F32)


def _experts(xs, block_expert, nused, block_valid, w_gate, w_up, w_down):
    e, d, de = w_gate.shape
    nchunk = d // LANES
    mb = MOE_BLOCK
    rows = mb * nchunk
    xrows = rows // 2
    n_blocks = xs.shape[0] // xrows
    blk = lambda i, be, nu, bv: (jnp.minimum(i, nu[0] - 1), 0)
    wsel = lambda i, be, nu, bv: (be[i], 0, 0)
    grid_spec = pltpu.PrefetchScalarGridSpec(
        num_scalar_prefetch=3,
        grid=(n_blocks,),
        in_specs=[pl.BlockSpec(memory_space=pl.ANY),
                  pl.BlockSpec((1, d, de), wsel), pl.BlockSpec((1, d, de), wsel),
                  pl.BlockSpec((1, de, d), wsel)],
        out_specs=pl.BlockSpec((rows, LANES), lambda i, be, nu, bv: (i, 0)),
        scratch_shapes=[pltpu.VMEM((3, xrows, LANES), xs.dtype),
                        pltpu.VMEM((d, de), BF16), pltpu.VMEM((d, de), BF16), pltpu.VMEM((de, d), BF16),
                        pltpu.SemaphoreType.DMA((3,))],
    )
    return pl.pallas_call(
        functools.partial(_experts_kernel, mb=mb, sub=MOE_SUB, nchunk=nchunk),
        grid_spec=grid_spec,
        out_shape=jax.ShapeDtypeStruct((n_blocks * rows, LANES), F32),
        compiler_params=pltpu.CompilerParams(
            dimension_semantics=("arbitrary",), vmem_limit_bytes=VMEM_LIMIT),
        name="routed_experts",
    )(block_expert, nused, block_valid, xs, w_gate, w_up, w_down)


def _final_kernel(dst_hbm, w_hbm, ys_hbm, x1_ref, wsg_ref, wsu_ref, wsd_ref,
                  lg_ref, lb_ref, o_ref, ybuf, rbuf, dst_s, w_s, isem, gsem,
                  *, tm, nchunk, alpha, nsteps):
    i = pl.program_id(0)
    tok_rows = TOP_K * nchunk

    def idx_copies(step, slot):
        cols = pl.ds(step * tm, tm)
        return (pltpu.make_async_copy(dst_hbm.at[:, cols], dst_s.at[slot], isem.at[slot]),
                pltpu.make_async_copy(w_hbm.at[:, cols], w_s.at[slot], isem.at[slot]))

    def issue_gathers(slot):
        def body(r, _):
            for k in range(TOP_K):
                src = pl.multiple_of(dst_s[slot, k, r], nchunk)
                pltpu.make_async_copy(ys_hbm.at[pl.ds(src, nchunk), :],
                                      ybuf.at[slot, pl.ds(r * tok_rows + k * nchunk, nchunk), :],
                                      gsem.at[slot]).start(priority=k % 2)
            return 0

        lax.fori_loop(0, tm, body, 0, unroll=2)

    def combine(slot):
        def body(r, _):
            acc = ybuf[slot, pl.ds(r * tok_rows, nchunk), :] * w_s[slot, 0, r]
            for k in range(1, TOP_K):
                acc = acc + ybuf[slot, pl.ds(r * tok_rows + k * nchunk, nchunk), :] * w_s[slot, k, r]
            rbuf[pl.ds(r * nchunk, nchunk), :] = acc
            return 0

        lax.fori_loop(0, tm, body, 0, unroll=2)

    @pl.when(i == 0)
    def _():
        for cp in idx_copies(0, 0):
            cp.start()
        for cp in idx_copies(0, 0):
            cp.wait()
        issue_gathers(0)

        @pl.when(nsteps > 1)
        def _():
            for cp in idx_copies(1, 1):
                cp.start()

    for cur in range(2):
        nxt = 1 - cur

        @pl.when(i % 2 == cur)
        def _():
            @pl.when(i + 1 < nsteps)
            def _():
                for cp in idx_copies(i + 1, nxt):
                    cp.wait()
                issue_gathers(nxt)

            pltpu.make_async_copy(ybuf.at[cur], ybuf.at[cur], gsem.at[cur]).wait()
            combine(cur)

            @pl.when(i + 2 < nsteps)
            def _():
                for cp in idx_copies(i + 2, cur):
                    cp.start()

    routed = jnp.concatenate([rbuf[pl.ds(c, tm, stride=nchunk), :] for c in range(nchunk)], axis=1)

    x1 = x1_ref[...]
    xb = x1.astype(BF16)
    gate = _dot(xb, wsg_ref[...])
    up = _dot(xb, wsu_ref[...])
    shared = _dot(((gate * _sigmoid(gate)) * up).astype(BF16), wsd_ref[...])
    o_ref[...] = _layer_norm(alpha * x1 + (routed + shared), lg_ref[...], lb_ref[...])


def _final(ys, dst_t, w_t, x1, wsg, wsu, wsd, lg, lb, alpha, tm):
    t, d = x1.shape
    nchunk = d // LANES
    ds = wsg.shape[1]
    nsteps = t // tm
    full = lambda i: (0, 0)
    return pl.pallas_call(
        functools.partial(_final_kernel, tm=tm, nchunk=nchunk, alpha=alpha, nsteps=nsteps),
        grid=(nsteps,),
        in_specs=[pl.BlockSpec(memory_space=pl.ANY)] * 3 + [
            pl.BlockSpec((tm, d), lambda i: (i, 0)),
            pl.BlockSpec((d, ds), full), pl.BlockSpec((d, ds), full), pl.BlockSpec((ds, d), full),
            pl.BlockSpec((1, d), full), pl.BlockSpec((1, d), full)],
        out_specs=pl.BlockSpec((tm, d), lambda i: (i, 0)),
        out_shape=jax.ShapeDtypeStruct((t, d), F32),
        scratch_shapes=[pltpu.VMEM((2, tm * TOP_K * nchunk, LANES), F32),
                        pltpu.VMEM((tm * nchunk, LANES), F32),
                        pltpu.SMEM((2, TOP_K, tm), jnp.int32),
                        pltpu.SMEM((2, TOP_K, tm), F32),
                        pltpu.SemaphoreType.DMA((2,)),
                        pltpu.SemaphoreType.DMA((2,))],
        compiler_params=pltpu.CompilerParams(
            dimension_semantics=("arbitrary",), vmem_limit_bytes=VMEM_LIMIT),
        name="combine_shared_ln2",
    )(dst_t, w_t, ys, x1, wsg, wsu, wsd, lg, lb)


def _block_tables(counts, n_rows):
    mb = MOE_BLOCK
    n_exp = counts.shape[0]
    padded = (counts + mb - 1) // mb * mb
    pad_end = jnp.cumsum(padded)
    pad_start = pad_end - padded
    valid_end = pad_start + counts
    n_blocks = n_rows // mb
    first_row = jnp.arange(n_blocks, dtype=jnp.int32) * mb
    block_expert = jnp.minimum(
        jnp.sum((pad_end[None, :] <= first_row[:, None]).astype(jnp.int32), axis=1), n_exp - 1)
    block_valid = jnp.clip(valid_end[block_expert] - first_row, 0, mb)
    nused = (pad_end[-1:] // mb).astype(jnp.int32)
    i32 = lambda a: a.astype(jnp.int32)
    return i32(pad_start), i32(pad_end), i32(valid_end), block_expert, i32(block_valid), nused


def _tile(n, pref):
    while n % pref:
        pref //= 2
    return pref


def kernel(x, w_in, lower_bounds, hg_norm_g, w_branch_hg, w_branch_sb, w_out, ln1_g, ln1_b,
           w_router, router_bias, w_exp_gate, w_exp_up, w_exp_down,
           w_sh_gate, w_sh_up, w_sh_down, ln2_g, ln2_b):
    depth = w_in.shape[0]
    assert depth == 1, "single-layer block only"
    batch, seq, d = x.shape
    t = batch * seq
    n_exp = w_router.shape[-1]
    alpha = (2.0 * depth) ** 0.25
    nchunk = d // LANES

    lb = jnp.cumsum(jax.nn.softmax(lower_bounds.astype(F32), axis=0), axis=0)[0]
    xf = x.reshape(t, d)

    w = w_in[0].astype(BF16)
    c = d
    o_hg = _hgrn(xf, w[:, 0:4 * c], lb, hg_norm_g[0].astype(F32), batch, seq, _tile(seq, 256))
    pbf = _matmul(xf, w[:, 4 * c:7 * c], BF16, _tile(t, 2048), 1024)
    o_sb = _sb_attention(pbf, batch, seq, _tile(seq, 256), 4)

    x1, x1s = _merge(o_hg, o_sb, xf, w[:, 7 * c:9 * c],
                     w_branch_hg[0].astype(BF16), w_branch_sb[0].astype(BF16), w_out[0].astype(BF16),
                     ln1_g[0].reshape(1, d).astype(F32), ln1_b[0].reshape(1, d).astype(F32),
                     alpha, _tile(t, 512))

    idx_t, rank_t, w_t, cnt = _route(x1, w_router[0], router_bias[0], _tile(t, 256))
    p_rows = t * TOP_K + n_exp * MOE_BLOCK
    pstart, pend, vend, block_expert, block_valid, nused = _block_tables(cnt[:, 0].astype(jnp.int32), p_rows)
    dstx_t, dsty_t = _dest_rows(idx_t, rank_t, pstart, nchunk, _tile(t, 256))
    xs = _dispatch(x1s, dstx_t, pstart, pend, vend, p_rows, _tile(t, 512))
    ys = _experts(xs, block_expert, nused, block_valid, w_exp_gate[0], w_exp_up[0], w_exp_down[0])
    out = _final(ys, dsty_t, w_t, x1,
                 w_sh_gate[0].astype(BF16), w_sh_up[0].astype(BF16), w_sh_down[0].astype(BF16),
                 ln2_g[0].reshape(1, d).astype(F32), ln2_b[0].reshape(1, d).astype(F32),
                 alpha, _tile(t, 512))
    return out.reshape(batch, seq, d)
```
